```python
import numpy as np
import jax
import jax.numpy as jnp
from jax import lax

D_MODEL = 1024
BATCH = 2
SEQ = 16384
DEPTH = 4

GRID_W = 64
CTX_LEN = 256
HEAD_DIM = 64
NORM_EPS = 1e-6
RWKV_HEADS = 6
RWKV_DIM = RWKV_HEADS * HEAD_DIM
DECAY_LORA = 64
ICLR_LORA = 64
GATE_LORA = 160
GN_EPS = 64e-5
RWKV_SIZES = (RWKV_DIM, RWKV_DIM, RWKV_DIM, DECAY_LORA, DECAY_LORA, ICLR_LORA, ICLR_LORA, GATE_LORA)
RWKV_PROJ = sum(RWKV_SIZES)
POOL_WINDOWS = (2, 4, 8, 16)
POOL_GROUP = 64
POOL_DIM = POOL_GROUP * len(POOL_WINDOWS)
NAT_HEADS = 6
NAT_DIM = NAT_HEADS * HEAD_DIM
WIN_H = 8
WIN_W = 16
Q_BLOCK = 128
N_BRANCH = 3
IN_SIZES = (RWKV_PROJ, POOL_DIM, NAT_DIM, NAT_DIM, NAT_DIM, N_BRANCH * D_MODEL)
IN_PROJ = sum(IN_SIZES)
FFN_HIDDEN = -(-8 * D_MODEL // (3 * 256)) * 256

kernel_name = 'hybrid_rwkv7_pool_natten_prefix_dit'


def _split(t, sizes):
    cuts = np.cumsum(sizes)[:-1].tolist()
    return jnp.split(t, cuts, axis=-1)


def _rmsnorm(t, gain):
    tf = t.astype(jnp.float32)
    y = tf * lax.rsqrt(jnp.mean(tf * tf, axis=-1, keepdims=True) + NORM_EPS)
    return (y * gain.astype(jnp.float32)).astype(t.dtype)


def _modulate(h, shift, scale):
    return h * (1.0 + scale) + shift


def _shift_mix(p, mu_prev, mu_next):
    zero = jnp.zeros_like(p[:, :1])
    prev = jnp.concatenate([zero, p[:, :-1]], axis=1)
    nxt = jnp.concatenate([p[:, 1:], zero], axis=1)
    return p + mu_prev * (prev - p) + mu_next * (nxt - p)


def _rwkv_heads(t):
    return t.reshape(t.shape[:-1] + (RWKV_HEADS, HEAD_DIM))


def _rwkv_prep(p, decay_w0, decay_up, iclr_a0, iclr_up, k_k, k_a):
    pf = p.astype(jnp.float32)
    r, k, v, wd_f, wd_b, ad_f, ad_b, gd = _split(pf, RWKV_SIZES)
    wd = jnp.stack([wd_f, wd_b])
    ad = jnp.stack([ad_f, ad_b])
    log_w = -jax.nn.softplus(-(decay_w0[:, None, None, :] + jnp.einsum('zbtr,zrc->zbtc', jnp.tanh(wd), decay_up))) - 0.5
    decay = jnp.exp(-jnp.exp(log_w))
    a = jax.nn.sigmoid(iclr_a0[:, None, None, :] + jnp.einsum('zbtr,zrc->zbtc', ad, iclr_up))
    kk = _rwkv_heads(k * k_k)
    kk = kk * lax.rsqrt(jnp.maximum(jnp.sum(kk * kk, axis=-1, keepdims=True), 1e-24))
    k_dir = _rwkv_heads(k * (1.0 + (a - 1.0) * k_a))
    b = kk * _rwkv_heads(a)
    return (_rwkv_heads(decay), k_dir, _rwkv_heads(v), kk, b), _rwkv_heads(r), gd


def _dir_seqs(st, z):
    decay, k_dir, v, kk, b = st
    return tuple(jnp.swapaxes(t, 0, 1) for t in (decay[z], k_dir[z], v, kk, b[z]))


def _delta_scan(s0, seqs, r, reverse):
    def update(s, w_t, k_t, v_t, kk_t, b_t):
        sa = jnp.einsum('bhij,bhj->bhi', s, kk_t)
        return s * w_t[:, :, None, :] + v_t[..., None] * k_t[:, :, None, :] - sa[..., None] * b_t[:, :, None, :]

    if r is None:
        def step_state(s, inp):
            return update(s, *inp), None
        s_final, _ = lax.scan(step_state, s0, seqs, reverse=reverse)
        return s_final, None

    def step(s, inp):
        s = update(s, *inp[:5])
        return s, jnp.einsum('bhij,bhj->bhi', s, inp[5])
    return lax.scan(step, s0, seqs + (r,), reverse=reverse)


def _rwkv_readout(y_f, y_b, r, st, gd, gn_w, gn_b, r_k, gate_up):
    k_dir, v = st[1], st[2]
    y = jnp.swapaxes(y_f + y_b, 0, 1)
    mean = jnp.mean(y, axis=-1, keepdims=True)
    var = jnp.mean(jnp.square(y - mean), axis=-1, keepdims=True)
    y = (y - mean) * lax.rsqrt(var + GN_EPS) * _rwkv_heads(gn_w) + _rwkv_heads(gn_b)
    y = y + jnp.sum(r * r_k * (k_dir[0] + k_dir[1]), axis=-1, keepdims=True) * v
    g = jax.nn.sigmoid(gd) @ gate_up
    return y.reshape(g.shape) * g


def _multiscale_pool(p, pool_w, pool_scale):
    T = p.shape[1]
    pf = p.astype(jnp.float32)
    csum = jnp.concatenate([jnp.zeros_like(pf[:, :1]), jnp.cumsum(pf, axis=1)], axis=1)
    t = jnp.arange(T)
    groups = []
    for g, w in enumerate(POOL_WINDOWS):
        lo = jnp.clip(t - w // 2, 0, T - 1)
        hi = jnp.clip(t + (w - w // 2) - 1, 0, T - 1)
        cs_g = csum[..., g * POOL_GROUP:(g + 1) * POOL_GROUP]
        mean = (cs_g[:, hi + 1] - cs_g[:, lo]) / (hi - lo + 1).astype(jnp.float32)[None, :, None]
        groups.append(mean - pf[..., g * POOL_GROUP:(g + 1) * POOL_GROUP])
    mixed = jnp.stack(groups, axis=2)
    y = jnp.einsum('btgc,gcd->btgd', mixed, pool_w).reshape(p.shape) * pool_scale
    return y.astype(p.dtype)


def _natten_indices(T):
    rows = T // GRID_W
    kh = min(WIN_H, rows)
    t = np.arange(T)
    r, c = t // GRID_W, t % GRID_W
    rs = np.clip(r - kh // 2, 0, rows - kh)
    cs = np.clip(c - WIN_W // 2, 0, GRID_W - WIN_W)
    kr = rs[:, None, None] + np.arange(kh)[None, :, None]
    kc = cs[:, None, None] + np.arange(WIN_W)[None, None, :]
    key_idx = (kr * GRID_W + kc).reshape(T, kh * WIN_W)
    bias_idx = ((kr - r[:, None, None] + WIN_H - 1) * (2 * WIN_W - 1) + (kc - c[:, None, None] + WIN_W - 1)).reshape(T, kh * WIN_W)
    return jnp.asarray(key_idx, jnp.int32), jnp.asarray(bias_idx, jnp.int32)


def _split_heads(t):
    B, T, _ = t.shape
    return t.reshape(B, T, NAT_HEADS, HEAD_DIM).transpose(0, 2, 1, 3)


def _natten_latent(q, k, v, k_ctx, v_ctx, rpb_flat, key_idx, bias_idx):
    B, H, T, N = q.shape
    nblk = T // Q_BLOCK
    n_loc = key_idx.shape[1]
    q_blocks = q.reshape(B, H, nblk, Q_BLOCK, N).transpose(2, 0, 1, 3, 4)
    k_ix = key_idx.reshape(nblk, Q_BLOCK, n_loc)
    b_ix = bias_idx.reshape(nblk, Q_BLOCK, n_loc)

    def block(args):
        q_blk, kix, bix = args
        k_g = k[:, :, kix]
        v_g = v[:, :, kix]
        s_loc = jnp.einsum('bhqd,bhqkd->bhqk', q_blk, k_g).astype(jnp.float32) + rpb_flat[:, bix][None].astype(jnp.float32)
        s_ctx = jnp.einsum('bhqd,bhkd->bhqk', q_blk, k_ctx).astype(jnp.float32)
        prob = jax.nn.softmax(jnp.concatenate([s_loc, s_ctx], axis=-1), axis=-1).astype(v.dtype)
        return jnp.einsum('bhqk,bhqkd->bhqd', prob[..., :n_loc], v_g) + jnp.einsum('bhqk,bhkd->bhqd', prob[..., n_loc:], v_ctx)

    o = lax.map(block, (q_blocks, k_ix, b_ix))
    return o.transpose(1, 0, 3, 2, 4).reshape(B, T, H * N)


def _context_attention(q, k, v):
    s = jnp.einsum('bhqd,bhkd->bhqk', q, k).astype(jnp.float32)
    prob = jax.nn.softmax(s, axis=-1).astype(v.dtype)
    o = jnp.einsum('bhqk,bhkd->bhqd', prob, v)
    B, H, L, N = o.shape
    return o.transpose(0, 2, 1, 3).reshape(B, L, H * N)


def _merge(gates, br_a, br_b, br_c):
    g_a, g_b, g_c = jnp.split(gates, N_BRANCH, axis=-1)
    return jax.nn.sigmoid(g_a) * br_a + jax.nn.sigmoid(g_b) * br_b + jax.nn.sigmoid(g_c) * br_c


def _swiglu(h, w_ffn_in, w_ffn_out):
    gate, up = jnp.split(h @ w_ffn_in, 2, axis=-1)
    return (jax.nn.silu(gate) * up) @ w_ffn_out


def _token_mixer(h_lat, h_ctx, ctx_out, key_idx, bias_idx, w_in, mu_prev, mu_next, decay_w0, decay_up,
                 iclr_a0, iclr_up, gate_up, k_k, k_a, r_k, gn_w, gn_b, pool_w, pool_scale, q_gain,
                 k_gain, rpb_flat, w_rwkv_o, w_pool_o, w_nat_o, w_out):
    B = h_lat.shape[0]
    q_scale = HEAD_DIM ** -0.5
    rw_l, pl_l, q_l, k_l, v_l, gt_l = _split(h_lat @ w_in, IN_SIZES)
    rw_c, pl_c, q_c, k_c, v_c, gt_c = _split(h_ctx @ w_in, IN_SIZES)
    lora = (decay_w0, decay_up, iclr_a0, iclr_up, k_k, k_a)
    st_l, r_l, gd_l = _rwkv_prep(_shift_mix(rw_l, mu_prev, mu_next), *lora)
    st_c, r_c, gd_c = _rwkv_prep(_shift_mix(rw_c, mu_prev, mu_next), *lora)
    s0 = jnp.zeros((B, RWKV_HEADS, HEAD_DIM, HEAD_DIM), jnp.float32)
    r_c_t = jnp.swapaxes(r_c, 0, 1) if ctx_out else None
    s_fwd, y_fwd_c = _delta_scan(s0, _dir_seqs(st_c, 0), r_c_t, False)
    s_bwd, y_bwd_c = _delta_scan(s0, _dir_seqs(st_c, 1), r_c_t, True)
    r_l_t = jnp.swapaxes(r_l, 0, 1)
    _, y_fwd_l = _delta_scan(s_fwd, _dir_seqs(st_l, 0), r_l_t, False)
    _, y_bwd_l = _delta_scan(s_bwd, _dir_seqs(st_l, 1), r_l_t, True)
    readout = (gn_w, gn_b, r_k, gate_up)
    a_l = _rwkv_readout(y_fwd_l, y_bwd_l, r_l, st_l, gd_l, *readout).astype(h_lat.dtype)
    b_l = _multiscale_pool(pl_l, pool_w, pool_scale)
    k_ch = _rmsnorm(_split_heads(k_c), k_gain)
    v_ch = _split_heads(v_c)
    c_l = _natten_latent(_rmsnorm(_split_heads(q_l), q_gain) * q_scale, _rmsnorm(_split_heads(k_l), k_gain),
                         _split_heads(v_l), k_ch, v_ch, rpb_flat, key_idx, bias_idx)
    out_l = _merge(gt_l, a_l @ w_rwkv_o, b_l @ w_pool_o, c_l @ w_nat_o) @ w_out
    if not ctx_out:
        return out_l, None
    a_c = _rwkv_readout(y_fwd_c, y_bwd_c, r_c, st_c, gd_c, *readout).astype(h_ctx.dtype)
    b_c = _multiscale_pool(pl_c, pool_w, pool_scale)
    c_c = _context_attention(_rmsnorm(_split_heads(q_c), q_gain) * q_scale, k_ch, v_ch)
    out_c = _merge(gt_c, a_c @ w_rwkv_o, b_c @ w_pool_o, c_c @ w_nat_o) @ w_out
    return out_l, out_c


def setup_inputs(seed: int = 0) -> dict:
    key = jax.random.key(seed)
    ks = jax.random.split(key, 32)
    L, D = DEPTH, D_MODEL

    def nrm(k, shape, scale):
        return jax.random.normal(k, shape, jnp.float32) * scale

    def uni(k, shape, lo, hi):
        return jax.random.uniform(k, shape, jnp.float32, lo, hi)

    return {
        'x': nrm(ks[0], (BATCH, SEQ, D), 1.0),
        'c': nrm(ks[1], (BATCH, D), 1.0),
        'ctx': nrm(ks[2], (BATCH, CTX_LEN, D), 1.0),
        'c_ctx': nrm(ks[3], (D,), 1.0),
        'w_mod': nrm(ks[4], (L, D, 6 * D), 0.5 * D ** -0.5),
        'b_mod': nrm(ks[5], (L, 6 * D), 0.01),
        'norm1': 1.0 + nrm(ks[6], (L, D), 0.05),
        'norm2': 1.0 + nrm(ks[7], (L, D), 0.05),
        'w_in': nrm(ks[8], (L, D, IN_PROJ), D ** -0.5),
        'mu_prev': uni(ks[9], (L, RWKV_PROJ), 0.0, 0.5),
        'mu_next': uni(ks[10], (L, RWKV_PROJ), 0.0, 0.5),
        'decay_w0': uni(ks[11], (L, 2, RWKV_DIM), -6.0, -1.0),
        'decay_up': nrm(ks[12], (L, 2, DECAY_LORA, RWKV_DIM), 0.1),
        'iclr_a0': nrm(ks[13], (L, 2, RWKV_DIM), 0.1),
        'iclr_up': nrm(ks[14], (L, 2, ICLR_LORA, RWKV_DIM), ICLR_LORA ** -0.5),
        'gate_up': nrm(ks[15], (L, GATE_LORA, RWKV_DIM), GATE_LORA ** -0.5),
        'k_k': 0.85 + nrm(ks[16], (L, RWKV_DIM), 0.05),
        'k_a': 1.0 + nrm(ks[17], (L, RWKV_DIM), 0.05),
        'r_k': nrm(ks[18], (L, RWKV_HEADS, HEAD_DIM), 0.1),
        'gn_w': 1.0 + nrm(ks[19], (L, RWKV_DIM), 0.05),
        'gn_b': nrm(ks[20], (L, RWKV_DIM), 0.01),
        'pool_w': nrm(ks[21], (L, len(POOL_WINDOWS), POOL_GROUP, POOL_GROUP), POOL_GROUP ** -0.5),
        'pool_scale': 1.0 + nrm(ks[22], (L, POOL_DIM), 0.05),
        'q_gain': 1.0 + nrm(ks[23], (L, HEAD_DIM), 0.05),
        'k_gain': 1.0 + nrm(ks[24], (L, HEAD_DIM), 0.05),
        'rpb': nrm(ks[25], (L, NAT_HEADS, 2 * WIN_H - 1, 2 * WIN_W - 1), 0.1),
        'w_rwkv_o': nrm(ks[26], (L, RWKV_DIM, D), RWKV_DIM ** -0.5),
        'w_pool_o': nrm(ks[27], (L, POOL_DIM, D), POOL_DIM ** -0.5),
        'w_nat_o': nrm(ks[28], (L, NAT_DIM, D), NAT_DIM ** -0.5),
        'w_out': nrm(ks[29], (L, D, D), D ** -0.5),
        'w_ffn_in': nrm(ks[30], (L, D, 2 * FFN_HIDDEN), D ** -0.5),
        'w_ffn_out': nrm(ks[31], (L, FFN_HIDDEN, D), FFN_HIDDEN ** -0.5),
    }


def reference(x, c, ctx, c_ctx, w_mod, b_mod, norm1, norm2, w_in, mu_prev, mu_next, decay_w0, decay_up,
              iclr_a0, iclr_up, gate_up, k_k, k_a, r_k, gn_w, gn_b, pool_w, pool_scale, q_gain, k_gain,
              rpb, w_rwkv_o, w_pool_o, w_nat_o, w_out, w_ffn_in, w_ffn_out):
    T = x.shape[1]
    key_idx, bias_idx = _natten_indices(T)
    s_lat = jax.nn.silu(c)
    s_ctx = jax.nn.silu(c_ctx)
    x_lat, x_ctx = x, ctx
    for l in range(DEPTH):
        ctx_out = l < DEPTH - 1
        sh1_l, sc1_l, g1_l, sh2_l, sc2_l, g2_l = jnp.split((s_lat @ w_mod[l] + b_mod[l])[:, None, :], 6, axis=-1)
        sh1_c, sc1_c, g1_c, sh2_c, sc2_c, g2_c = jnp.split(s_ctx @ w_mod[l] + b_mod[l], 6, axis=-1)
        h_lat = _modulate(_rmsnorm(x_lat, norm1[l]), sh1_l, sc1_l)
        h_ctx = _modulate(_rmsnorm(x_ctx, norm1[l]), sh1_c, sc1_c)
        o_lat, o_ctx = _token_mixer(h_lat, h_ctx, ctx_out, key_idx, bias_idx, w_in[l], mu_prev[l], mu_next[l],
                                    decay_w0[l], decay_up[l], iclr_a0[l], iclr_up[l], gate_up[l], k_k[l], k_a[l],
                                    r_k[l], gn_w[l], gn_b[l], pool_w[l], pool_scale[l], q_gain[l], k_gain[l],
                                    rpb[l].reshape(NAT_HEADS, -1), w_rwkv_o[l], w_pool_o[l], w_nat_o[l], w_out[l])
        x_lat = x_lat + g1_l * o_lat
        x_lat = x_lat + g2_l * _swiglu(_modulate(_rmsnorm(x_lat, norm2[l]), sh2_l, sc2_l), w_ffn_in[l], w_ffn_out[l])
        if ctx_out:
            x_ctx = x_ctx + g1_c * o_ctx
            x_ctx = x_ctx + g2_c * _swiglu(_modulate(_rmsnorm(x_ctx, norm2[l]), sh2_c, sc2_c), w_ffn_in[l], w_ffn_out[l])
    return x_lat
```

```python
import functools

import numpy as np
import jax
import jax.numpy as jnp
from jax import lax
from jax.experimental import pallas as pl
from jax.experimental.pallas import tpu as pltpu

F32 = jnp.float32
BF16 = jnp.bfloat16

HEAD = 64
NORM_EPS = 1e-6
GN_EPS = 64e-5
KK_EPS = 1e-24
POOL_WINDOWS = (2, 4, 8, 16)
POOL_GROUP = 64
GRID_W = 64
WIN_H = 8
WIN_W = 16
MASK_BIAS = -1e30

LANES = 128
SUBLANES = 8
VMEM_LIMIT = 56 * 1024 * 1024

TM = 256
TC = 128
TN_IN = 2048
TN_MOD = 1536


def _dot_hi(a, b):
    return jnp.dot(a, b, precision=lax.Precision.HIGHEST, preferred_element_type=F32)


def _dot_bf(a, b):
    return jnp.dot(a.astype(BF16), b.astype(BF16), preferred_element_type=F32)


def _dot_nt(a, b):
    return lax.dot_general(a, b, (((1,), (1,)), ((), ())), preferred_element_type=F32)


def _params(*sem):
    return pltpu.CompilerParams(dimension_semantics=sem, vmem_limit_bytes=VMEM_LIMIT)


def _const_spec(shape):
    nd = len(shape)
    return pl.BlockSpec(shape, lambda *_: (0,) * nd, pipeline_mode=pl.Buffered(1))


def _mod_kernel(s_ref, w_ref, b_ref, o_ref):
    s = s_ref[...]
    s = s * jax.nn.sigmoid(s)
    o_ref[0] = _dot_hi(s, w_ref[0]) + b_ref[0]


def _modulation(s_rows, w_mod, b_mod):
    depth, d, n = w_mod.shape
    return pl.pallas_call(
        _mod_kernel,
        grid=(depth, n // TN_MOD),
        in_specs=[
            pl.BlockSpec((SUBLANES, d), lambda l, j: (0, 0)),
            pl.BlockSpec((1, d, TN_MOD), lambda l, j: (l, 0, j)),
            pl.BlockSpec((1, 1, TN_MOD), lambda l, j: (l, 0, j)),
        ],
        out_specs=pl.BlockSpec((1, SUBLANES, TN_MOD), lambda l, j: (l, 0, j)),
        out_shape=jax.ShapeDtypeStruct((depth, SUBLANES, n), F32),
        compiler_params=_params("parallel", "parallel"),
        name="modulation",
    )(s_rows, w_mod, b_mod.reshape(depth, 1, n))


def _mod_spec(nct, k, lead=0):
    def index(*ids):
        b, i = ids[lead], ids[lead + 1]
        return ((b * 2 + (i >= nct).astype(jnp.int32)) * 6 + k, 0, 0)

    return index


def _norm_mod(x, gain, shift, scale):
    ms = jnp.mean(x * x, axis=-1, keepdims=True)
    return x * lax.rsqrt(ms + NORM_EPS) * gain * (1.0 + scale) + shift


def _in_proj_kernel(x_ref, sh_ref, sc_ref, g_ref, w_ref, o_ref):
    h = _norm_mod(x_ref[0], g_ref[...], sh_ref[0], sc_ref[0])
    o_ref[0] = jnp.dot(h.astype(BF16), w_ref[...], preferred_element_type=F32)


def _in_proj(x, mod, gain, w, nct):
    bsz, n, d = x.shape
    nout = w.shape[1]
    mspec = lambda k: pl.BlockSpec((1, 1, d), _mod_spec(nct, k, lead=1))
    return pl.pallas_call(
        _in_proj_kernel,
        grid=(nout // TN_IN, bsz, n // TM),
        in_specs=[
            pl.BlockSpec((1, TM, d), lambda j, b, i: (b, i, 0)),
            mspec(0), mspec(1),
            pl.BlockSpec((1, d), lambda j, b, i: (0, 0)),
            pl.BlockSpec((d, TN_IN), lambda j, b, i: (0, j)),
        ],
        out_specs=pl.BlockSpec((1, TM, TN_IN), lambda j, b, i: (b, i, j)),
        out_shape=jax.ShapeDtypeStruct((bsz, n, nout), F32),
        compiler_params=_params("parallel", "parallel", "parallel"),
        name="in_proj",
    )(x, mod, mod, gain, w)


def _halo_specs(width, col, n):
    nb = n // SUBLANES
    per = TM // SUBLANES
    return [
        pl.BlockSpec((1, TM, width), lambda b, i: (b, i, col)),
        pl.BlockSpec((1, SUBLANES, width), lambda b, i: (b, jnp.maximum(i * per - 1, 0), col)),
        pl.BlockSpec((1, SUBLANES, width), lambda b, i: (b, jnp.minimum((i + 1) * per, nb - 1), col)),
    ]


def _prep_kernel(nct, nt, rkv_ref, rkv_p, rkv_n, lo_ref, lo_p, lo_n, gd_ref, gd_p, gd_n,
                 mup_ref, mun_ref, du_ref, w0_ref, au_ref, a0_ref, kkw_ref, ka_ref, ones_ref,
                 wf_ref, wb_ref, kdf_ref, kdb_ref, bf_ref, bb_ref, v_ref, kk_ref, r_ref, gdo_ref):
    i = pl.program_id(1)
    first = jnp.logical_or(i == 0, i == nct)
    last = jnp.logical_or(i == nct - 1, i == nt - 1)
    row = lax.broadcasted_iota(jnp.int32, (TM, 1), 0)
    c = rkv_ref.shape[2] // 3

    def mix(main, prev8, next8, lo, hi):
        pm = main[0]
        prow = jnp.where(first, 0.0, prev8[0, SUBLANES - 1:SUBLANES, :])
        nrow = jnp.where(last, 0.0, next8[0, 0:1, :])
        prev = jnp.where(row == 0, prow, pltpu.roll(pm, 1, 0))
        nxt = jnp.where(row == TM - 1, nrow, pltpu.roll(pm, TM - 1, 0))
        return pm + mup_ref[:, lo:hi] * (prev - pm) + mun_ref[:, lo:hi] * (nxt - pm)

    rkv = mix(rkv_ref, rkv_p, rkv_n, 0, 3 * c)
    lora = mix(lo_ref, lo_p, lo_n, 3 * c, 3 * c + 2 * LANES)
    gdo_ref[0] = mix(gd_ref, gd_p, gd_n, 3 * c + 2 * LANES, 3 * c + 4 * LANES)

    r, k, v = rkv[:, 0:c], rkv[:, c:2 * c], rkv[:, 2 * c:3 * c]
    wd, ad = lora[:, 0:LANES], lora[:, LANES:2 * LANES]
    dec = _dot_hi(jnp.tanh(wd), du_ref[...]) + w0_ref[...]
    softplus_neg = jnp.maximum(-dec, 0.0) + jnp.log1p(jnp.exp(-jnp.abs(dec)))
    decay = jnp.exp(-jnp.exp(-softplus_neg - 0.5))
    a = jax.nn.sigmoid(_dot_hi(ad, au_ref[...]) + a0_ref[...])
    kk = k * kkw_ref[...]
    kk = kk * lax.rsqrt(jnp.maximum(_dot_hi(kk * kk, ones_ref[...]), KK_EPS))
    ka = ka_ref[...]
    for z, (w_o, kd_o, b_o) in enumerate(((wf_ref, kdf_ref, bf_ref), (wb_ref, kdb_ref, bb_ref))):
        az = a[:, z * c:(z + 1) * c]
        w_o[0] = decay[:, z * c:(z + 1) * c]
        kd_o[0] = k * (1.0 + (az - 1.0) * ka)
        b_o[0] = kk * az
    v_ref[0] = v
    kk_ref[0] = kk
    r_ref[0] = r


def _rwkv_prep(p, lw, nct):
    bsz, n, _ = p.shape
    c = lw["k_k"].shape[1]
    nt = n // TM
    tok = jax.ShapeDtypeStruct((bsz, n, c), F32)
    vec = lambda a: _const_spec(a.shape)
    consts = [lw["mu_prev"], lw["mu_next"], lw["decay_up"], lw["decay_w0"], lw["iclr_up"], lw["iclr_a0"],
              lw["k_k"], lw["k_a"], lw["ones_bd"]]
    return pl.pallas_call(
        functools.partial(_prep_kernel, nct, nt),
        grid=(bsz, nt),
        in_specs=_halo_specs(3 * c, 0, n) + _halo_specs(2 * LANES, (6 * c) // (2 * LANES), n)
        + _halo_specs(2 * LANES, (6 * c) // (2 * LANES) + 1, n) + [vec(a) for a in consts],
        out_specs=[pl.BlockSpec((1, TM, c), lambda b, i: (b, i, 0))] * 9
        + [pl.BlockSpec((1, TM, 2 * LANES), lambda b, i: (b, i, 0))],
        out_shape=[tok] * 9 + [jax.ShapeDtypeStruct((bsz, n, 2 * LANES), F32)],
        compiler_params=_params("parallel", "parallel"),
        name="rwkv_prep",
    )(p, p, p, p, p, p, p, p, p, *consts)


def _scan_kernel(wf, kdf, bf, vf, kkf, rf, wb, kdb, bb, vb, kkb, rb, yf_ref, yb_ref, s_ref):
    @pl.when(pl.program_id(0) == 0)
    def _():
        s_ref[...] = jnp.zeros_like(s_ref)

    bsz, tc, c = wf.shape
    sub = lax.broadcasted_iota(jnp.int32, (HEAD, LANES), 0)
    lane = lax.broadcasted_iota(jnp.int32, (HEAD, LANES), 1)
    diag = (jnp.bitwise_and(lane, HEAD - 1) == sub).astype(F32)
    hrow = lax.broadcasted_iota(jnp.int32, (LANES, LANES), 0) >= HEAD
    hcol = lax.broadcasted_iota(jnp.int32, (LANES, LANES), 1) >= HEAD
    ones_bd = (hrow == hcol).astype(F32).astype(BF16)

    def head_sum(p):
        hi = p.astype(BF16)
        lo = (p - hi.astype(F32)).astype(BF16)
        return (jnp.dot(hi, ones_bd, preferred_element_type=F32)
                + jnp.dot(lo, ones_bd, preferred_element_type=F32))

    dirs = ((wf, kdf, bf, vf, kkf, rf, yf_ref), (wb, kdb, bb, vb, kkb, rb, yb_ref))

    groups = tc // SUBLANES
    sub8 = lax.broadcasted_iota(jnp.int32, (SUBLANES, LANES), 0)

    def body(gi, carry):
        for d, (w_r, kd_r, b_r, v_r, kk_r, r_r, y_r) in enumerate(dirs):
            base = pl.multiple_of((gi if d == 0 else groups - 1 - gi) * SUBLANES, SUBLANES)
            for b in range(bsz):
                for p in range(c // LANES):
                    ls = slice(p * LANES, (p + 1) * LANES)
                    w8, kd8, b8, v8, kk8, r8 = (ref[b, pl.ds(base, SUBLANES), ls]
                                                 for ref in (w_r, kd_r, b_r, v_r, kk_r, r_r))
                    y8 = jnp.zeros((SUBLANES, LANES), F32)
                    st = s_ref[b, d, p]
                    for jj in range(SUBLANES):
                        j = jj if d == 0 else SUBLANES - 1 - jj
                        sa = head_sum(st * kk8[j:j + 1])
                        vcol = head_sum(v8[j:j + 1] * diag)
                        st = st * w8[j:j + 1] + vcol * kd8[j:j + 1] - sa * b8[j:j + 1]
                        yrep = head_sum(st * r8[j:j + 1])
                        y8 = jnp.where(sub8 == j, jnp.sum(yrep * diag, axis=0, keepdims=True), y8)
                    s_ref[b, d, p] = st
                    y_r[b, pl.ds(base, SUBLANES), ls] = y8
        return carry

    lax.fori_loop(0, groups, body, 0)


def _rwkv_scan(w_f, w_b, kd_f, kd_b, b_f, b_b, v, kk, r, nct_c):
    bsz, n, c = v.shape
    nc = n // TC

    def rev(g):
        return jnp.where(g < nct_c, nct_c - 1 - g, nc - 1 - g + nct_c)

    fwd = pl.BlockSpec((bsz, TC, c), lambda g: (0, g, 0))
    bwd = pl.BlockSpec((bsz, TC, c), lambda g: (0, rev(g), 0))
    tok = jax.ShapeDtypeStruct((bsz, n, c), F32)
    return pl.pallas_call(
        _scan_kernel,
        grid=(nc,),
        in_specs=[fwd] * 6 + [bwd] * 6,
        out_specs=[fwd, bwd],
        out_shape=[tok, tok],
        scratch_shapes=[pltpu.VMEM((bsz, 2, c // LANES, HEAD, LANES), F32)],
        compiler_params=_params("arbitrary"),
        name="rwkv_scan",
    )(w_f, kd_f, b_f, v, kk, r, w_b, kd_b, b_b, v, kk, r)


def _readout_kernel(yf, yb, r, kdf, kdb, v, gd, gnw, gnb, rk, gup, ones_ref, a_ref):
    ones = ones_ref[...]
    inv = 1.0 / HEAD
    y = yf[0] + yb[0]
    yc = y - _dot_hi(y, ones) * inv
    var = _dot_hi(yc * yc, ones) * inv
    yn = yc * lax.rsqrt(var + GN_EPS) * gnw[...] + gnb[...]
    bonus = _dot_hi(r[0] * rk[...] * (kdf[0] + kdb[0]), ones) * v[0]
    g = _dot_hi(jax.nn.sigmoid(gd[0]), gup[...])
    a_ref[0] = (yn + bonus) * g


def _rwkv_readout(y_f, y_b, r, kd_f, kd_b, v, gd, lw):
    bsz, n, c = v.shape
    tok = pl.BlockSpec((1, TM, c), lambda b, i: (b, i, 0))
    consts = [lw["gn_w"], lw["gn_b"], lw["r_k"], lw["gate_up"], lw["ones_bd"]]
    return pl.pallas_call(
        _readout_kernel,
        grid=(bsz, n // TM),
        in_specs=[tok] * 6 + [pl.BlockSpec((1, TM, 2 * LANES), lambda b, i: (b, i, 0))]
        + [_const_spec(a.shape) for a in consts],
        out_specs=tok,
        out_shape=jax.ShapeDtypeStruct((bsz, n, c), F32),
        compiler_params=_params("parallel", "parallel"),
        name="rwkv_readout",
    )(y_f, y_b, r, kd_f, kd_b, v, gd, *consts)


def _pool_kernel(nct, len_ctx, len_lat, main, prev8, next8, pw_ref, scale_ref, o_ref):
    i = pl.program_id(1)
    is_lat = i >= nct
    seq_len = jnp.where(is_lat, len_lat, len_ctx)
    t0 = jnp.where(is_lat, i - nct, i) * TM
    n = TM + 2 * SUBLANES
    pm = main[0]
    ext = jnp.concatenate([prev8[0], pm, next8[0]], axis=0)
    pos = t0 - SUBLANES + lax.broadcasted_iota(jnp.int32, (n, 1), 0)
    e = jnp.where(jnp.logical_and(pos >= 0, pos < seq_len), ext, 0.0)
    a2 = e + pltpu.roll(e, 1, 0)
    a4 = pltpu.roll(a2, 1, 0) + pltpu.roll(a2, n - 1, 0)
    a8 = pltpu.roll(a4, 2, 0) + pltpu.roll(a4, n - 2, 0)
    a16 = pltpu.roll(a8, 4, 0) + pltpu.roll(a8, n - 4, 0)
    t = t0 + lax.broadcasted_iota(jnp.int32, (TM, 1), 0)
    lane = lax.broadcasted_iota(jnp.int32, pm.shape, 1)
    mean = None
    for g, (w, acc) in reversed(list(enumerate(zip(POOL_WINDOWS, (a2, a4, a8, a16))))):
        lo = jnp.maximum(t - w // 2, 0)
        hi = jnp.minimum(t + (w - w // 2) - 1, seq_len - 1)
        m = acc[SUBLANES:SUBLANES + TM] / (hi - lo + 1).astype(F32)
        mean = m if mean is None else jnp.where(lane < (g + 1) * POOL_GROUP, m, mean)
    o_ref[0] = _dot_hi(mean - pm, pw_ref[...]) * scale_ref[...]


def _pool(p, lw, nct, len_ctx, len_lat, col):
    bsz, n, _ = p.shape
    width = lw["pool_w"].shape[0]
    return pl.pallas_call(
        functools.partial(_pool_kernel, nct, len_ctx, len_lat),
        grid=(bsz, n // TM),
        in_specs=_halo_specs(width, col, n) + [_const_spec(lw["pool_w"].shape), _const_spec(lw["pool_scale"].shape)],
        out_specs=pl.BlockSpec((1, TM, width), lambda b, i: (b, i, 0)),
        out_shape=jax.ShapeDtypeStruct((bsz, n, width), F32),
        compiler_params=_params("parallel", "parallel"),
        name="pool",
    )(p, p, p, lw["pool_w"], lw["pool_scale"])


def _qk_norm_kernel(q_ref, k_ref, v_ref, qg_ref, kg_ref, ones_ref, qo, ko, vo):
    ones = ones_ref[...]
    inv = 1.0 / HEAD
    q, k = q_ref[0], k_ref[0]
    qn = q * lax.rsqrt(_dot_hi(q * q, ones) * inv + NORM_EPS) * qg_ref[...]
    kn = k * lax.rsqrt(_dot_hi(k * k, ones) * inv + NORM_EPS) * kg_ref[...]
    qo[0] = (qn * HEAD ** -0.5).astype(BF16)
    ko[0] = kn.astype(BF16)
    vo[0] = v_ref[0].astype(BF16)


def _qk_norm(p, lw, col):
    bsz, n, _ = p.shape
    c = lw["q_gain"].shape[1]
    tok = jax.ShapeDtypeStruct((bsz, n, c), BF16)
    ospec = pl.BlockSpec((1, TM, c), lambda b, i: (b, i, 0))
    consts = [lw["q_gain"], lw["k_gain"], lw["ones_bd"]]
    return pl.pallas_call(
        _qk_norm_kernel,
        grid=(bsz, n // TM),
        in_specs=[pl.BlockSpec((1, TM, c), lambda b, i, j=j: (b, i, col + j)) for j in range(3)]
        + [_const_spec(a.shape) for a in consts],
        out_specs=[ospec] * 3,
        out_shape=[tok] * 3,
        compiler_params=_params("parallel", "parallel"),
        name="qk_norm",
    )(p, p, p, *consts)


def _softmax_pv(parts):
    m = functools.reduce(jnp.maximum, [jnp.max(s, axis=-1, keepdims=True) for s, _ in parts])
    ps = [jnp.exp(s - m) for s, _ in parts]
    den = functools.reduce(jnp.add, [jnp.sum(p, axis=-1, keepdims=True) for p in ps])
    num = functools.reduce(jnp.add, [jnp.dot(p.astype(BF16), v, preferred_element_type=F32)
                                      for p, (_, v) in zip(ps, parts)])
    return num / den


def _nat_kernel(rows, q_ref, kp, kc, kn, vp, vc, vn, kx_ref, vx_ref, bias_ref, o_ref, ks, vs):
    i = pl.program_id(1)
    tq = q_ref.shape[1]
    rb = tq // GRID_W
    nloc = WIN_H * GRID_W
    for j, (kr, vr) in enumerate(((kp, vp), (kc, vc), (kn, vn))):
        ks[j * tq:(j + 1) * tq, :] = kr[0]
        vs[j * tq:(j + 1) * tq, :] = vr[0]
    for rr in range(rb):
        r = i * rb + rr
        rs = jnp.clip(r - WIN_H // 2, 0, rows - WIN_H)
        off = r - rs
        start = pl.multiple_of((rs - i * rb + rb) * GRID_W, GRID_W)
        qs = slice(rr * GRID_W, (rr + 1) * GRID_W)
        for h in range(q_ref.shape[2] // HEAD):
            hl = slice(h * HEAD, (h + 1) * HEAD)
            q = q_ref[0, qs, hl]
            s_loc = _dot_nt(q, ks[pl.ds(start, nloc), hl]) + bias_ref[h, off]
            s_ctx = _dot_nt(q, kx_ref[0, :, hl])
            o_ref[0, qs, hl] = _softmax_pv([(s_loc, vs[pl.ds(start, nloc), hl]), (s_ctx, vx_ref[0, :, hl])])


def _nat_latent(qn, kn, vn, bias, len_ctx, len_lat):
    bsz, n, c = qn.shape
    tq = len_ctx
    assert tq == (WIN_H // 2) * GRID_W and len_lat % tq == 0
    rows = len_lat // GRID_W
    nblk = len_lat // tq

    def blk(shift):
        return pl.BlockSpec((1, tq, c), lambda b, i: (b, 1 + jnp.clip(i + shift, 0, nblk - 1), 0))

    ctx = pl.BlockSpec((1, tq, c), lambda b, i: (b, 0, 0))
    return pl.pallas_call(
        functools.partial(_nat_kernel, rows),
        grid=(bsz, nblk),
        in_specs=[blk(0), blk(-1), blk(0), blk(1), blk(-1), blk(0), blk(1), ctx, ctx, _const_spec(bias.shape)],
        out_specs=pl.BlockSpec((1, tq, c), lambda b, i: (b, i, 0)),
        out_shape=jax.ShapeDtypeStruct((bsz, len_lat, c), F32),
        scratch_shapes=[pltpu.VMEM((3 * tq, c), BF16), pltpu.VMEM((3 * tq, c), BF16)],
        compiler_params=_params("parallel", "parallel"),
        name="nat_latent",
    )(qn, kn, kn, kn, vn, vn, vn, kn, vn, bias)


def _ctx_attn_kernel(q_ref, k_ref, v_ref, o_ref):
    for h in range(q_ref.shape[2] // HEAD):
        hl = slice(h * HEAD, (h + 1) * HEAD)
        o_ref[0, :, hl] = _softmax_pv([(_dot_nt(q_ref[0, :, hl], k_ref[0, :, hl]), v_ref[0, :, hl])])


def _ctx_attn(qn, kn, vn, len_ctx):
    bsz, _, c = qn.shape
    spec = pl.BlockSpec((1, len_ctx, c), lambda b: (b, 0, 0))
    return pl.pallas_call(
        _ctx_attn_kernel,
        grid=(bsz,),
        in_specs=[spec] * 3,
        out_specs=spec,
        out_shape=jax.ShapeDtypeStruct((bsz, len_ctx, c), F32),
        compiler_params=_params("parallel"),
        name="ctx_attn",
    )(qn, kn, vn)


def _nat_bias_table(rpb):
    off = np.arange(WIN_H)[:, None]
    j = np.arange(WIN_H)[None, :]
    dr = j - off + WIN_H - 1
    qc = np.arange(GRID_W)[:, None]
    kc = np.arange(GRID_W)[None, :]
    cs = np.clip(qc - WIN_W // 2, 0, GRID_W - WIN_W)
    valid = (kc >= cs) & (kc < cs + WIN_W)
    dc = np.clip(kc - qc + WIN_W - 1, 0, 2 * WIN_W - 2)
    t = rpb[:, dr[:, :, None, None], dc[None, None, :, :]]
    t = jnp.where(valid[None, None, None], t, MASK_BIAS)
    return t.transpose(0, 1, 3, 2, 4).reshape(rpb.shape[0], WIN_H, GRID_W, WIN_H * GRID_W)


def _merge_kernel(a, bp, cn, ga, gb, gc, x, g1, wa, wb, wc, wo, o_ref):
    m = (jax.nn.sigmoid(ga[0]) * _dot_bf(a[0], wa[...])
         + jax.nn.sigmoid(gb[0]) * _dot_bf(bp[0], wb[...])
         + jax.nn.sigmoid(gc[0]) * _dot_bf(cn[0], wc[...]))
    o_ref[0] = x[0] + g1[0] * _dot_bf(m, wo[...])


def _merge(a, bp, cn, p, x, mod, lw, nct, gate_col):
    bsz, n, d = x.shape
    tok = lambda w, col=0: pl.BlockSpec((1, TM, w), lambda b, i: (b, i, col))
    ws = [lw["w_rwkv_o"], lw["w_pool_o"], lw["w_nat_o"], lw["w_out"]]
    return pl.pallas_call(
        _merge_kernel,
        grid=(bsz, n // TM),
        in_specs=[tok(a.shape[2]), tok(bp.shape[2]), tok(cn.shape[2]),
                  tok(d, gate_col), tok(d, gate_col + 1), tok(d, gate_col + 2), tok(d),
                  pl.BlockSpec((1, 1, d), _mod_spec(nct, 2))] + [_const_spec(w.shape) for w in ws],
        out_specs=tok(d),
        out_shape=jax.ShapeDtypeStruct((bsz, n, d), F32),
        compiler_params=_params("parallel", "parallel"),
        name="merge",
    )(a, bp, cn, p, p, p, x, mod, *ws)


def _ffn_kernel(x_ref, sh, sc, g2, gain, w1, w2, o_ref):
    x = x_ref[0]
    h = _norm_mod(x, gain[...], sh[0], sc[0])
    u = jnp.dot(h.astype(BF16), w1[...], preferred_element_type=F32)
    hid = w2.shape[0]
    gate, up = u[:, :hid], u[:, hid:]
    act = gate * jax.nn.sigmoid(gate) * up
    o_ref[0] = x + g2[0] * jnp.dot(act.astype(BF16), w2[...], preferred_element_type=F32)


def _ffn(x, mod, lw, nct):
    bsz, n, d = x.shape
    tok = pl.BlockSpec((1, TM, d), lambda b, i: (b, i, 0))
    mspec = lambda k: pl.BlockSpec((1, 1, d), _mod_spec(nct, k))
    return pl.pallas_call(
        _ffn_kernel,
        grid=(bsz, n // TM),
        in_specs=[tok, mspec(3), mspec(4), mspec(5), _const_spec(lw["norm2"].shape),
                  _const_spec(lw["w_ffn_in"].shape), _const_spec(lw["w_ffn_out"].shape)],
        out_specs=tok,
        out_shape=jax.ShapeDtypeStruct((bsz, n, d), F32),
        compiler_params=_params("parallel", "parallel"),
        name="ffn",
    )(x, mod, mod, mod, lw["norm2"], lw["w_ffn_in"], lw["w_ffn_out"])


def _block_diag(blocks):
    n = len(blocks)
    rows = []
    for i, blk in enumerate(blocks):
        rows.append(jnp.concatenate([blk if j == i else jnp.zeros((blk.shape[0], blocks[j].shape[1]), blk.dtype)
                                     for j in range(n)], axis=1))
    return jnp.concatenate(rows, axis=0)


def _pad_cols(a, width):
    return jnp.pad(a, ((0, 0), (0, width - a.shape[1])))


def _layer_weights(l, prm):
    c = prm["k_k"].shape[1]
    lora = prm["decay_up"].shape[2]
    gl = prm["gate_up"].shape[1]
    pool = prm["pool_scale"].shape[1]
    d = prm["w_out"].shape[1]
    w_in = prm["w_in"][l]
    o_lora, o_gd, o_pool = 3 * c, 3 * c + 4 * lora, 3 * c + 4 * lora + gl
    o_q = o_pool + pool
    o_gate = o_q + 3 * c
    assert 4 * lora == 2 * LANES and gl <= 2 * LANES and pool == 2 * LANES and o_gate + 3 * d == w_in.shape[1]
    w_perm = jnp.concatenate([
        w_in[:, 0:o_lora], w_in[:, o_q:o_gate], w_in[:, o_lora:o_gd],
        _pad_cols(w_in[:, o_gd:o_pool], 2 * LANES), w_in[:, o_pool:o_q], w_in[:, o_gate:]], axis=1).astype(BF16)
    mu = lambda m: _pad_cols(m[l][None, :o_pool], o_pool + 2 * LANES - gl)
    heads = c // HEAD
    row = lambda a: a.reshape(1, -1)
    return {
        "w_in": w_perm,
        "norm1": row(prm["norm1"][l]), "norm2": row(prm["norm2"][l]),
        "mu_prev": mu(prm["mu_prev"]), "mu_next": mu(prm["mu_next"]),
        "decay_up": _block_diag([prm["decay_up"][l, 0], prm["decay_up"][l, 1]]),
        "decay_w0": row(prm["decay_w0"][l]),
        "iclr_up": _block_diag([prm["iclr_up"][l, 0], prm["iclr_up"][l, 1]]),
        "iclr_a0": row(prm["iclr_a0"][l]),
        "k_k": row(prm["k_k"][l]), "k_a": row(prm["k_a"][l]), "r_k": row(prm["r_k"][l]),
        "gn_w": row(prm["gn_w"][l]), "gn_b": row(prm["gn_b"][l]),
        "gate_up": jnp.pad(prm["gate_up"][l], ((0, 2 * LANES - gl), (0, 0))),
        "ones_bd": jnp.kron(jnp.eye(heads, dtype=F32), jnp.ones((HEAD, HEAD), F32)),
        "pool_w": _block_diag([prm["pool_w"][l, g] for g in range(len(POOL_WINDOWS))]),
        "pool_scale": row(prm["pool_scale"][l]),
        "q_gain": row(jnp.tile(prm["q_gain"][l], heads)), "k_gain": row(jnp.tile(prm["k_gain"][l], heads)),
        "nat_bias": _nat_bias_table(prm["rpb"][l]),
        "w_rwkv_o": prm["w_rwkv_o"][l].astype(BF16), "w_pool_o": prm["w_pool_o"][l].astype(BF16),
        "w_nat_o": prm["w_nat_o"][l].astype(BF16), "w_out": prm["w_out"][l].astype(BF16),
        "w_ffn_in": prm["w_ffn_in"][l].astype(BF16), "w_ffn_out": prm["w_ffn_out"][l].astype(BF16),
    }


def kernel(x, c, ctx, c_ctx, w_mod, b_mod, norm1, norm2, w_in, mu_prev, mu_next, decay_w0, decay_up, iclr_a0, iclr_up, gate_up, k_k, k_a, r_k, gn_w, gn_b, pool_w, pool_scale, q_gain, k_gain, rpb, w_rwkv_o, w_pool_o, w_nat_o, w_out, w_ffn_in, w_ffn_out):
    prm = dict(norm1=norm1, norm2=norm2, w_in=w_in, mu_prev=mu_prev, mu_next=mu_next, decay_w0=decay_w0,
               decay_up=decay_up, iclr_a0=iclr_a0, iclr_up=iclr_up, gate_up=gate_up, k_k=k_k, k_a=k_a, r_k=r_k,
               gn_w=gn_w, gn_b=gn_b, pool_w=pool_w, pool_scale=pool_scale, q_gain=q_gain, k_gain=k_gain, rpb=rpb,
               w_rwkv_o=w_rwkv_o, w_pool_o=w_pool_o, w_nat_o=w_nat_o, w_out=w_out, w_ffn_in=w_ffn_in,
               w_ffn_out=w_ffn_out)
    bsz, len_lat, d = x.shape
    len_ctx = ctx.shape[1]
    depth = w_mod.shape[0]
    assert len_ctx % TM == 0 and len_lat % TM == 0 and bsz + 1 <= SUBLANES
    nct = len_ctx // TM
    cdim = k_k.shape[1]

    s_rows = jnp.concatenate([c, c_ctx[None, :], jnp.zeros((SUBLANES - bsz - 1, d), F32)], axis=0)
    mod_all = _modulation(s_rows, w_mod, b_mod)
    xa = jnp.concatenate([ctx, x], axis=1)

    for l in range(depth):
        lw = _layer_weights(l, prm)
        m_lat = mod_all[l, :bsz]
        m_ctx = jnp.broadcast_to(mod_all[l, bsz][None], m_lat.shape)
        mod = jnp.stack([m_ctx, m_lat], axis=1).reshape(bsz * 2 * 6, 1, d)

        p = _in_proj(xa, mod, lw["norm1"], lw["w_in"], nct)
        w_f, w_b, kd_f, kd_b, b_f, b_b, v, kk, r, gd = _rwkv_prep(p, lw, nct)
        y_f, y_b = _rwkv_scan(w_f, w_b, kd_f, kd_b, b_f, b_b, v, kk, r, len_ctx // TC)
        a_br = _rwkv_readout(y_f, y_b, r, kd_f, kd_b, v, gd, lw)
        b_br = _pool(p, lw, nct, len_ctx, len_lat, (6 * cdim + 4 * LANES) // (2 * LANES))
        qn, kn, vn = _qk_norm(p, lw, 3)
        c_br = jnp.concatenate([_ctx_attn(qn, kn, vn, len_ctx),
                                _nat_latent(qn, kn, vn, lw["nat_bias"], len_ctx, len_lat)], axis=1)
        xa = _merge(a_br, b_br, c_br, p, xa, mod, lw, nct, (6 * cdim + 6 * LANES) // d)
        xa = _ffn(xa, mod, lw, nct)
    return xa[:, len_ctx:]
```

```python
import functools

import numpy as np
import jax
import jax.numpy as jnp
from jax import lax
from jax.experimental import pallas as pl
from jax.experimental.pallas import tpu as pltpu

F32 = jnp.float32
BF16 = jnp.bfloat16

HEAD = 64
NORM_EPS = 1e-6
GN_EPS = 64e-5
KK_EPS = 1e-24
POOL_WINDOWS = (2, 4, 8, 16)
POOL_GROUP = 64
GRID_W = 64
WIN_H = 8
WIN_W = 16
MASK_BIAS = -1e30

LANES = 128
SUBLANES = 8
VMEM_LIMIT = 56 * 1024 * 1024

TM = 256
TC = 128
TN_IN = 2048
TN_MOD = 1536


def _dot_hi(a, b):
    return jnp.dot(a, b, precision=lax.Precision.HIGHEST, preferred_element_type=F32)


def _dot_bf(a, b):
    return jnp.dot(a.astype(BF16), b.astype(BF16), preferred_element_type=F32)


def _dot_nt(a, b):
    return lax.dot_general(a, b, (((1,), (1,)), ((), ())), preferred_element_type=F32)


def _params(*sem):
    return pltpu.CompilerParams(dimension_semantics=sem, vmem_limit_bytes=VMEM_LIMIT)


def _const_spec(shape):
    nd = len(shape)
    return pl.BlockSpec(shape, lambda *_: (0,) * nd, pipeline_mode=pl.Buffered(1))


def _mod_kernel(s_ref, w_ref, b_ref, o_ref):
    s = s_ref[...]
    s = s * jax.nn.sigmoid(s)
    o_ref[0] = _dot_hi(s, w_ref[0]) + b_ref[0]


def _modulation(s_rows, w_mod, b_mod):
    depth, d, n = w_mod.shape
    return pl.pallas_call(
        _mod_kernel,
        grid=(depth, n // TN_MOD),
        in_specs=[
            pl.BlockSpec((SUBLANES, d), lambda l, j: (0, 0)),
            pl.BlockSpec((1, d, TN_MOD), lambda l, j: (l, 0, j)),
            pl.BlockSpec((1, 1, TN_MOD), lambda l, j: (l, 0, j)),
        ],
        out_specs=pl.BlockSpec((1, SUBLANES, TN_MOD), lambda l, j: (l, 0, j)),
        out_shape=jax.ShapeDtypeStruct((depth, SUBLANES, n), F32),
        compiler_params=_params("parallel", "parallel"),
        name="modulation",
    )(s_rows, w_mod, b_mod.reshape(depth, 1, n))


def _mod_spec(nct, k, lead=0):
    def index(*ids):
        b, i = ids[lead], ids[lead + 1]
        return ((b * 2 + (i >= nct).astype(jnp.int32)) * 6 + k, 0, 0)

    return index


def _norm_mod(x, gain, shift, scale):
    ms = jnp.mean(x * x, axis=-1, keepdims=True)
    return x * lax.rsqrt(ms + NORM_EPS) * gain * (1.0 + scale) + shift


def _in_proj_kernel(x_ref, sh_ref, sc_ref, g_ref, w_ref, o_ref):
    h = _norm_mod(x_ref[0], g_ref[...], sh_ref[0], sc_ref[0])
    o_ref[0] = jnp.dot(h.astype(BF16), w_ref[...], preferred_element_type=F32)


def _in_proj(x, mod, gain, w, nct):
    bsz, n, d = x.shape
    nout = w.shape[1]
    mspec = lambda k: pl.BlockSpec((1, 1, d), _mod_spec(nct, k, lead=1))
    return pl.pallas_call(
        _in_proj_kernel,
        grid=(nout // TN_IN, bsz, n // TM),
        in_specs=[
            pl.BlockSpec((1, TM, d), lambda j, b, i: (b, i, 0)),
            mspec(0), mspec(1),
            pl.BlockSpec((1, d), lambda j, b, i: (0, 0)),
            pl.BlockSpec((d, TN_IN), lambda j, b, i: (0, j)),
        ],
        out_specs=pl.BlockSpec((1, TM, TN_IN), lambda j, b, i: (b, i, j)),
        out_shape=jax.ShapeDtypeStruct((bsz, n, nout), F32),
        compiler_params=_params("parallel", "parallel", "parallel"),
        name="in_proj",
    )(x, mod, mod, gain, w)


def _halo_specs(width, col, n):
    nb = n // SUBLANES
    per = TM // SUBLANES
    return [
        pl.BlockSpec((1, TM, width), lambda b, i: (b, i, col)),
        pl.BlockSpec((1, SUBLANES, width), lambda b, i: (b, jnp.maximum(i * per - 1, 0), col)),
        pl.BlockSpec((1, SUBLANES, width), lambda b, i: (b, jnp.minimum((i + 1) * per, nb - 1), col)),
    ]


def _prep_kernel(nct, nt, rkv_ref, rkv_p, rkv_n, lo_ref, lo_p, lo_n, gd_ref, gd_p, gd_n,
                 mup_ref, mun_ref, du_ref, w0_ref, au_ref, a0_ref, kkw_ref, ka_ref, ones_ref,
                 wf_ref, wb_ref, kdf_ref, kdb_ref, bf_ref, bb_ref, v_ref, kk_ref, r_ref, gdo_ref):
    i = pl.program_id(1)
    first = jnp.logical_or(i == 0, i == nct)
    last = jnp.logical_or(i == nct - 1, i == nt - 1)
    row = lax.broadcasted_iota(jnp.int32, (TM, 1), 0)
    c = rkv_ref.shape[2] // 3

    def mix(main, prev8, next8, lo, hi):
        pm = main[0]
        prow = jnp.where(first, 0.0, prev8[0, SUBLANES - 1:SUBLANES, :])
        nrow = jnp.where(last, 0.0, next8[0, 0:1, :])
        prev = jnp.where(row == 0, prow, pltpu.roll(pm, 1, 0))
        nxt = jnp.where(row == TM - 1, nrow, pltpu.roll(pm, TM - 1, 0))
        return pm + mup_ref[:, lo:hi] * (prev - pm) + mun_ref[:, lo:hi] * (nxt - pm)

    rkv = mix(rkv_ref, rkv_p, rkv_n, 0, 3 * c)
    lora = mix(lo_ref, lo_p, lo_n, 3 * c, 3 * c + 2 * LANES)
    gdo_ref[0] = mix(gd_ref, gd_p, gd_n, 3 * c + 2 * LANES, 3 * c + 4 * LANES)

    r, k, v = rkv[:, 0:c], rkv[:, c:2 * c], rkv[:, 2 * c:3 * c]
    wd, ad = lora[:, 0:LANES], lora[:, LANES:2 * LANES]
    dec = _dot_hi(jnp.tanh(wd), du_ref[...]) + w0_ref[...]
    softplus_neg = jnp.maximum(-dec, 0.0) + jnp.log1p(jnp.exp(-jnp.abs(dec)))
    decay = jnp.exp(-jnp.exp(-softplus_neg - 0.5))
    a = jax.nn.sigmoid(_dot_hi(ad, au_ref[...]) + a0_ref[...])
    kk = k * kkw_ref[...]
    kk = kk * lax.rsqrt(jnp.maximum(_dot_hi(kk * kk, ones_ref[...]), KK_EPS))
    ka = ka_ref[...]
    for z, (w_o, kd_o, b_o) in enumerate(((wf_ref, kdf_ref, bf_ref), (wb_ref, kdb_ref, bb_ref))):
        az = a[:, z * c:(z + 1) * c]
        w_o[0] = decay[:, z * c:(z + 1) * c]
        kd_o[0] = k * (1.0 + (az - 1.0) * ka)
        b_o[0] = kk * az
    v_ref[0] = v
    kk_ref[0] = kk
    r_ref[0] = r


def _rwkv_prep(p, lw, nct):
    bsz, n, _ = p.shape
    c = lw["k_k"].shape[1]
    nt = n // TM
    tok = jax.ShapeDtypeStruct((bsz, n, c), F32)
    vec = lambda a: _const_spec(a.shape)
    consts = [lw["mu_prev"], lw["mu_next"], lw["decay_up"], lw["decay_w0"], lw["iclr_up"], lw["iclr_a0"],
              lw["k_k"], lw["k_a"], lw["ones_bd"]]
    return pl.pallas_call(
        functools.partial(_prep_kernel, nct, nt),
        grid=(bsz, nt),
        in_specs=_halo_specs(3 * c, 0, n) + _halo_specs(2 * LANES, (6 * c) // (2 * LANES), n)
        + _halo_specs(2 * LANES, (6 * c) // (2 * LANES) + 1, n) + [vec(a) for a in consts],
        out_specs=[pl.BlockSpec((1, TM, c), lambda b, i: (b, i, 0))] * 9
        + [pl.BlockSpec((1, TM, 2 * LANES), lambda b, i: (b, i, 0))],
        out_shape=[tok] * 9 + [jax.ShapeDtypeStruct((bsz, n, 2 * LANES), F32)],
        compiler_params=_params("parallel", "parallel"),
        name="rwkv_prep",
    )(p, p, p, p, p, p, p, p, p, *consts)


def _scan_kernel(wf, kdf, bf, vf, kkf, rf, wb, kdb, bb, vb, kkb, rb, yf_ref, yb_ref, s_ref):
    @pl.when(pl.program_id(0) == 0)
    def _():
        s_ref[...] = jnp.zeros_like(s_ref)

    yf_ref[...] = jnp.zeros_like(yf_ref)
    yb_ref[...] = jnp.zeros_like(yb_ref)

    bsz, tc, c = wf.shape
    sub = lax.broadcasted_iota(jnp.int32, (HEAD, LANES), 0)
    lane_in_head = jnp.bitwise_and(lax.broadcasted_iota(jnp.int32, (HEAD, LANES), 1), HEAD - 1)
    diag = (lane_in_head == sub).astype(F32).astype(BF16)
    krow = jnp.bitwise_and(lax.broadcasted_iota(jnp.int32, (2 * LANES, LANES), 0), LANES - 1)
    kcol = lax.broadcasted_iota(jnp.int32, (2 * LANES, LANES), 1)
    red2 = ((krow >= HEAD) == (kcol >= HEAD)).astype(F32).astype(BF16)
    red1 = red2[:LANES]

    dirs = ((wf, kdf, bf, vf, kkf, rf, yf_ref), (wb, kdb, bb, vb, kkb, rb, yb_ref))
    tiles = [(d, b, p) for d in range(2) for b in range(bsz) for p in range(c // LANES)]
    groups = tc // SUBLANES

    def body(gi, carry):
        base = [pl.multiple_of(gi * SUBLANES, SUBLANES), pl.multiple_of((groups - 1 - gi) * SUBLANES, SUBLANES)]
        blk = [bs // HEAD for bs in base]
        ops, st, acc = {}, {}, {}
        for (d, b, p) in tiles:
            ls = slice(p * LANES, (p + 1) * LANES)
            ops[d, b, p] = [ref[b, pl.ds(base[d], SUBLANES), ls] for ref in dirs[d][:6]]
            st[d, b, p] = s_ref[b, d, p]
            acc[d, b, p] = dirs[d][6][b, blk[d], p]
        for jj in range(SUBLANES):
            jd = [jj, SUBLANES - 1 - jj]
            here = [lane_in_head == jnp.bitwise_and(base[d] + jd[d], HEAD - 1) for d in range(2)]
            lhs, vds = [], []
            for k in tiles:
                w8, kd8, b8, v8, kk8, r8 = ops[k]
                j = jd[k[0]]
                p1 = st[k] * kk8[j:j + 1]
                hi = p1.astype(BF16)
                lo = (p1 - hi.astype(F32)).astype(BF16)
                vds.append(v8[j:j + 1].astype(BF16) * diag)
                lhs.append(jnp.concatenate([hi, lo], axis=1))
            vc = jnp.dot(jnp.concatenate(vds, axis=0), red1, preferred_element_type=F32)
            sa = jnp.dot(jnp.concatenate(lhs, axis=0), red2, preferred_element_type=F32)
            p2 = []
            for i, k in enumerate(tiles):
                w8, kd8, b8, v8, kk8, r8 = ops[k]
                j = jd[k[0]]
                part = slice(i * HEAD, (i + 1) * HEAD)
                st[k] = st[k] * w8[j:j + 1] + vc[part] * kd8[j:j + 1] - sa[part] * b8[j:j + 1]
                p2.append((st[k] * r8[j:j + 1]).astype(BF16))
            yrep = jnp.dot(jnp.concatenate(p2, axis=0), red1, preferred_element_type=F32)
            for i, k in enumerate(tiles):
                acc[k] = jnp.where(here[k[0]], yrep[i * HEAD:(i + 1) * HEAD], acc[k])
        for (d, b, p) in tiles:
            s_ref[b, d, p] = st[d, b, p]
            dirs[d][6][b, blk[d], p] = acc[d, b, p]
        return carry

    lax.fori_loop(0, groups, body, 0)


def _rwkv_scan(w_f, w_b, kd_f, kd_b, b_f, b_b, v, kk, r, nct_c):
    bsz, n, c = v.shape
    nc = n // TC

    def rev(g):
        return jnp.where(g < nct_c, nct_c - 1 - g, nc - 1 - g + nct_c)

    fwd = pl.BlockSpec((bsz, TC, c), lambda g: (0, g, 0))
    bwd = pl.BlockSpec((bsz, TC, c), lambda g: (0, rev(g), 0))
    pairs = c // LANES
    yshape = (bsz, TC // HEAD, pairs, HEAD, LANES)
    yt = jax.ShapeDtypeStruct((bsz, n // HEAD, pairs, HEAD, LANES), F32)
    y_f, y_b = pl.pallas_call(
        _scan_kernel,
        grid=(nc,),
        in_specs=[fwd] * 6 + [bwd] * 6,
        out_specs=[pl.BlockSpec(yshape, lambda g: (0, g, 0, 0, 0)), pl.BlockSpec(yshape, lambda g: (0, rev(g), 0, 0, 0))],
        out_shape=[yt, yt],
        scratch_shapes=[pltpu.VMEM((bsz, 2, pairs, HEAD, LANES), F32)],
        compiler_params=_params("arbitrary"),
        name="rwkv_scan",
    )(w_f, kd_f, b_f, v, kk, r, w_b, kd_b, b_b, v, kk, r)

    def untranspose(y):
        y = y.reshape(bsz, n // HEAD, pairs, HEAD, LANES // HEAD, HEAD)
        return y.transpose(0, 1, 5, 2, 4, 3).reshape(bsz, n, c)

    return untranspose(y_f), untranspose(y_b)


def _readout_kernel(yf, yb, r, kdf, kdb, v, gd, gnw, gnb, rk, gup, ones_ref, a_ref):
    ones = ones_ref[...]
    inv = 1.0 / HEAD
    y = yf[0] + yb[0]
    yc = y - _dot_hi(y, ones) * inv
    var = _dot_hi(yc * yc, ones) * inv
    yn = yc * lax.rsqrt(var + GN_EPS) * gnw[...] + gnb[...]
    bonus = _dot_hi(r[0] * rk[...] * (kdf[0] + kdb[0]), ones) * v[0]
    g = _dot_hi(jax.nn.sigmoid(gd[0]), gup[...])
    a_ref[0] = (yn + bonus) * g


def _rwkv_readout(y_f, y_b, r, kd_f, kd_b, v, gd, lw):
    bsz, n, c = v.shape
    tok = pl.BlockSpec((1, TM, c), lambda b, i: (b, i, 0))
    consts = [lw["gn_w"], lw["gn_b"], lw["r_k"], lw["gate_up"], lw["ones_bd"]]
    return pl.pallas_call(
        _readout_kernel,
        grid=(bsz, n // TM),
        in_specs=[tok] * 6 + [pl.BlockSpec((1, TM, 2 * LANES), lambda b, i: (b, i, 0))]
        + [_const_spec(a.shape) for a in consts],
        out_specs=tok,
        out_shape=jax.ShapeDtypeStruct((bsz, n, c), F32),
        compiler_params=_params("parallel", "parallel"),
        name="rwkv_readout",
    )(y_f, y_b, r, kd_f, kd_b, v, gd, *consts)


def _pool_kernel(nct, len_ctx, len_lat, main, prev8, next8, pw_ref, scale_ref, o_ref):
    i = pl.program_id(1)
    is_lat = i >= nct
    seq_len = jnp.where(is_lat, len_lat, len_ctx)
    t0 = jnp.where(is_lat, i - nct, i) * TM
    n = TM + 2 * SUBLANES
    pm = main[0]
    ext = jnp.concatenate([prev8[0], pm, next8[0]], axis=0)
    pos = t0 - SUBLANES + lax.broadcasted_iota(jnp.int32, (n, 1), 0)
    e = jnp.where(jnp.logical_and(pos >= 0, pos < seq_len), ext, 0.0)
    a2 = e + pltpu.roll(e, 1, 0)
    a4 = pltpu.roll(a2, 1, 0) + pltpu.roll(a2, n - 1, 0)
    a8 = pltpu.roll(a4, 2, 0) + pltpu.roll(a4, n - 2, 0)
    a16 = pltpu.roll(a8, 4, 0) + pltpu.roll(a8, n - 4, 0)
    t = t0 + lax.broadcasted_iota(jnp.int32, (TM, 1), 0)
    lane = lax.broadcasted_iota(jnp.int32, pm.shape, 1)
    mean = None
    for g, (w, acc) in reversed(list(enumerate(zip(POOL_WINDOWS, (a2, a4, a8, a16))))):
        lo = jnp.maximum(t - w // 2, 0)
        hi = jnp.minimum(t + (w - w // 2) - 1, seq_len - 1)
        m = acc[SUBLANES:SUBLANES + TM] / (hi - lo + 1).astype(F32)
        mean = m if mean is None else jnp.where(lane < (g + 1) * POOL_GROUP, m, mean)
    o_ref[0] = _dot_hi(mean - pm, pw_ref[...]) * scale_ref[...]


def _pool(p, lw, nct, len_ctx, len_lat, col):
    bsz, n, _ = p.shape
    width = lw["pool_w"].shape[0]
    return pl.pallas_call(
        functools.partial(_pool_kernel, nct, len_ctx, len_lat),
        grid=(bsz, n // TM),
        in_specs=_halo_specs(width, col, n) + [_const_spec(lw["pool_w"].shape), _const_spec(lw["pool_scale"].shape)],
        out_specs=pl.BlockSpec((1, TM, width), lambda b, i: (b, i, 0)),
        out_shape=jax.ShapeDtypeStruct((bsz, n, width), F32),
        compiler_params=_params("parallel", "parallel"),
        name="pool",
    )(p, p, p, lw["pool_w"], lw["pool_scale"])


def _qk_norm_kernel(q_ref, k_ref, v_ref, qg_ref, kg_ref, ones_ref, qo, ko, vo):
    ones = ones_ref[...]
    inv = 1.0 / HEAD
    q, k = q_ref[0], k_ref[0]
    qn = q * lax.rsqrt(_dot_hi(q * q, ones) * inv + NORM_EPS) * qg_ref[...]
    kn = k * lax.rsqrt(_dot_hi(k * k, ones) * inv + NORM_EPS) * kg_ref[...]
    qo[0] = (qn * HEAD ** -0.5).astype(BF16)
    ko[0] = kn.astype(BF16)
    vo[0] = v_ref[0].astype(BF16)


def _qk_norm(p, lw, col):
    bsz, n, _ = p.shape
    c = lw["q_gain"].shape[1]
    tok = jax.ShapeDtypeStruct((bsz, n, c), BF16)
    ospec = pl.BlockSpec((1, TM, c), lambda b, i: (b, i, 0))
    consts = [lw["q_gain"], lw["k_gain"], lw["ones_bd"]]
    return pl.pallas_call(
        _qk_norm_kernel,
        grid=(bsz, n // TM),
        in_specs=[pl.BlockSpec((1, TM, c), lambda b, i, j=j: (b, i, col + j)) for j in range(3)]
        + [_const_spec(a.shape) for a in consts],
        out_specs=[ospec] * 3,
        out_shape=[tok] * 3,
        compiler_params=_params("parallel", "parallel"),
        name="qk_norm",
    )(p, p, p, *consts)


def _softmax_pv(parts):
    m = functools.reduce(jnp.maximum, [jnp.max(s, axis=-1, keepdims=True) for s, _ in parts])
    ps = [jnp.exp(s - m) for s, _ in parts]
    den = functools.reduce(jnp.add, [jnp.sum(p, axis=-1, keepdims=True) for p in ps])
    num = functools.reduce(jnp.add, [jnp.dot(p.astype(BF16), v, preferred_element_type=F32)
                                      for p, (_, v) in zip(ps, parts)])
    return num / den


def _nat_kernel(rows, q_ref, kp, kc, kn, vp, vc, vn, kx_ref, vx_ref, bias_ref, o_ref, ks, vs):
    i = pl.program_id(1)
    tq = q_ref.shape[1]
    rb = tq // GRID_W
    nloc = WIN_H * GRID_W
    for j, (kr, vr) in enumerate(((kp, vp), (kc, vc), (kn, vn))):
        ks[j * tq:(j + 1) * tq, :] = kr[0]
        vs[j * tq:(j + 1) * tq, :] = vr[0]
    for rr in range(rb):
        r = i * rb + rr
        rs = jnp.clip(r - WIN_H // 2, 0, rows - WIN_H)
        off = r - rs
        start = pl.multiple_of((rs - i * rb + rb) * GRID_W, GRID_W)
        qs = slice(rr * GRID_W, (rr + 1) * GRID_W)
        for h in range(q_ref.shape[2] // HEAD):
            hl = slice(h * HEAD, (h + 1) * HEAD)
            q = q_ref[0, qs, hl]
            s_loc = _dot_nt(q, ks[pl.ds(start, nloc), hl]) + bias_ref[h, off]
            s_ctx = _dot_nt(q, kx_ref[0, :, hl])
            o_ref[0, qs, hl] = _softmax_pv([(s_loc, vs[pl.ds(start, nloc), hl]), (s_ctx, vx_ref[0, :, hl])])


def _nat_latent(qn, kn, vn, bias, len_ctx, len_lat):
    bsz, n, c = qn.shape
    tq = len_ctx
    assert tq == (WIN_H // 2) * GRID_W and len_lat % tq == 0
    rows = len_lat // GRID_W
    nblk = len_lat // tq

    def blk(shift):
        return pl.BlockSpec((1, tq, c), lambda b, i: (b, 1 + jnp.clip(i + shift, 0, nblk - 1), 0))

    ctx = pl.BlockSpec((1, tq, c), lambda b, i: (b, 0, 0))
    return pl.pallas_call(
        functools.partial(_nat_kernel, rows),
        grid=(bsz, nblk),
        in_specs=[blk(0), blk(-1), blk(0), blk(1), blk(-1), blk(0), blk(1), ctx, ctx, _const_spec(bias.shape)],
        out_specs=pl.BlockSpec((1, tq, c), lambda b, i: (b, i, 0)),
        out_shape=jax.ShapeDtypeStruct((bsz, len_lat, c), F32),
        scratch_shapes=[pltpu.VMEM((3 * tq, c), BF16), pltpu.VMEM((3 * tq, c), BF16)],
        compiler_params=_params("parallel", "parallel"),
        name="nat_latent",
    )(qn, kn, kn, kn, vn, vn, vn, kn, vn, bias)


def _ctx_attn_kernel(q_ref, k_ref, v_ref, o_ref):
    for h in range(q_ref.shape[2] // HEAD):
        hl = slice(h * HEAD, (h + 1) * HEAD)
        o_ref[0, :, hl] = _softmax_pv([(_dot_nt(q_ref[0, :, hl], k_ref[0, :, hl]), v_ref[0, :, hl])])


def _ctx_attn(qn, kn, vn, len_ctx):
    bsz, _, c = qn.shape
    spec = pl.BlockSpec((1, len_ctx, c), lambda b: (b, 0, 0))
    return pl.pallas_call(
        _ctx_attn_kernel,
        grid=(bsz,),
        in_specs=[spec] * 3,
        out_specs=spec,
        out_shape=jax.ShapeDtypeStruct((bsz, len_ctx, c), F32),
        compiler_params=_params("parallel"),
        name="ctx_attn",
    )(qn, kn, vn)


def _nat_bias_table(rpb):
    qc = np.arange(GRID_W)[:, None]
    kc = np.arange(GRID_W)[None, :]
    cs = np.clip(qc - WIN_W // 2, 0, GRID_W - WIN_W)
    valid = (kc >= cs) & (kc < cs + WIN_W)
    dc = kc - qc + WIN_W - 1
    pick = (np.arange(2 * WIN_W - 1)[:, None, None] == dc[None]) & valid[None]
    cols = jnp.einsum("hdm,mqk->hdqk", rpb, jnp.asarray(pick, F32), precision=lax.Precision.HIGHEST)
    cols = jnp.where(valid[None, None], cols, MASK_BIAS)
    t = jnp.stack([cols[:, WIN_H - 1 - off:2 * WIN_H - 1 - off] for off in range(WIN_H)], axis=1)
    return t.transpose(0, 1, 3, 2, 4).reshape(rpb.shape[0], WIN_H, GRID_W, WIN_H * GRID_W)


def _merge_kernel(a, bp, cn, ga, gb, gc, x, g1, wa, wb, wc, wo, o_ref):
    m = (jax.nn.sigmoid(ga[0]) * _dot_bf(a[0], wa[...])
         + jax.nn.sigmoid(gb[0]) * _dot_bf(bp[0], wb[...])
         + jax.nn.sigmoid(gc[0]) * _dot_bf(cn[0], wc[...]))
    o_ref[0] = x[0] + g1[0] * _dot_bf(m, wo[...])


def _merge(a, bp, cn, p, x, mod, lw, nct, gate_col):
    bsz, n, d = x.shape
    tok = lambda w, col=0: pl.BlockSpec((1, TM, w), lambda b, i: (b, i, col))
    ws = [lw["w_rwkv_o"], lw["w_pool_o"], lw["w_nat_o"], lw["w_out"]]
    return pl.pallas_call(
        _merge_kernel,
        grid=(bsz, n // TM),
        in_specs=[tok(a.shape[2]), tok(bp.shape[2]), tok(cn.shape[2]),
                  tok(d, gate_col), tok(d, gate_col + 1), tok(d, gate_col + 2), tok(d),
                  pl.BlockSpec((1, 1, d), _mod_spec(nct, 2))] + [_const_spec(w.shape) for w in ws],
        out_specs=tok(d),
        out_shape=jax.ShapeDtypeStruct((bsz, n, d), F32),
        compiler_params=_params("parallel", "parallel"),
        name="merge",
    )(a, bp, cn, p, p, p, x, mod, *ws)


def _ffn_kernel(x_ref, sh, sc, g2, gain, w1, w2, o_ref):
    x = x_ref[0]
    h = _norm_mod(x, gain[...], sh[0], sc[0])
    u = jnp.dot(h.astype(BF16), w1[...], preferred_element_type=F32)
    hid = w2.shape[0]
    gate, up = u[:, :hid], u[:, hid:]
    act = gate * jax.nn.sigmoid(gate) * up
    o_ref[0] = x + g2[0] * jnp.dot(act.astype(BF16), w2[...], preferred_element_type=F32)


def _ffn(x, mod, lw, nct):
    bsz, n, d = x.shape
    tok = pl.BlockSpec((1, TM, d), lambda b, i: (b, i, 0))
    mspec = lambda k: pl.BlockSpec((1, 1, d), _mod_spec(nct, k))
    return pl.pallas_call(
        _ffn_kernel,
        grid=(bsz, n // TM),
        in_specs=[tok, mspec(3), mspec(4), mspec(5), _const_spec(lw["norm2"].shape),
                  _const_spec(lw["w_ffn_in"].shape), _const_spec(lw["w_ffn_out"].shape)],
        out_specs=tok,
        out_shape=jax.ShapeDtypeStruct((bsz, n, d), F32),
        compiler_params=_params("parallel", "parallel"),
        name="ffn",
    )(x, mod, mod, mod, lw["norm2"], lw["w_ffn_in"], lw["w_ffn_out"])


def _block_diag(blocks):
    n = len(blocks)
    rows = []
    for i, blk in enumerate(blocks):
        rows.append(jnp.concatenate([blk if j == i else jnp.zeros((blk.shape[0], blocks[j].shape[1]), blk.dtype)
                                     for j in range(n)], axis=1))
    return jnp.concatenate(rows, axis=0)


def _pad_cols(a, width):
    return jnp.pad(a, ((0, 0), (0, width - a.shape[1])))


def _layer_weights(l, prm):
    c = prm["k_k"].shape[1]
    lora = prm["decay_up"].shape[2]
    gl = prm["gate_up"].shape[1]
    pool = prm["pool_scale"].shape[1]
    d = prm["w_out"].shape[1]
    w_in = prm["w_in"][l]
    o_lora, o_gd, o_pool = 3 * c, 3 * c + 4 * lora, 3 * c + 4 * lora + gl
    o_q = o_pool + pool
    o_gate = o_q + 3 * c
    assert 4 * lora == 2 * LANES and gl <= 2 * LANES and pool == 2 * LANES and o_gate + 3 * d == w_in.shape[1]
    w_perm = jnp.concatenate([
        w_in[:, 0:o_lora], w_in[:, o_q:o_gate], w_in[:, o_lora:o_gd],
        _pad_cols(w_in[:, o_gd:o_pool], 2 * LANES), w_in[:, o_pool:o_q], w_in[:, o_gate:]], axis=1).astype(BF16)
    mu = lambda m: _pad_cols(m[l][None, :o_pool], o_pool + 2 * LANES - gl)
    heads = c // HEAD
    row = lambda a: a.reshape(1, -1)
    return {
        "w_in": w_perm,
        "norm1": row(prm["norm1"][l]), "norm2": row(prm["norm2"][l]),
        "mu_prev": mu(prm["mu_prev"]), "mu_next": mu(prm["mu_next"]),
        "decay_up": _block_diag([prm["decay_up"][l, 0], prm["decay_up"][l, 1]]),
        "decay_w0": row(prm["decay_w0"][l]),
        "iclr_up": _block_diag([prm["iclr_up"][l, 0], prm["iclr_up"][l, 1]]),
        "iclr_a0": row(prm["iclr_a0"][l]),
        "k_k": row(prm["k_k"][l]), "k_a": row(prm["k_a"][l]), "r_k": row(prm["r_k"][l]),
        "gn_w": row(prm["gn_w"][l]), "gn_b": row(prm["gn_b"][l]),
        "gate_up": jnp.pad(prm["gate_up"][l], ((0, 2 * LANES - gl), (0, 0))),
        "ones_bd": jnp.kron(jnp.eye(heads, dtype=F32), jnp.ones((HEAD, HEAD), F32)),
        "pool_w": _block_diag([prm["pool_w"][l, g] for g in range(len(POOL_WINDOWS))]),
        "pool_scale": row(prm["pool_scale"][l]),
        "q_gain": row(jnp.tile(prm["q_gain"][l], heads)), "k_gain": row(jnp.tile(prm["k_gain"][l], heads)),
        "nat_bias": _nat_bias_table(prm["rpb"][l]),
        "w_rwkv_o": prm["w_rwkv_o"][l].astype(BF16), "w_pool_o": prm["w_pool_o"][l].astype(BF16),
        "w_nat_o": prm["w_nat_o"][l].astype(BF16), "w_out": prm["w_out"][l].astype(BF16),
        "w_ffn_in": prm["w_ffn_in"][l].astype(BF16), "w_ffn_out": prm["w_ffn_out"][l].astype(BF16),
    }


def kernel(x, c, ctx, c_ctx, w_mod, b_mod, norm1, norm2, w_in, mu_prev, mu_next, decay_w0, decay_up, iclr_a0, iclr_up, gate_up, k_k, k_a, r_k, gn_w, gn_b, pool_w, pool_scale, q_gain, k_gain, rpb, w_rwkv_o, w_pool_o, w_nat_o, w_out, w_ffn_in, w_ffn_out):
    prm = dict(norm1=norm1, norm2=norm2, w_in=w_in, mu_prev=mu_prev, mu_next=mu_next, decay_w0=decay_w0,
               decay_up=decay_up, iclr_a0=iclr_a0, iclr_up=iclr_up, gate_up=gate_up, k_k=k_k, k_a=k_a, r_k=r_k,
               gn_w=gn_w, gn_b=gn_b, pool_w=pool_w, pool_scale=pool_scale, q_gain=q_gain, k_gain=k_gain, rpb=rpb,
               w_rwkv_o=w_rwkv_o, w_pool_o=w_pool_o, w_nat_o=w_nat_o, w_out=w_out, w_ffn_in=w_ffn_in,
               w_ffn_out=w_ffn_out)
    bsz, len_lat, d = x.shape
    len_ctx = ctx.shape[1]
    depth = w_mod.shape[0]
    assert len_ctx % TM == 0 and len_lat % TM == 0 and bsz + 1 <= SUBLANES
    nct = len_ctx // TM
    cdim = k_k.shape[1]

    s_rows = jnp.concatenate([c, c_ctx[None, :], jnp.zeros((SUBLANES - bsz - 1, d), F32)], axis=0)
    mod_all = _modulation(s_rows, w_mod, b_mod)
    xa = jnp.concatenate([ctx, x], axis=1)

    for l in range(depth):
        lw = _layer_weights(l, prm)
        m_lat = mod_all[l, :bsz]
        m_ctx = jnp.broadcast_to(mod_all[l, bsz][None], m_lat.shape)
        mod = jnp.stack([m_ctx, m_lat], axis=1).reshape(bsz * 2 * 6, 1, d)

        p = _in_proj(xa, mod, lw["norm1"], lw["w_in"], nct)
        w_f, w_b, kd_f, kd_b, b_f, b_b, v, kk, r, gd = _rwkv_prep(p, lw, nct)
        y_f, y_b = _rwkv_scan(w_f, w_b, kd_f, kd_b, b_f, b_b, v, kk, r, len_ctx // TC)
        a_br = _rwkv_readout(y_f, y_b, r, kd_f, kd_b, v, gd, lw)
        b_br = _pool(p, lw, nct, len_ctx, len_lat, (6 * cdim + 4 * LANES) // (2 * LANES))
        qn, kn, vn = _qk_norm(p, lw, 3)
        c_br = jnp.concatenate([_ctx_attn(qn, kn, vn, len_ctx),
                                _nat_latent(qn, kn, vn, lw["nat_bias"], len_ctx, len_lat)], axis=1)
        xa = _merge(a_br, b_br, c_br, p, xa, mod, lw, nct, (6 * cdim + 6 * LANES) // d)
        xa = _ffn(xa, mod, lw, nct)
    return xa[:, len_ctx:]
```

```python
import functools

import numpy as np
import jax
import jax.numpy as jnp
from jax import lax
from jax.experimental import pallas as pl
from jax.experimental.pallas import tpu as pltpu

F32 = jnp.float32
BF16 = jnp.bfloat16

HEAD = 64
NORM_EPS = 1e-6
GN_EPS = 64e-5
KK_EPS = 1e-24
POOL_WINDOWS = (2, 4, 8, 16)
POOL_GROUP = 64
GRID_W = 64
WIN_H = 8
WIN_W = 16
MASK_BIAS = -1e30

LANES = 128
SUBLANES = 8
VMEM_LIMIT = 56 * 1024 * 1024

TM = 256
CHUNK = 64
TN_IN = 2048
TN_MOD = 1536


def _dot_hi(a, b):
    return jnp.dot(a, b, precision=lax.Precision.HIGHEST, preferred_element_type=F32)


def _dot_bf(a, b):
    return jnp.dot(a.astype(BF16), b.astype(BF16), preferred_element_type=F32)


def _dot_nt(a, b):
    return lax.dot_general(a, b, (((1,), (1,)), ((), ())), preferred_element_type=F32)


def _params(*sem):
    return pltpu.CompilerParams(dimension_semantics=sem, vmem_limit_bytes=VMEM_LIMIT)


def _const_spec(shape):
    nd = len(shape)
    return pl.BlockSpec(shape, lambda *_: (0,) * nd, pipeline_mode=pl.Buffered(1))


def _mod_kernel(s_ref, w_ref, b_ref, o_ref):
    s = s_ref[...]
    s = s * jax.nn.sigmoid(s)
    o_ref[0] = _dot_hi(s, w_ref[0]) + b_ref[0]


def _modulation(s_rows, w_mod, b_mod):
    depth, d, n = w_mod.shape
    return pl.pallas_call(
        _mod_kernel,
        grid=(depth, n // TN_MOD),
        in_specs=[
            pl.BlockSpec((SUBLANES, d), lambda l, j: (0, 0)),
            pl.BlockSpec((1, d, TN_MOD), lambda l, j: (l, 0, j)),
            pl.BlockSpec((1, 1, TN_MOD), lambda l, j: (l, 0, j)),
        ],
        out_specs=pl.BlockSpec((1, SUBLANES, TN_MOD), lambda l, j: (l, 0, j)),
        out_shape=jax.ShapeDtypeStruct((depth, SUBLANES, n), F32),
        compiler_params=_params("parallel", "parallel"),
        name="modulation",
    )(s_rows, w_mod, b_mod.reshape(depth, 1, n))


def _mod_spec(nct, k, lead=0):
    def index(*ids):
        b, i = ids[lead], ids[lead + 1]
        return ((b * 2 + (i >= nct).astype(jnp.int32)) * 6 + k, 0, 0)

    return index


def _norm_mod(x, gain, shift, scale):
    ms = jnp.mean(x * x, axis=-1, keepdims=True)
    return x * lax.rsqrt(ms + NORM_EPS) * gain * (1.0 + scale) + shift


def _in_proj_kernel(x_ref, sh_ref, sc_ref, g_ref, w_ref, o_ref):
    h = _norm_mod(x_ref[0], g_ref[...], sh_ref[0], sc_ref[0])
    o_ref[0] = jnp.dot(h.astype(BF16), w_ref[...], preferred_element_type=F32)


def _in_proj(x, mod, gain, w, nct):
    bsz, n, d = x.shape
    nout = w.shape[1]
    mspec = lambda k: pl.BlockSpec((1, 1, d), _mod_spec(nct, k, lead=1))
    return pl.pallas_call(
        _in_proj_kernel,
        grid=(nout // TN_IN, bsz, n // TM),
        in_specs=[
            pl.BlockSpec((1, TM, d), lambda j, b, i: (b, i, 0)),
            mspec(0), mspec(1),
            pl.BlockSpec((1, d), lambda j, b, i: (0, 0)),
            pl.BlockSpec((d, TN_IN), lambda j, b, i: (0, j)),
        ],
        out_specs=pl.BlockSpec((1, TM, TN_IN), lambda j, b, i: (b, i, j)),
        out_shape=jax.ShapeDtypeStruct((bsz, n, nout), F32),
        compiler_params=_params("parallel", "parallel", "parallel"),
        name="in_proj",
    )(x, mod, mod, gain, w)


def _halo_specs(width, col, n):
    nb = n // SUBLANES
    per = TM // SUBLANES
    return [
        pl.BlockSpec((1, TM, width), lambda b, i: (b, i, col)),
        pl.BlockSpec((1, SUBLANES, width), lambda b, i: (b, jnp.maximum(i * per - 1, 0), col)),
        pl.BlockSpec((1, SUBLANES, width), lambda b, i: (b, jnp.minimum((i + 1) * per, nb - 1), col)),
    ]


def _prep_kernel(nct, nt, rkv_ref, rkv_p, rkv_n, lo_ref, lo_p, lo_n, gd_ref, gd_p, gd_n,
                 mup_ref, mun_ref, du_ref, w0_ref, au_ref, a0_ref, kkw_ref, ka_ref, ones_ref,
                 wf_ref, wb_ref, kdf_ref, kdb_ref, bf_ref, bb_ref, v_ref, kk_ref, r_ref, gdo_ref):
    i = pl.program_id(1)
    first = jnp.logical_or(i == 0, i == nct)
    last = jnp.logical_or(i == nct - 1, i == nt - 1)
    row = lax.broadcasted_iota(jnp.int32, (TM, 1), 0)
    c = rkv_ref.shape[2] // 3

    def mix(main, prev8, next8, lo, hi):
        pm = main[0]
        prow = jnp.where(first, 0.0, prev8[0, SUBLANES - 1:SUBLANES, :])
        nrow = jnp.where(last, 0.0, next8[0, 0:1, :])
        prev = jnp.where(row == 0, prow, pltpu.roll(pm, 1, 0))
        nxt = jnp.where(row == TM - 1, nrow, pltpu.roll(pm, TM - 1, 0))
        return pm + mup_ref[:, lo:hi] * (prev - pm) + mun_ref[:, lo:hi] * (nxt - pm)

    rkv = mix(rkv_ref, rkv_p, rkv_n, 0, 3 * c)
    lora = mix(lo_ref, lo_p, lo_n, 3 * c, 3 * c + 2 * LANES)
    gdo_ref[0] = mix(gd_ref, gd_p, gd_n, 3 * c + 2 * LANES, 3 * c + 4 * LANES)

    r, k, v = rkv[:, 0:c], rkv[:, c:2 * c], rkv[:, 2 * c:3 * c]
    wd, ad = lora[:, 0:LANES], lora[:, LANES:2 * LANES]
    dec = _dot_hi(jnp.tanh(wd), du_ref[...]) + w0_ref[...]
    softplus_neg = jnp.maximum(-dec, 0.0) + jnp.log1p(jnp.exp(-jnp.abs(dec)))
    log_decay = -jnp.exp(-softplus_neg - 0.5)
    a = jax.nn.sigmoid(_dot_hi(ad, au_ref[...]) + a0_ref[...])
    kk = k * kkw_ref[...]
    kk = kk * lax.rsqrt(jnp.maximum(_dot_hi(kk * kk, ones_ref[...]), KK_EPS))
    ka = ka_ref[...]
    for z, (w_o, kd_o, b_o) in enumerate(((wf_ref, kdf_ref, bf_ref), (wb_ref, kdb_ref, bb_ref))):
        az = a[:, z * c:(z + 1) * c]
        w_o[0] = log_decay[:, z * c:(z + 1) * c]
        kd_o[0] = k * (1.0 + (az - 1.0) * ka)
        b_o[0] = kk * az
    v_ref[0] = v
    kk_ref[0] = kk
    r_ref[0] = r


def _rwkv_prep(p, lw, nct):
    bsz, n, _ = p.shape
    c = lw["k_k"].shape[1]
    nt = n // TM
    tok = jax.ShapeDtypeStruct((bsz, n, c), F32)
    vec = lambda a: _const_spec(a.shape)
    consts = [lw["mu_prev"], lw["mu_next"], lw["decay_up"], lw["decay_w0"], lw["iclr_up"], lw["iclr_a0"],
              lw["k_k"], lw["k_a"], lw["ones_bd"]]
    return pl.pallas_call(
        functools.partial(_prep_kernel, nct, nt),
        grid=(bsz, nt),
        in_specs=_halo_specs(3 * c, 0, n) + _halo_specs(2 * LANES, (6 * c) // (2 * LANES), n)
        + _halo_specs(2 * LANES, (6 * c) // (2 * LANES) + 1, n) + [vec(a) for a in consts],
        out_specs=[pl.BlockSpec((1, TM, c), lambda b, i: (b, i, 0))] * 9
        + [pl.BlockSpec((1, TM, 2 * LANES), lambda b, i: (b, i, 0))],
        out_shape=[tok] * 9 + [jax.ShapeDtypeStruct((bsz, n, 2 * LANES), F32)],
        compiler_params=_params("parallel", "parallel"),
        name="rwkv_prep",
    )(p, p, p, p, p, p, p, p, p, *consts)


def _split(a):
    hi = a.astype(BF16)
    return hi, (a - hi.astype(F32)).astype(BF16)


def _mm(a, b, terms=1):
    if terms == 1:
        return jnp.dot(a.astype(BF16), b.astype(BF16), preferred_element_type=F32)
    bh, bl = _split(b)
    if terms == 2:
        ah = a.astype(BF16)
        return jnp.dot(jnp.concatenate([ah, ah], axis=1), jnp.concatenate([bh, bl], axis=0),
                       preferred_element_type=F32)
    ah, al = _split(a)
    return jnp.dot(jnp.concatenate([ah, al, ah], axis=1), jnp.concatenate([bh, bh, bl], axis=0),
                   preferred_element_type=F32)


def _chunk_scan_kernel(ldf, kdf, bf, vf, kkf, rf, ldb, kdb, bb, vb, kkb, rb, yf_ref, yb_ref, h_ref):
    @pl.when(pl.program_id(0) == 0)
    def _():
        h_ref[...] = jnp.zeros_like(h_ref)

    bsz, cs, c = ldf.shape
    t_i = lax.broadcasted_iota(jnp.int32, (cs, LANES), 0)
    lane = lax.broadcasted_iota(jnp.int32, (cs, LANES), 1)
    s_i = jnp.bitwise_and(lane, HEAD - 1)
    m_a = (lane < HEAD).astype(F32)
    m_b = 1.0 - m_a
    m_a2 = jnp.concatenate([m_a, m_a], axis=1)
    m_b2 = 1.0 - m_a2
    r2 = lax.broadcasted_iota(jnp.int32, (LANES, LANES), 0)
    l2 = lax.broadcasted_iota(jnp.int32, (LANES, LANES), 1)
    block = ((r2 >= HEAD) == (l2 >= HEAD)).astype(F32)
    eye = (r2 == l2).astype(F32)
    zeros = jnp.zeros((cs, LANES), F32)

    def bd(x):
        return jnp.concatenate([x * m_a, x * m_b], axis=0)

    def bd2(x):
        return jnp.concatenate([x * m_a2, x * m_b2], axis=0)

    def unit(d, ld, kd, b_, v, kk, r, h):
        before = (s_i < t_i) if d == 0 else (s_i > t_i)
        upto = (s_i <= t_i) if d == 0 else (s_i >= t_i)
        tri = upto.astype(F32).astype(BF16)
        lh, ll = _split(ld)
        big_l = jnp.dot(tri, jnp.concatenate([lh, ll], axis=0), preferred_element_type=F32)
        yield
        ltot = big_l[cs - 1:cs] if d == 0 else big_l[0:1]
        kap = kk * jnp.exp(big_l - ld)
        rt = r * jnp.exp(big_l)
        einv = jnp.exp(-big_l)
        kt, bt = kd * einv, b_ * einv
        efin = jnp.exp(ltot - big_l)
        khat, bhat = kd * efin, b_ * efin
        sc = _dot_nt(jnp.concatenate([kap, rt], axis=0).astype(BF16),
                     jnp.concatenate([kt * m_a, kt * m_b, bt * m_a, bt * m_b], axis=0).astype(BF16))
        yield
        a_k = jnp.where(before, sc[:cs, :LANES], 0.0)
        n_p = jnp.where(before, -sc[:cs, LANES:], 0.0)
        m_k = jnp.where(upto, sc[cs:, :LANES], 0.0)
        m_nb = jnp.where(upto, -sc[cs:, LANES:], 0.0)
        x = jnp.concatenate([kap, _mm(a_k, bd(v))], axis=1)
        yield
        for level in range(6):
            x = x + _mm(n_p, bd2(x), 2)
            if level < 5:
                n_p = _mm(n_p, bd(n_p))
            yield
        vz = jnp.concatenate([v, zeros], axis=1)
        uw = jnp.concatenate([x[:, LANES:], x[:, :LANES]], axis=1)
        o1 = _mm(jnp.concatenate([m_k, m_nb], axis=1), jnp.concatenate([bd2(vz), bd2(uw)], axis=0))
        o2 = _mm(jnp.concatenate([khat, -bhat], axis=0).T, jnp.concatenate([vz, uw], axis=0))
        yield
        y0, q = o1[:, :LANES], rt + o1[:, LANES:]
        psi = o2[:, :LANES] * block
        phi = o2[:, LANES:] * block + eye * jnp.exp(ltot)
        y = y0 + _mm(q, h)
        return y, _mm(phi, h, 3) + psi

    dirs = ((ldf, kdf, bf, vf, kkf, rf, yf_ref), (ldb, kdb, bb, vb, kkb, rb, yb_ref))
    keys = [(d, b, p) for d in range(2) for b in range(bsz) for p in range(c // LANES)]
    lanes_of = lambda p: slice(p * LANES, (p + 1) * LANES)
    running = {k: unit(k[0], *[ref[k[1], :, lanes_of(k[2])] for ref in dirs[k[0]][:6]], h_ref[k[1], k[0], k[2]])
               for k in keys}
    while running:
        for k in list(running):
            try:
                next(running[k])
            except StopIteration as done:
                del running[k]
                (d, b, p), (y, h_new) = k, done.value
                dirs[d][6][b, :, lanes_of(p)] = y
                h_ref[b, d, p] = h_new


def _rwkv_scan(ld_f, ld_b, kd_f, kd_b, b_f, b_b, v, kk, r, nct_c):
    bsz, n, c = v.shape
    nc = n // CHUNK

    def rev(g):
        return jnp.where(g < nct_c, nct_c - 1 - g, nc - 1 - g + nct_c)

    fwd = pl.BlockSpec((bsz, CHUNK, c), lambda g: (0, g, 0))
    bwd = pl.BlockSpec((bsz, CHUNK, c), lambda g: (0, rev(g), 0))
    tok = jax.ShapeDtypeStruct((bsz, n, c), F32)
    return pl.pallas_call(
        _chunk_scan_kernel,
        grid=(nc,),
        in_specs=[fwd] * 6 + [bwd] * 6,
        out_specs=[fwd, bwd],
        out_shape=[tok, tok],
        scratch_shapes=[pltpu.VMEM((bsz, 2, c // LANES, LANES, LANES), F32)],
        compiler_params=_params("arbitrary"),
        name="rwkv_scan",
    )(ld_f, kd_f, b_f, v, kk, r, ld_b, kd_b, b_b, v, kk, r)


def _readout_kernel(yf, yb, r, kdf, kdb, v, gd, gnw, gnb, rk, gup, ones_ref, a_ref):
    ones = ones_ref[...]
    inv = 1.0 / HEAD
    y = yf[0] + yb[0]
    yc = y - _dot_hi(y, ones) * inv
    var = _dot_hi(yc * yc, ones) * inv
    yn = yc * lax.rsqrt(var + GN_EPS) * gnw[...] + gnb[...]
    bonus = _dot_hi(r[0] * rk[...] * (kdf[0] + kdb[0]), ones) * v[0]
    g = _dot_hi(jax.nn.sigmoid(gd[0]), gup[...])
    a_ref[0] = (yn + bonus) * g


def _rwkv_readout(y_f, y_b, r, kd_f, kd_b, v, gd, lw):
    bsz, n, c = v.shape
    tok = pl.BlockSpec((1, TM, c), lambda b, i: (b, i, 0))
    consts = [lw["gn_w"], lw["gn_b"], lw["r_k"], lw["gate_up"], lw["ones_bd"]]
    return pl.pallas_call(
        _readout_kernel,
        grid=(bsz, n // TM),
        in_specs=[tok] * 6 + [pl.BlockSpec((1, TM, 2 * LANES), lambda b, i: (b, i, 0))]
        + [_const_spec(a.shape) for a in consts],
        out_specs=tok,
        out_shape=jax.ShapeDtypeStruct((bsz, n, c), F32),
        compiler_params=_params("parallel", "parallel"),
        name="rwkv_readout",
    )(y_f, y_b, r, kd_f, kd_b, v, gd, *consts)


def _pool_kernel(nct, len_ctx, len_lat, main, prev8, next8, pw_ref, scale_ref, o_ref):
    i = pl.program_id(1)
    is_lat = i >= nct
    seq_len = jnp.where(is_lat, len_lat, len_ctx)
    t0 = jnp.where(is_lat, i - nct, i) * TM
    n = TM + 2 * SUBLANES
    pm = main[0]
    ext = jnp.concatenate([prev8[0], pm, next8[0]], axis=0)
    pos = t0 - SUBLANES + lax.broadcasted_iota(jnp.int32, (n, 1), 0)
    e = jnp.where(jnp.logical_and(pos >= 0, pos < seq_len), ext, 0.0)
    a2 = e + pltpu.roll(e, 1, 0)
    a4 = pltpu.roll(a2, 1, 0) + pltpu.roll(a2, n - 1, 0)
    a8 = pltpu.roll(a4, 2, 0) + pltpu.roll(a4, n - 2, 0)
    a16 = pltpu.roll(a8, 4, 0) + pltpu.roll(a8, n - 4, 0)
    t = t0 + lax.broadcasted_iota(jnp.int32, (TM, 1), 0)
    lane = lax.broadcasted_iota(jnp.int32, pm.shape, 1)
    mean = None
    for g, (w, acc) in reversed(list(enumerate(zip(POOL_WINDOWS, (a2, a4, a8, a16))))):
        lo = jnp.maximum(t - w // 2, 0)
        hi = jnp.minimum(t + (w - w // 2) - 1, seq_len - 1)
        m = acc[SUBLANES:SUBLANES + TM] / (hi - lo + 1).astype(F32)
        mean = m if mean is None else jnp.where(lane < (g + 1) * POOL_GROUP, m, mean)
    o_ref[0] = _dot_hi(mean - pm, pw_ref[...]) * scale_ref[...]


def _pool(p, lw, nct, len_ctx, len_lat, col):
    bsz, n, _ = p.shape
    width = lw["pool_w"].shape[0]
    return pl.pallas_call(
        functools.partial(_pool_kernel, nct, len_ctx, len_lat),
        grid=(bsz, n // TM),
        in_specs=_halo_specs(width, col, n) + [_const_spec(lw["pool_w"].shape), _const_spec(lw["pool_scale"].shape)],
        out_specs=pl.BlockSpec((1, TM, width), lambda b, i: (b, i, 0)),
        out_shape=jax.ShapeDtypeStruct((bsz, n, width), F32),
        compiler_params=_params("parallel", "parallel"),
        name="pool",
    )(p, p, p, lw["pool_w"], lw["pool_scale"])


def _qk_norm_kernel(q_ref, k_ref, v_ref, qg_ref, kg_ref, ones_ref, qo, ko, vo):
    ones = ones_ref[...]
    inv = 1.0 / HEAD
    q, k = q_ref[0], k_ref[0]
    qn = q * lax.rsqrt(_dot_hi(q * q, ones) * inv + NORM_EPS) * qg_ref[...]
    kn = k * lax.rsqrt(_dot_hi(k * k, ones) * inv + NORM_EPS) * kg_ref[...]
    qo[0] = (qn * HEAD ** -0.5).astype(BF16)
    ko[0] = kn.astype(BF16)
    vo[0] = v_ref[0].astype(BF16)


def _qk_norm(p, lw, col):
    bsz, n, _ = p.shape
    c = lw["q_gain"].shape[1]
    tok = jax.ShapeDtypeStruct((bsz, n, c), BF16)
    ospec = pl.BlockSpec((1, TM, c), lambda b, i: (b, i, 0))
    consts = [lw["q_gain"], lw["k_gain"], lw["ones_bd"]]
    return pl.pallas_call(
        _qk_norm_kernel,
        grid=(bsz, n // TM),
        in_specs=[pl.BlockSpec((1, TM, c), lambda b, i, j=j: (b, i, col + j)) for j in range(3)]
        + [_const_spec(a.shape) for a in consts],
        out_specs=[ospec] * 3,
        out_shape=[tok] * 3,
        compiler_params=_params("parallel", "parallel"),
        name="qk_norm",
    )(p, p, p, *consts)


def _softmax_pv(parts):
    m = functools.reduce(jnp.maximum, [jnp.max(s, axis=-1, keepdims=True) for s, _ in parts])
    ps = [jnp.exp(s - m) for s, _ in parts]
    den = functools.reduce(jnp.add, [jnp.sum(p, axis=-1, keepdims=True) for p in ps])
    num = functools.reduce(jnp.add, [jnp.dot(p.astype(BF16), v, preferred_element_type=F32)
                                      for p, (_, v) in zip(ps, parts)])
    return num / den


def _nat_kernel(rows, q_ref, kp, kc, kn, vp, vc, vn, kx_ref, vx_ref, bias_ref, o_ref, ks, vs):
    i = pl.program_id(1)
    tq = q_ref.shape[1]
    rb = tq // GRID_W
    nloc = WIN_H * GRID_W
    for j, (kr, vr) in enumerate(((kp, vp), (kc, vc), (kn, vn))):
        ks[j * tq:(j + 1) * tq, :] = kr[0]
        vs[j * tq:(j + 1) * tq, :] = vr[0]
    for rr in range(rb):
        r = i * rb + rr
        rs = jnp.clip(r - WIN_H // 2, 0, rows - WIN_H)
        off = r - rs
        start = pl.multiple_of((rs - i * rb + rb) * GRID_W, GRID_W)
        qs = slice(rr * GRID_W, (rr + 1) * GRID_W)
        for h in range(q_ref.shape[2] // HEAD):
            hl = slice(h * HEAD, (h + 1) * HEAD)
            q = q_ref[0, qs, hl]
            s_loc = _dot_nt(q, ks[pl.ds(start, nloc), hl]) + bias_ref[h, off]
            s_ctx = _dot_nt(q, kx_ref[0, :, hl])
            o_ref[0, qs, hl] = _softmax_pv([(s_loc, vs[pl.ds(start, nloc), hl]), (s_ctx, vx_ref[0, :, hl])])


def _nat_latent(qn, kn, vn, bias, len_ctx, len_lat):
    bsz, n, c = qn.shape
    tq = len_ctx
    assert tq == (WIN_H // 2) * GRID_W and len_lat % tq == 0
    rows = len_lat // GRID_W
    nblk = len_lat // tq

    def blk(shift):
        return pl.BlockSpec((1, tq, c), lambda b, i: (b, 1 + jnp.clip(i + shift, 0, nblk - 1), 0))

    ctx = pl.BlockSpec((1, tq, c), lambda b, i: (b, 0, 0))
    return pl.pallas_call(
        functools.partial(_nat_kernel, rows),
        grid=(bsz, nblk),
        in_specs=[blk(0), blk(-1), blk(0), blk(1), blk(-1), blk(0), blk(1), ctx, ctx, _const_spec(bias.shape)],
        out_specs=pl.BlockSpec((1, tq, c), lambda b, i: (b, i, 0)),
        out_shape=jax.ShapeDtypeStruct((bsz, len_lat, c), F32),
        scratch_shapes=[pltpu.VMEM((3 * tq, c), BF16), pltpu.VMEM((3 * tq, c), BF16)],
        compiler_params=_params("parallel", "parallel"),
        name="nat_latent",
    )(qn, kn, kn, kn, vn, vn, vn, kn, vn, bias)


def _ctx_attn_kernel(q_ref, k_ref, v_ref, o_ref):
    for h in range(q_ref.shape[2] // HEAD):
        hl = slice(h * HEAD, (h + 1) * HEAD)
        o_ref[0, :, hl] = _softmax_pv([(_dot_nt(q_ref[0, :, hl], k_ref[0, :, hl]), v_ref[0, :, hl])])


def _ctx_attn(qn, kn, vn, len_ctx):
    bsz, _, c = qn.shape
    spec = pl.BlockSpec((1, len_ctx, c), lambda b: (b, 0, 0))
    return pl.pallas_call(
        _ctx_attn_kernel,
        grid=(bsz,),
        in_specs=[spec] * 3,
        out_specs=spec,
        out_shape=jax.ShapeDtypeStruct((bsz, len_ctx, c), F32),
        compiler_params=_params("parallel"),
        name="ctx_attn",
    )(qn, kn, vn)


def _nat_bias_table(rpb):
    qc = np.arange(GRID_W)[:, None]
    kc = np.arange(GRID_W)[None, :]
    cs = np.clip(qc - WIN_W // 2, 0, GRID_W - WIN_W)
    valid = (kc >= cs) & (kc < cs + WIN_W)
    dc = kc - qc + WIN_W - 1
    pick = (np.arange(2 * WIN_W - 1)[:, None, None] == dc[None]) & valid[None]
    cols = jnp.einsum("hdm,mqk->hdqk", rpb, jnp.asarray(pick, F32), precision=lax.Precision.HIGHEST)
    cols = jnp.where(valid[None, None], cols, MASK_BIAS)
    t = jnp.stack([cols[:, WIN_H - 1 - off:2 * WIN_H - 1 - off] for off in range(WIN_H)], axis=1)
    return t.transpose(0, 1, 3, 2, 4).reshape(rpb.shape[0], WIN_H, GRID_W, WIN_H * GRID_W)


def _merge_kernel(a, bp, cn, ga, gb, gc, x, g1, wa, wb, wc, wo, o_ref):
    m = (jax.nn.sigmoid(ga[0]) * _dot_bf(a[0], wa[...])
         + jax.nn.sigmoid(gb[0]) * _dot_bf(bp[0], wb[...])
         + jax.nn.sigmoid(gc[0]) * _dot_bf(cn[0], wc[...]))
    o_ref[0] = x[0] + g1[0] * _dot_bf(m, wo[...])


def _merge(a, bp, cn, p, x, mod, lw, nct, gate_col):
    bsz, n, d = x.shape
    tok = lambda w, col=0: pl.BlockSpec((1, TM, w), lambda b, i: (b, i, col))
    ws = [lw["w_rwkv_o"], lw["w_pool_o"], lw["w_nat_o"], lw["w_out"]]
    return pl.pallas_call(
        _merge_kernel,
        grid=(bsz, n // TM),
        in_specs=[tok(a.shape[2]), tok(bp.shape[2]), tok(cn.shape[2]),
                  tok(d, gate_col), tok(d, gate_col + 1), tok(d, gate_col + 2), tok(d),
                  pl.BlockSpec((1, 1, d), _mod_spec(nct, 2))] + [_const_spec(w.shape) for w in ws],
        out_specs=tok(d),
        out_shape=jax.ShapeDtypeStruct((bsz, n, d), F32),
        compiler_params=_params("parallel", "parallel"),
        name="merge",
    )(a, bp, cn, p, p, p, x, mod, *ws)


def _ffn_kernel(x_ref, sh, sc, g2, gain, w1, w2, o_ref):
    x = x_ref[0]
    h = _norm_mod(x, gain[...], sh[0], sc[0])
    u = jnp.dot(h.astype(BF16), w1[...], preferred_element_type=F32)
    hid = w2.shape[0]
    gate, up = u[:, :hid], u[:, hid:]
    act = gate * jax.nn.sigmoid(gate) * up
    o_ref[0] = x + g2[0] * jnp.dot(act.astype(BF16), w2[...], preferred_element_type=F32)


def _ffn(x, mod, lw, nct):
    bsz, n, d = x.shape
    tok = pl.BlockSpec((1, TM, d), lambda b, i: (b, i, 0))
    mspec = lambda k: pl.BlockSpec((1, 1, d), _mod_spec(nct, k))
    return pl.pallas_call(
        _ffn_kernel,
        grid=(bsz, n // TM),
        in_specs=[tok, mspec(3), mspec(4), mspec(5), _const_spec(lw["norm2"].shape),
                  _const_spec(lw["w_ffn_in"].shape), _const_spec(lw["w_ffn_out"].shape)],
        out_specs=tok,
        out_shape=jax.ShapeDtypeStruct((bsz, n, d), F32),
        compiler_params=_params("parallel", "parallel"),
        name="ffn",
    )(x, mod, mod, mod, lw["norm2"], lw["w_ffn_in"], lw["w_ffn_out"])


def _block_diag(blocks):
    n = len(blocks)
    rows = []
    for i, blk in enumerate(blocks):
        rows.append(jnp.concatenate([blk if j == i else jnp.zeros((blk.shape[0], blocks[j].shape[1]), blk.dtype)
                                     for j in range(n)], axis=1))
    return jnp.concatenate(rows, axis=0)


def _pad_cols(a, width):
    return jnp.pad(a, ((0, 0), (0, width - a.shape[1])))


def _layer_weights(l, prm):
    c = prm["k_k"].shape[1]
    lora = prm["decay_up"].shape[2]
    gl = prm["gate_up"].shape[1]
    pool = prm["pool_scale"].shape[1]
    d = prm["w_out"].shape[1]
    w_in = prm["w_in"][l]
    o_lora, o_gd, o_pool = 3 * c, 3 * c + 4 * lora, 3 * c + 4 * lora + gl
    o_q = o_pool + pool
    o_gate = o_q + 3 * c
    assert 4 * lora == 2 * LANES and gl <= 2 * LANES and pool == 2 * LANES and o_gate + 3 * d == w_in.shape[1]
    w_perm = jnp.concatenate([
        w_in[:, 0:o_lora], w_in[:, o_q:o_gate], w_in[:, o_lora:o_gd],
        _pad_cols(w_in[:, o_gd:o_pool], 2 * LANES), w_in[:, o_pool:o_q], w_in[:, o_gate:]], axis=1).astype(BF16)
    mu = lambda m: _pad_cols(m[l][None, :o_pool], o_pool + 2 * LANES - gl)
    heads = c // HEAD
    row = lambda a: a.reshape(1, -1)
    return {
        "w_in": w_perm,
        "norm1": row(prm["norm1"][l]), "norm2": row(prm["norm2"][l]),
        "mu_prev": mu(prm["mu_prev"]), "mu_next": mu(prm["mu_next"]),
        "decay_up": _block_diag([prm["decay_up"][l, 0], prm["decay_up"][l, 1]]),
        "decay_w0": row(prm["decay_w0"][l]),
        "iclr_up": _block_diag([prm["iclr_up"][l, 0], prm["iclr_up"][l, 1]]),
        "iclr_a0": row(prm["iclr_a0"][l]),
        "k_k": row(prm["k_k"][l]), "k_a": row(prm["k_a"][l]), "r_k": row(prm["r_k"][l]),
        "gn_w": row(prm["gn_w"][l]), "gn_b": row(prm["gn_b"][l]),
        "gate_up": jnp.pad(prm["gate_up"][l], ((0, 2 * LANES - gl), (0, 0))),
        "ones_bd": jnp.kron(jnp.eye(heads, dtype=F32), jnp.ones((HEAD, HEAD), F32)),
        "pool_w": _block_diag([prm["pool_w"][l, g] for g in range(len(POOL_WINDOWS))]),
        "pool_scale": row(prm["pool_scale"][l]),
        "q_gain": row(jnp.tile(prm["q_gain"][l], heads)), "k_gain": row(jnp.tile(prm["k_gain"][l], heads)),
        "nat_bias": _nat_bias_table(prm["rpb"][l]),
        "w_rwkv_o": prm["w_rwkv_o"][l].astype(BF16), "w_pool_o": prm["w_pool_o"][l].astype(BF16),
        "w_nat_o": prm["w_nat_o"][l].astype(BF16), "w_out": prm["w_out"][l].astype(BF16),
        "w_ffn_in": prm["w_ffn_in"][l].astype(BF16), "w_ffn_out": prm["w_ffn_out"][l].astype(BF16),
    }


def kernel(x, c, ctx, c_ctx, w_mod, b_mod, norm1, norm2, w_in, mu_prev, mu_next, decay_w0, decay_up, iclr_a0, iclr_up, gate_up, k_k, k_a, r_k, gn_w, gn_b, pool_w, pool_scale, q_gain, k_gain, rpb, w_rwkv_o, w_pool_o, w_nat_o, w_out, w_ffn_in, w_ffn_out):
    prm = dict(norm1=norm1, norm2=norm2, w_in=w_in, mu_prev=mu_prev, mu_next=mu_next, decay_w0=decay_w0,
               decay_up=decay_up, iclr_a0=iclr_a0, iclr_up=iclr_up, gate_up=gate_up, k_k=k_k, k_a=k_a, r_k=r_k,
               gn_w=gn_w, gn_b=gn_b, pool_w=pool_w, pool_scale=pool_scale, q_gain=q_gain, k_gain=k_gain, rpb=rpb,
               w_rwkv_o=w_rwkv_o, w_pool_o=w_pool_o, w_nat_o=w_nat_o, w_out=w_out, w_ffn_in=w_ffn_in,
               w_ffn_out=w_ffn_out)
    bsz, len_lat, d = x.shape
    len_ctx = ctx.shape[1]
    depth = w_mod.shape[0]
    assert len_ctx % TM == 0 and len_lat % TM == 0 and bsz + 1 <= SUBLANES
    nct = len_ctx // TM
    cdim = k_k.shape[1]

    s_rows = jnp.concatenate([c, c_ctx[None, :], jnp.zeros((SUBLANES - bsz - 1, d), F32)], axis=0)
    mod_all = _modulation(s_rows, w_mod, b_mod)
    xa = jnp.concatenate([ctx, x], axis=1)

    for l in range(depth):
        lw = _layer_weights(l, prm)
        m_lat = mod_all[l, :bsz]
        m_ctx = jnp.broadcast_to(mod_all[l, bsz][None], m_lat.shape)
        mod = jnp.stack([m_ctx, m_lat], axis=1).reshape(bsz * 2 * 6, 1, d)

        p = _in_proj(xa, mod, lw["norm1"], lw["w_in"], nct)
        w_f, w_b, kd_f, kd_b, b_f, b_b, v, kk, r, gd = _rwkv_prep(p, lw, nct)
        y_f, y_b = _rwkv_scan(w_f, w_b, kd_f, kd_b, b_f, b_b, v, kk, r, len_ctx // CHUNK)
        a_br = _rwkv_readout(y_f, y_b, r, kd_f, kd_b, v, gd, lw)
        b_br = _pool(p, lw, nct, len_ctx, len_lat, (6 * cdim + 4 * LANES) // (2 * LANES))
        qn, kn, vn = _qk_norm(p, lw, 3)
        c_br = jnp.concatenate([_ctx_attn(qn, kn, vn, len_ctx),
                                _nat_latent(qn, kn, vn, lw["nat_bias"], len_ctx, len_lat)], axis=1)
        xa = _merge(a_br, b_br, c_br, p, xa, mod, lw, nct, (6 * cdim + 6 * LANES) // d)
        xa = _ffn(xa, mod, lw, nct)
    return xa[:, len_ctx:]
```

```python
import functools

import numpy as np
import jax
import jax.numpy as jnp
from jax import lax
from jax.experimental import pallas as pl
from jax.experimental.pallas import tpu as pltpu

F32 = jnp.float32
BF16 = jnp.bfloat16

HEAD = 64
NORM_EPS = 1e-6
GN_EPS = 64e-5
KK_EPS = 1e-24
POOL_WINDOWS = (2, 4, 8, 16)
POOL_GROUP = 64
GRID_W = 64
WIN_H = 8
WIN_W = 16
MASK_BIAS = -1e30

LANES = 128
SUBLANES = 8
VMEM_LIMIT = 56 * 1024 * 1024

TM = 256
CHUNK = 64
TM_IN = 640
TN_IN = 2048
TN_MOD = 1536


def _dot_hi(a, b):
    return jnp.dot(a, b, precision=lax.Precision.HIGHEST, preferred_element_type=F32)


def _dot_bf(a, b):
    return jnp.dot(a.astype(BF16), b.astype(BF16), preferred_element_type=F32)


def _dot_nt(a, b):
    return lax.dot_general(a, b, (((1,), (1,)), ((), ())), preferred_element_type=F32)


def _params(*sem):
    return pltpu.CompilerParams(dimension_semantics=sem, vmem_limit_bytes=VMEM_LIMIT)


def _const_spec(shape):
    nd = len(shape)
    return pl.BlockSpec(shape, lambda *_: (0,) * nd, pipeline_mode=pl.Buffered(1))


def _mod_kernel(s_ref, w_ref, b_ref, o_ref):
    s = s_ref[...]
    s = s * jax.nn.sigmoid(s)
    o_ref[0] = _dot_hi(s, w_ref[0]) + b_ref[0]


def _modulation(s_rows, w_mod, b_mod):
    depth, d, n = w_mod.shape
    return pl.pallas_call(
        _mod_kernel,
        grid=(depth, n // TN_MOD),
        in_specs=[
            pl.BlockSpec((SUBLANES, d), lambda l, j: (0, 0)),
            pl.BlockSpec((1, d, TN_MOD), lambda l, j: (l, 0, j)),
            pl.BlockSpec((1, 1, TN_MOD), lambda l, j: (l, 0, j)),
        ],
        out_specs=pl.BlockSpec((1, SUBLANES, TN_MOD), lambda l, j: (l, 0, j)),
        out_shape=jax.ShapeDtypeStruct((depth, SUBLANES, n), F32),
        compiler_params=_params("parallel", "parallel"),
        name="modulation",
    )(s_rows, w_mod, b_mod.reshape(depth, 1, n))


def _mod_spec(nct, k):
    return lambda b, i: ((b * 2 + (i >= nct).astype(jnp.int32)) * 6 + k, 0, 0)


def _norm_mod(x, gain, shift, scale):
    ms = jnp.mean(x * x, axis=-1, keepdims=True)
    return x * lax.rsqrt(ms + NORM_EPS) * gain * (1.0 + scale) + shift


def _in_proj_kernel(len_ctx, x_ref, sh_c, sc_c, sh_l, sc_l, g_ref, w_ref, o_ref):
    tm = x_ref.shape[1]
    row = pl.program_id(2) * tm + lax.broadcasted_iota(jnp.int32, (tm, 1), 0)
    is_ctx = row < len_ctx
    shift = jnp.where(is_ctx, sh_c[0], sh_l[0])
    scale = jnp.where(is_ctx, sc_c[0], sc_l[0])
    h = _norm_mod(x_ref[0], g_ref[...], shift, scale)
    o_ref[0] = jnp.dot(h.astype(BF16), w_ref[...], preferred_element_type=F32)


def _in_proj(x, mod, gain, w, len_ctx):
    bsz, n, d = x.shape
    nout = w.shape[1]
    assert n % TM_IN == 0
    mspec = lambda is_lat, k: pl.BlockSpec((1, 1, d), lambda j, b, i: ((b * 2 + is_lat) * 6 + k, 0, 0))
    return pl.pallas_call(
        functools.partial(_in_proj_kernel, len_ctx),
        grid=(nout // TN_IN, bsz, n // TM_IN),
        in_specs=[
            pl.BlockSpec((1, TM_IN, d), lambda j, b, i: (b, i, 0)),
            mspec(0, 0), mspec(0, 1), mspec(1, 0), mspec(1, 1),
            pl.BlockSpec((1, d), lambda j, b, i: (0, 0)),
            pl.BlockSpec((d, TN_IN), lambda j, b, i: (0, j)),
        ],
        out_specs=pl.BlockSpec((1, TM_IN, TN_IN), lambda j, b, i: (b, i, j)),
        out_shape=jax.ShapeDtypeStruct((bsz, n, nout), F32),
        compiler_params=_params("parallel", "parallel", "parallel"),
        name="in_proj",
    )(x, mod, mod, mod, mod, gain, w)


def _halo_specs(width, col, n):
    nb = n // SUBLANES
    per = TM // SUBLANES
    return [
        pl.BlockSpec((1, TM, width), lambda b, i: (b, i, col)),
        pl.BlockSpec((1, SUBLANES, width), lambda b, i: (b, jnp.maximum(i * per - 1, 0), col)),
        pl.BlockSpec((1, SUBLANES, width), lambda b, i: (b, jnp.minimum((i + 1) * per, nb - 1), col)),
    ]


def _prep_kernel(nct, nt, rkv_ref, rkv_p, rkv_n, lo_ref, lo_p, lo_n, gd_ref, gd_p, gd_n,
                 mup_ref, mun_ref, du_ref, w0_ref, au_ref, a0_ref, kkw_ref, ka_ref, ones_ref,
                 wf_ref, wb_ref, kdf_ref, kdb_ref, bf_ref, bb_ref, v_ref, kk_ref, r_ref, gdo_ref):
    i = pl.program_id(1)
    first = jnp.logical_or(i == 0, i == nct)
    last = jnp.logical_or(i == nct - 1, i == nt - 1)
    row = lax.broadcasted_iota(jnp.int32, (TM, 1), 0)
    c = rkv_ref.shape[2] // 3

    def mix(main, prev8, next8, lo, hi):
        pm = main[0]
        prow = jnp.where(first, 0.0, prev8[0, SUBLANES - 1:SUBLANES, :])
        nrow = jnp.where(last, 0.0, next8[0, 0:1, :])
        prev = jnp.where(row == 0, prow, pltpu.roll(pm, 1, 0))
        nxt = jnp.where(row == TM - 1, nrow, pltpu.roll(pm, TM - 1, 0))
        return pm + mup_ref[:, lo:hi] * (prev - pm) + mun_ref[:, lo:hi] * (nxt - pm)

    rkv = mix(rkv_ref, rkv_p, rkv_n, 0, 3 * c)
    lora = mix(lo_ref, lo_p, lo_n, 3 * c, 3 * c + 2 * LANES)
    gdo_ref[0] = mix(gd_ref, gd_p, gd_n, 3 * c + 2 * LANES, 3 * c + 4 * LANES)

    r, k, v = rkv[:, 0:c], rkv[:, c:2 * c], rkv[:, 2 * c:3 * c]
    wd, ad = lora[:, 0:LANES], lora[:, LANES:2 * LANES]
    dec = _mm(jnp.tanh(wd), du_ref[...], split_a=True, split_b=True) + w0_ref[...]
    softplus_neg = jnp.maximum(-dec, 0.0) + jnp.log1p(jnp.exp(-jnp.abs(dec)))
    log_decay = -jnp.exp(-softplus_neg - 0.5)
    a = jax.nn.sigmoid(_mm(ad, au_ref[...], split_a=True, split_b=True) + a0_ref[...])
    kk = k * kkw_ref[...]
    kk = kk * lax.rsqrt(jnp.maximum(_mm(kk * kk, ones_ref[...], split_a=True), KK_EPS))
    ka = ka_ref[...]
    for z, (w_o, kd_o, b_o) in enumerate(((wf_ref, kdf_ref, bf_ref), (wb_ref, kdb_ref, bb_ref))):
        az = a[:, z * c:(z + 1) * c]
        w_o[0] = log_decay[:, z * c:(z + 1) * c]
        kd_o[0] = k * (1.0 + (az - 1.0) * ka)
        b_o[0] = kk * az
    v_ref[0] = v
    kk_ref[0] = kk
    r_ref[0] = r


def _rwkv_prep(p, lw, nct):
    bsz, n, _ = p.shape
    c = lw["k_k"].shape[1]
    nt = n // TM
    tok = jax.ShapeDtypeStruct((bsz, n, c), F32)
    vec = lambda a: _const_spec(a.shape)
    consts = [lw["mu_prev"], lw["mu_next"], lw["decay_up"], lw["decay_w0"], lw["iclr_up"], lw["iclr_a0"],
              lw["k_k"], lw["k_a"], lw["ones_bd"]]
    return pl.pallas_call(
        functools.partial(_prep_kernel, nct, nt),
        grid=(bsz, nt),
        in_specs=_halo_specs(3 * c, 0, n) + _halo_specs(2 * LANES, (6 * c) // (2 * LANES), n)
        + _halo_specs(2 * LANES, (6 * c) // (2 * LANES) + 1, n) + [vec(a) for a in consts],
        out_specs=[pl.BlockSpec((1, TM, c), lambda b, i: (b, i, 0))] * 9
        + [pl.BlockSpec((1, TM, 2 * LANES), lambda b, i: (b, i, 0))],
        out_shape=[tok] * 9 + [jax.ShapeDtypeStruct((bsz, n, 2 * LANES), F32)],
        compiler_params=_params("parallel", "parallel"),
        name="rwkv_prep",
    )(p, p, p, p, p, p, p, p, p, *consts)


def _split(a):
    hi = a.astype(BF16)
    return hi, (a - hi.astype(F32)).astype(BF16)


def _mm(a, b, split_a=False, split_b=False):
    def halves(x, split):
        if x.dtype == BF16 or not split:
            return x.astype(BF16), None
        return _split(x)

    (ah, al), (bh, bl) = halves(a, split_a), halves(b, split_b)
    lhs, rhs = [ah], [bh]
    if al is not None:
        lhs.append(al)
        rhs.append(bh)
    if bl is not None:
        lhs.append(ah)
        rhs.append(bl)
    if len(lhs) == 1:
        return jnp.dot(ah, bh, preferred_element_type=F32)
    return jnp.dot(jnp.concatenate(lhs, axis=1), jnp.concatenate(rhs, axis=0), preferred_element_type=F32)


def _chunk_scan_kernel(ldf, kdf, bf, vf, kkf, rf, ldb, kdb, bb, vb, kkb, rb, yf_ref, yb_ref, h_ref):
    @pl.when(pl.program_id(0) == 0)
    def _():
        h_ref[...] = jnp.zeros_like(h_ref)

    bsz, cs, c = ldf.shape
    t_i = lax.broadcasted_iota(jnp.int32, (cs, LANES), 0)
    lane = lax.broadcasted_iota(jnp.int32, (cs, LANES), 1)
    s_i = jnp.bitwise_and(lane, HEAD - 1)
    m_a = (lane < HEAD).astype(F32).astype(BF16)
    m_b = (lane >= HEAD).astype(F32).astype(BF16)
    r2 = lax.broadcasted_iota(jnp.int32, (LANES, LANES), 0)
    l2 = lax.broadcasted_iota(jnp.int32, (LANES, LANES), 1)
    block = ((r2 >= HEAD) == (l2 >= HEAD)).astype(F32)
    eye = (r2 == l2).astype(F32)
    zeros = jnp.zeros((cs, LANES), BF16)

    def bd(x):
        k = x.shape[1] // LANES
        return jnp.concatenate([x * jnp.concatenate([m_a] * k, axis=1), x * jnp.concatenate([m_b] * k, axis=1)],
                               axis=0)

    def unit(d, ld, kd, b_, v, kk, r, h):
        before = (s_i < t_i) if d == 0 else (s_i > t_i)
        upto = (s_i <= t_i) if d == 0 else (s_i >= t_i)
        tri = upto.astype(F32).astype(BF16)
        lh, ll = _split(ld)
        big_l = jnp.dot(tri, jnp.concatenate([lh, ll], axis=0), preferred_element_type=F32)
        yield
        ltot = big_l[cs - 1:cs] if d == 0 else big_l[0:1]
        kap = kk * jnp.exp(big_l - ld)
        rt = r * jnp.exp(big_l)
        einv = jnp.exp(-big_l)
        kt, bt = kd * einv, b_ * einv
        efin = jnp.exp(ltot - big_l)
        khat, bhat = kd * efin, b_ * efin
        ktb, btb, vb16 = kt.astype(BF16), bt.astype(BF16), v.astype(BF16)
        sc = _dot_nt(jnp.concatenate([kap, rt], axis=0).astype(BF16),
                     jnp.concatenate([ktb * m_a, ktb * m_b, btb * m_a, btb * m_b], axis=0))
        yield
        a_k = jnp.where(before, sc[:cs, :LANES], 0.0)
        n_p = jnp.where(before, -sc[:cs, LANES:], 0.0)
        m_k = jnp.where(upto, sc[cs:, :LANES], 0.0)
        m_nb = jnp.where(upto, -sc[cs:, LANES:], 0.0)
        x = jnp.concatenate([kap, _mm(a_k, bd(vb16))], axis=1)
        yield
        for level in range(6):
            xh, xl = _split(x)
            nb = n_p.astype(BF16)
            x = x + jnp.dot(jnp.concatenate([nb, nb], axis=1), jnp.concatenate([bd(xh), bd(xl)], axis=0),
                            preferred_element_type=F32)
            if level < 5:
                n_p = jnp.dot(nb, bd(nb), preferred_element_type=F32)
            yield
        vz = jnp.concatenate([vb16, zeros], axis=1)
        uw = jnp.concatenate([x[:, LANES:], x[:, :LANES]], axis=1).astype(BF16)
        o1 = _mm(jnp.concatenate([m_k, m_nb], axis=1), jnp.concatenate([bd(vz), bd(uw)], axis=0))
        o2 = _mm(jnp.concatenate([khat, -bhat], axis=0).T, jnp.concatenate([vz, uw], axis=0))
        yield
        y0, q = o1[:, :LANES], rt + o1[:, LANES:]
        psi = o2[:, :LANES] * block
        phi = o2[:, LANES:] * block + eye * jnp.exp(ltot)
        y = y0 + _mm(q, h)
        return y, _mm(phi, h, split_a=True, split_b=True) + psi

    dirs = ((ldf, kdf, bf, vf, kkf, rf, yf_ref), (ldb, kdb, bb, vb, kkb, rb, yb_ref))
    keys = [(d, b, p) for d in range(2) for b in range(bsz) for p in range(c // LANES)]
    lanes_of = lambda p: slice(p * LANES, (p + 1) * LANES)
    units = {k: unit(k[0], *[ref[k[1], :, lanes_of(k[2])] for ref in dirs[k[0]][:6]], h_ref[k[1], k[0], k[2]])
             for k in keys}
    for (d, b, p), (y, h_new) in _run_interleaved(units).items():
        dirs[d][6][b, :, lanes_of(p)] = y
        h_ref[b, d, p] = h_new


def _rwkv_scan(ld_f, ld_b, kd_f, kd_b, b_f, b_b, v, kk, r, nct_c):
    bsz, n, c = v.shape
    nc = n // CHUNK

    def rev(g):
        return jnp.where(g < nct_c, nct_c - 1 - g, nc - 1 - g + nct_c)

    fwd = pl.BlockSpec((bsz, CHUNK, c), lambda g: (0, g, 0))
    bwd = pl.BlockSpec((bsz, CHUNK, c), lambda g: (0, rev(g), 0))
    tok = jax.ShapeDtypeStruct((bsz, n, c), F32)
    return pl.pallas_call(
        _chunk_scan_kernel,
        grid=(nc,),
        in_specs=[fwd] * 6 + [bwd] * 6,
        out_specs=[fwd, bwd],
        out_shape=[tok, tok],
        scratch_shapes=[pltpu.VMEM((bsz, 2, c // LANES, LANES, LANES), F32)],
        compiler_params=_params("arbitrary"),
        name="rwkv_scan",
    )(ld_f, kd_f, b_f, v, kk, r, ld_b, kd_b, b_b, v, kk, r)


def _readout_kernel(yf, yb, r, kdf, kdb, v, gd, gnw, gnb, rk, gup, ones_ref, a_ref):
    ones = ones_ref[...]
    inv = 1.0 / HEAD
    y = yf[0] + yb[0]
    yc = y - _mm(y, ones, split_a=True) * inv
    var = _mm(yc * yc, ones, split_a=True) * inv
    yn = yc * lax.rsqrt(var + GN_EPS) * gnw[...] + gnb[...]
    bonus = _mm(r[0] * rk[...] * (kdf[0] + kdb[0]), ones, split_a=True) * v[0]
    g = _mm(jax.nn.sigmoid(gd[0]), gup[...])
    a_ref[0] = (yn + bonus) * g


def _rwkv_readout(y_f, y_b, r, kd_f, kd_b, v, gd, lw):
    bsz, n, c = v.shape
    tok = pl.BlockSpec((1, TM, c), lambda b, i: (b, i, 0))
    consts = [lw["gn_w"], lw["gn_b"], lw["r_k"], lw["gate_up"], lw["ones_bd"]]
    return pl.pallas_call(
        _readout_kernel,
        grid=(bsz, n // TM),
        in_specs=[tok] * 6 + [pl.BlockSpec((1, TM, 2 * LANES), lambda b, i: (b, i, 0))]
        + [_const_spec(a.shape) for a in consts],
        out_specs=tok,
        out_shape=jax.ShapeDtypeStruct((bsz, n, c), F32),
        compiler_params=_params("parallel", "parallel"),
        name="rwkv_readout",
    )(y_f, y_b, r, kd_f, kd_b, v, gd, *consts)


def _pool_kernel(nct, len_ctx, len_lat, main, prev8, next8, pw_ref, scale_ref, o_ref):
    i = pl.program_id(1)
    is_lat = i >= nct
    seq_len = jnp.where(is_lat, len_lat, len_ctx)
    t0 = jnp.where(is_lat, i - nct, i) * TM
    n = TM + 2 * SUBLANES
    pm = main[0]
    ext = jnp.concatenate([prev8[0], pm, next8[0]], axis=0)
    pos = t0 - SUBLANES + lax.broadcasted_iota(jnp.int32, (n, 1), 0)
    e = jnp.where(jnp.logical_and(pos >= 0, pos < seq_len), ext, 0.0)
    a2 = e + pltpu.roll(e, 1, 0)
    a4 = pltpu.roll(a2, 1, 0) + pltpu.roll(a2, n - 1, 0)
    a8 = pltpu.roll(a4, 2, 0) + pltpu.roll(a4, n - 2, 0)
    a16 = pltpu.roll(a8, 4, 0) + pltpu.roll(a8, n - 4, 0)
    t = t0 + lax.broadcasted_iota(jnp.int32, (TM, 1), 0)
    lane = lax.broadcasted_iota(jnp.int32, pm.shape, 1)
    mean = None
    for g, (w, acc) in reversed(list(enumerate(zip(POOL_WINDOWS, (a2, a4, a8, a16))))):
        lo = jnp.maximum(t - w // 2, 0)
        hi = jnp.minimum(t + (w - w // 2) - 1, seq_len - 1)
        m = acc[SUBLANES:SUBLANES + TM] / (hi - lo + 1).astype(F32)
        mean = m if mean is None else jnp.where(lane < (g + 1) * POOL_GROUP, m, mean)
    o_ref[0] = _mm(mean - pm, pw_ref[...], split_a=True, split_b=True) * scale_ref[...]


def _pool(p, lw, nct, len_ctx, len_lat, col):
    bsz, n, _ = p.shape
    width = lw["pool_w"].shape[0]
    return pl.pallas_call(
        functools.partial(_pool_kernel, nct, len_ctx, len_lat),
        grid=(bsz, n // TM),
        in_specs=_halo_specs(width, col, n) + [_const_spec(lw["pool_w"].shape), _const_spec(lw["pool_scale"].shape)],
        out_specs=pl.BlockSpec((1, TM, width), lambda b, i: (b, i, 0)),
        out_shape=jax.ShapeDtypeStruct((bsz, n, width), F32),
        compiler_params=_params("parallel", "parallel"),
        name="pool",
    )(p, p, p, lw["pool_w"], lw["pool_scale"])


def _qk_norm_kernel(q_ref, k_ref, v_ref, qg_ref, kg_ref, ones_ref, qo, ko, vo):
    ones = ones_ref[...]
    inv = 1.0 / HEAD
    q, k = q_ref[0], k_ref[0]
    qn = q * lax.rsqrt(_mm(q * q, ones, split_a=True) * inv + NORM_EPS) * qg_ref[...]
    kn = k * lax.rsqrt(_mm(k * k, ones, split_a=True) * inv + NORM_EPS) * kg_ref[...]
    for o_ref, val in ((qo, qn * HEAD ** -0.5), (ko, kn), (vo, v_ref[0])):
        val = val.astype(BF16)
        for h in range(o_ref.shape[1]):
            o_ref[0, h] = val[:, h * HEAD:(h + 1) * HEAD]


def _qk_norm(p, lw, col):
    bsz, n, _ = p.shape
    c = lw["q_gain"].shape[1]
    tok = jax.ShapeDtypeStruct((bsz, c // HEAD, n, HEAD), BF16)
    ospec = pl.BlockSpec((1, c // HEAD, TM, HEAD), lambda b, i: (b, 0, i, 0))
    consts = [lw["q_gain"], lw["k_gain"], lw["ones_bd"]]
    return pl.pallas_call(
        _qk_norm_kernel,
        grid=(bsz, n // TM),
        in_specs=[pl.BlockSpec((1, TM, c), lambda b, i, j=j: (b, i, col + j)) for j in range(3)]
        + [_const_spec(a.shape) for a in consts],
        out_specs=[ospec] * 3,
        out_shape=[tok] * 3,
        compiler_params=_params("parallel", "parallel"),
        name="qk_norm",
    )(p, p, p, *consts)


def _run_interleaved(units):
    done = {}
    while units:
        for k in list(units):
            try:
                next(units[k])
            except StopIteration as stop:
                done[k] = stop.value
                del units[k]
    return done


def _attend(q, key_sets, bias):
    scores = [_dot_nt(q, k) for k, _ in key_sets]
    yield
    if bias is not None:
        scores[0] = scores[0] + bias
    m = functools.reduce(jnp.maximum, [jnp.max(s, axis=-1, keepdims=True) for s in scores])
    yield
    ps = [jnp.exp(s - m) for s in scores]
    den = functools.reduce(jnp.add, [jnp.sum(p, axis=-1, keepdims=True) for p in ps])
    num = functools.reduce(jnp.add, [jnp.dot(p.astype(BF16), v, preferred_element_type=F32)
                                      for p, (_, v) in zip(ps, key_sets)])
    yield
    return num / den


def _nat_kernel(rows, q_ref, kp, kc, kn, vp, vc, vn, kx_ref, vx_ref, bias_ref, o_ref, ks, vs):
    i = pl.program_id(1) - 1
    heads, tq = q_ref.shape[1], q_ref.shape[2]
    rb = tq // GRID_W
    nloc = WIN_H * GRID_W

    @pl.when(i < 0)
    def _():
        units = {h: _attend(q_ref[0, h], [(kx_ref[0, h], vx_ref[0, h])], None) for h in range(heads)}
        for h, o in _run_interleaved(units).items():
            o_ref[0, :, h * HEAD:(h + 1) * HEAD] = o

    @pl.when(i >= 0)
    def _():
        for j, (kr, vr) in enumerate(((kp, vp), (kc, vc), (kn, vn))):
            ks[:, j * tq:(j + 1) * tq, :] = kr[0]
            vs[:, j * tq:(j + 1) * tq, :] = vr[0]
        for rr in range(rb):
            r = i * rb + rr
            rs = jnp.clip(r - WIN_H // 2, 0, rows - WIN_H)
            off = r - rs
            start = pl.multiple_of((rs - i * rb + rb) * GRID_W, GRID_W)
            qs = slice(rr * GRID_W, (rr + 1) * GRID_W)
            units = {h: _attend(q_ref[0, h, qs], [(ks[h, pl.ds(start, nloc)], vs[h, pl.ds(start, nloc)]),
                                                  (kx_ref[0, h], vx_ref[0, h])], bias_ref[h, off])
                     for h in range(heads)}
            for h, o in _run_interleaved(units).items():
                o_ref[0, qs, h * HEAD:(h + 1) * HEAD] = o


def _nat_attention(qn, kn, vn, bias, len_ctx, len_lat):
    bsz, heads, n, hd = qn.shape
    tq = len_ctx
    assert tq == (WIN_H // 2) * GRID_W and len_lat % tq == 0
    rows = len_lat // GRID_W
    nblk = len_lat // tq

    def blk(shift):
        return pl.BlockSpec((1, heads, tq, hd),
                            lambda b, i: (b, 0, jnp.where(i == 0, 0, 1 + jnp.clip(i - 1 + shift, 0, nblk - 1)), 0))

    ctx = pl.BlockSpec((1, heads, tq, hd), lambda b, i: (b, 0, 0, 0))
    return pl.pallas_call(
        functools.partial(_nat_kernel, rows),
        grid=(bsz, 1 + nblk),
        in_specs=[blk(0), blk(-1), blk(0), blk(1), blk(-1), blk(0), blk(1), ctx, ctx, _const_spec(bias.shape)],
        out_specs=pl.BlockSpec((1, tq, heads * hd), lambda b, i: (b, i, 0)),
        out_shape=jax.ShapeDtypeStruct((bsz, n, heads * hd), F32),
        scratch_shapes=[pltpu.VMEM((heads, 3 * tq, hd), BF16), pltpu.VMEM((heads, 3 * tq, hd), BF16)],
        compiler_params=_params("parallel", "parallel"),
        name="nat_attention",
    )(qn, kn, kn, kn, vn, vn, vn, kn, vn, bias)


def _nat_bias_table(rpb):
    qc = np.arange(GRID_W)[:, None]
    kc = np.arange(GRID_W)[None, :]
    cs = np.clip(qc - WIN_W // 2, 0, GRID_W - WIN_W)
    valid = (kc >= cs) & (kc < cs + WIN_W)
    dc = kc - qc + WIN_W - 1
    pick = (np.arange(2 * WIN_W - 1)[:, None, None] == dc[None]) & valid[None]
    cols = jnp.einsum("hdm,mqk->hdqk", rpb, jnp.asarray(pick, F32), precision=lax.Precision.HIGHEST)
    cols = jnp.where(valid[None, None], cols, MASK_BIAS)
    t = jnp.stack([cols[:, WIN_H - 1 - off:2 * WIN_H - 1 - off] for off in range(WIN_H)], axis=1)
    return t.transpose(0, 1, 3, 2, 4).reshape(rpb.shape[0], WIN_H, GRID_W, WIN_H * GRID_W)


def _merge_kernel(a, bp, cn, ga, gb, gc, x, g1, wa, wb, wc, wo, o_ref):
    m = (jax.nn.sigmoid(ga[0]) * _dot_bf(a[0], wa[...])
         + jax.nn.sigmoid(gb[0]) * _dot_bf(bp[0], wb[...])
         + jax.nn.sigmoid(gc[0]) * _dot_bf(cn[0], wc[...]))
    o_ref[0] = x[0] + g1[0] * _dot_bf(m, wo[...])


def _merge(a, bp, cn, p, x, mod, lw, nct, gate_col):
    bsz, n, d = x.shape
    tok = lambda w, col=0: pl.BlockSpec((1, TM, w), lambda b, i: (b, i, col))
    ws = [lw["w_rwkv_o"], lw["w_pool_o"], lw["w_nat_o"], lw["w_out"]]
    return pl.pallas_call(
        _merge_kernel,
        grid=(bsz, n // TM),
        in_specs=[tok(a.shape[2]), tok(bp.shape[2]), tok(cn.shape[2]),
                  tok(d, gate_col), tok(d, gate_col + 1), tok(d, gate_col + 2), tok(d),
                  pl.BlockSpec((1, 1, d), _mod_spec(nct, 2))] + [_const_spec(w.shape) for w in ws],
        out_specs=tok(d),
        out_shape=jax.ShapeDtypeStruct((bsz, n, d), F32),
        compiler_params=_params("parallel", "parallel"),
        name="merge",
    )(a, bp, cn, p, p, p, x, mod, *ws)


def _ffn_kernel(x_ref, sh, sc, g2, gain, w1, w2, o_ref):
    x = x_ref[0]
    h = _norm_mod(x, gain[...], sh[0], sc[0])
    u = jnp.dot(h.astype(BF16), w1[...], preferred_element_type=F32)
    hid = w2.shape[0]
    gate, up = u[:, :hid], u[:, hid:]
    act = gate * jax.nn.sigmoid(gate) * up
    o_ref[0] = x + g2[0] * jnp.dot(act.astype(BF16), w2[...], preferred_element_type=F32)


def _ffn(x, mod, lw, nct):
    bsz, n, d = x.shape
    tok = pl.BlockSpec((1, TM, d), lambda b, i: (b, i, 0))
    mspec = lambda k: pl.BlockSpec((1, 1, d), _mod_spec(nct, k))
    return pl.pallas_call(
        _ffn_kernel,
        grid=(bsz, n // TM),
        in_specs=[tok, mspec(3), mspec(4), mspec(5), _const_spec(lw["norm2"].shape),
                  _const_spec(lw["w_ffn_in"].shape), _const_spec(lw["w_ffn_out"].shape)],
        out_specs=tok,
        out_shape=jax.ShapeDtypeStruct((bsz, n, d), F32),
        compiler_params=_params("parallel", "parallel"),
        name="ffn",
    )(x, mod, mod, mod, lw["norm2"], lw["w_ffn_in"], lw["w_ffn_out"])


def _block_diag(blocks):
    n = len(blocks)
    rows = []
    for i, blk in enumerate(blocks):
        rows.append(jnp.concatenate([blk if j == i else jnp.zeros((blk.shape[0], blocks[j].shape[1]), blk.dtype)
                                     for j in range(n)], axis=1))
    return jnp.concatenate(rows, axis=0)


def _pad_cols(a, width):
    return jnp.pad(a, ((0, 0), (0, width - a.shape[1])))


def _layer_weights(l, prm):
    c = prm["k_k"].shape[1]
    lora = prm["decay_up"].shape[2]
    gl = prm["gate_up"].shape[1]
    pool = prm["pool_scale"].shape[1]
    d = prm["w_out"].shape[1]
    w_in = prm["w_in"][l]
    o_lora, o_gd, o_pool = 3 * c, 3 * c + 4 * lora, 3 * c + 4 * lora + gl
    o_q = o_pool + pool
    o_gate = o_q + 3 * c
    assert 4 * lora == 2 * LANES and gl <= 2 * LANES and pool == 2 * LANES and o_gate + 3 * d == w_in.shape[1]
    w_perm = jnp.concatenate([
        w_in[:, 0:o_lora], w_in[:, o_q:o_gate], w_in[:, o_lora:o_gd],
        _pad_cols(w_in[:, o_gd:o_pool], 2 * LANES), w_in[:, o_pool:o_q], w_in[:, o_gate:]], axis=1).astype(BF16)
    mu = lambda m: _pad_cols(m[l][None, :o_pool], o_pool + 2 * LANES - gl)
    heads = c // HEAD
    row = lambda a: a.reshape(1, -1)
    return {
        "w_in": w_perm,
        "norm1": row(prm["norm1"][l]), "norm2": row(prm["norm2"][l]),
        "mu_prev": mu(prm["mu_prev"]), "mu_next": mu(prm["mu_next"]),
        "decay_up": _block_diag([prm["decay_up"][l, 0], prm["decay_up"][l, 1]]),
        "decay_w0": row(prm["decay_w0"][l]),
        "iclr_up": _block_diag([prm["iclr_up"][l, 0], prm["iclr_up"][l, 1]]),
        "iclr_a0": row(prm["iclr_a0"][l]),
        "k_k": row(prm["k_k"][l]), "k_a": row(prm["k_a"][l]), "r_k": row(prm["r_k"][l]),
        "gn_w": row(prm["gn_w"][l]), "gn_b": row(prm["gn_b"][l]),
        "gate_up": jnp.pad(prm["gate_up"][l], ((0, 2 * LANES - gl), (0, 0))),
        "ones_bd": jnp.kron(jnp.eye(heads, dtype=F32), jnp.ones((HEAD, HEAD), F32)).astype(BF16),
        "pool_w": _block_diag([prm["pool_w"][l, g] for g in range(len(POOL_WINDOWS))]),
        "pool_scale": row(prm["pool_scale"][l]),
        "q_gain": row(jnp.tile(prm["q_gain"][l], heads)), "k_gain": row(jnp.tile(prm["k_gain"][l], heads)),
        "nat_bias": _nat_bias_table(prm["rpb"][l]),
        "w_rwkv_o": prm["w_rwkv_o"][l].astype(BF16), "w_pool_o": prm["w_pool_o"][l].astype(BF16),
        "w_nat_o": prm["w_nat_o"][l].astype(BF16), "w_out": prm["w_out"][l].astype(BF16),
        "w_ffn_in": prm["w_ffn_in"][l].astype(BF16), "w_ffn_out": prm["w_ffn_out"][l].astype(BF16),
    }


def kernel(x, c, ctx, c_ctx, w_mod, b_mod, norm1, norm2, w_in, mu_prev, mu_next, decay_w0, decay_up, iclr_a0, iclr_up, gate_up, k_k, k_a, r_k, gn_w, gn_b, pool_w, pool_scale, q_gain, k_gain, rpb, w_rwkv_o, w_pool_o, w_nat_o, w_out, w_ffn_in, w_ffn_out):
    prm = dict(norm1=norm1, norm2=norm2, w_in=w_in, mu_prev=mu_prev, mu_next=mu_next, decay_w0=decay_w0,
               decay_up=decay_up, iclr_a0=iclr_a0, iclr_up=iclr_up, gate_up=gate_up, k_k=k_k, k_a=k_a, r_k=r_k,
               gn_w=gn_w, gn_b=gn_b, pool_w=pool_w, pool_scale=pool_scale, q_gain=q_gain, k_gain=k_gain, rpb=rpb,
               w_rwkv_o=w_rwkv_o, w_pool_o=w_pool_o, w_nat_o=w_nat_o, w_out=w_out, w_ffn_in=w_ffn_in,
               w_ffn_out=w_ffn_out)
    bsz, len_lat, d = x.shape
    len_ctx = ctx.shape[1]
    depth = w_mod.shape[0]
    assert len_ctx % TM == 0 and len_lat % TM == 0 and bsz + 1 <= SUBLANES
    nct = len_ctx // TM
    cdim = k_k.shape[1]

    s_rows = jnp.concatenate([c, c_ctx[None, :], jnp.zeros((SUBLANES - bsz - 1, d), F32)], axis=0)
    mod_all = _modulation(s_rows, w_mod, b_mod)
    xa = jnp.concatenate([ctx, x], axis=1)

    for l in range(depth):
        lw = _layer_weights(l, prm)
        m_lat = mod_all[l, :bsz]
        m_ctx = jnp.broadcast_to(mod_all[l, bsz][None], m_lat.shape)
        mod = jnp.stack([m_ctx, m_lat], axis=1).reshape(bsz * 2 * 6, 1, d)

        p = _in_proj(xa, mod, lw["norm1"], lw["w_in"], len_ctx)
        w_f, w_b, kd_f, kd_b, b_f, b_b, v, kk, r, gd = _rwkv_prep(p, lw, nct)
        y_f, y_b = _rwkv_scan(w_f, w_b, kd_f, kd_b, b_f, b_b, v, kk, r, len_ctx // CHUNK)
        a_br = _rwkv_readout(y_f, y_b, r, kd_f, kd_b, v, gd, lw)
        b_br = _pool(p, lw, nct, len_ctx, len_lat, (6 * cdim + 4 * LANES) // (2 * LANES))
        qn, kn, vn = _qk_norm(p, lw, 3)
        c_br = _nat_attention(qn, kn, vn, lw["nat_bias"], len_ctx, len_lat)
        xa = _merge(a_br, b_br, c_br, p, xa, mod, lw, nct, (6 * cdim + 6 * LANES) // d)
        xa = _ffn(xa, mod, lw, nct)
    return xa[:, len_ctx:]
```

```python
import functools

import numpy as np
import jax
import jax.numpy as jnp
from jax import lax
from jax.experimental import pallas as pl
from jax.experimental.pallas import tpu as pltpu

F32 = jnp.float32
BF16 = jnp.bfloat16

HEAD = 64
NORM_EPS = 1e-6
GN_EPS = 64e-5
KK_EPS = 1e-24
POOL_WINDOWS = (2, 4, 8, 16)
POOL_GROUP = 64
GRID_W = 64
WIN_H = 8
WIN_W = 16
MASK_BIAS = -1e30

LANES = 128
SUBLANES = 8
VMEM_LIMIT = 56 * 1024 * 1024

TM = 256
SCAN_BATCH = 12
CHUNK = 64
TM_IN = 640
TN_IN = 3072
TN_MOD = 1536


def _dot_hi(a, b):
    return jnp.dot(a, b, precision=lax.Precision.HIGHEST, preferred_element_type=F32)


def _dot_bf(a, b):
    return jnp.dot(a.astype(BF16), b.astype(BF16), preferred_element_type=F32)


def _dot_nt(a, b):
    return lax.dot_general(a, b, (((1,), (1,)), ((), ())), preferred_element_type=F32)


def _params(*sem):
    return pltpu.CompilerParams(dimension_semantics=sem, vmem_limit_bytes=VMEM_LIMIT)


def _const_spec(shape):
    nd = len(shape)
    return pl.BlockSpec(shape, lambda *_: (0,) * nd, pipeline_mode=pl.Buffered(1))


def _mod_kernel(s_ref, w_ref, b_ref, o_ref):
    s = s_ref[...]
    s = s * jax.nn.sigmoid(s)
    o_ref[0] = _dot_hi(s, w_ref[0]) + b_ref[0]


def _modulation(s_rows, w_mod, b_mod):
    depth, d, n = w_mod.shape
    return pl.pallas_call(
        _mod_kernel,
        grid=(depth, n // TN_MOD),
        in_specs=[
            pl.BlockSpec((SUBLANES, d), lambda l, j: (0, 0)),
            pl.BlockSpec((1, d, TN_MOD), lambda l, j: (l, 0, j)),
            pl.BlockSpec((1, 1, TN_MOD), lambda l, j: (l, 0, j)),
        ],
        out_specs=pl.BlockSpec((1, SUBLANES, TN_MOD), lambda l, j: (l, 0, j)),
        out_shape=jax.ShapeDtypeStruct((depth, SUBLANES, n), F32),
        compiler_params=_params("parallel", "parallel"),
        name="modulation",
    )(s_rows, w_mod, b_mod.reshape(depth, 1, n))


def _mod_spec(nct, k):
    return lambda b, i: ((b * 2 + (i >= nct).astype(jnp.int32)) * 6 + k, 0, 0)


def _norm_mod(x, gain, shift, scale):
    ms = jnp.mean(x * x, axis=-1, keepdims=True)
    return x * lax.rsqrt(ms + NORM_EPS) * gain * (1.0 + scale) + shift


def _in_proj_kernel(len_ctx, x_ref, sh_c, sc_c, sh_l, sc_l, g_ref, w_ref, o_ref):
    tm = x_ref.shape[1]
    row = pl.program_id(2) * tm + lax.broadcasted_iota(jnp.int32, (tm, 1), 0)
    is_ctx = row < len_ctx
    shift = jnp.where(is_ctx, sh_c[0], sh_l[0])
    scale = jnp.where(is_ctx, sc_c[0], sc_l[0])
    h = _norm_mod(x_ref[0], g_ref[...], shift, scale)
    o_ref[0] = jnp.dot(h.astype(BF16), w_ref[...], preferred_element_type=F32).astype(o_ref.dtype)


def _in_proj(x, mod, gain, w, len_ctx, out_dtype):
    bsz, n, d = x.shape
    nout = w.shape[1]
    assert n % TM_IN == 0 and nout % TN_IN == 0
    mspec = lambda is_lat, k: pl.BlockSpec((1, 1, d), lambda j, b, i: ((b * 2 + is_lat) * 6 + k, 0, 0))
    return pl.pallas_call(
        functools.partial(_in_proj_kernel, len_ctx),
        grid=(nout // TN_IN, bsz, n // TM_IN),
        in_specs=[
            pl.BlockSpec((1, TM_IN, d), lambda j, b, i: (b, i, 0)),
            mspec(0, 0), mspec(0, 1), mspec(1, 0), mspec(1, 1),
            pl.BlockSpec((1, d), lambda j, b, i: (0, 0)),
            pl.BlockSpec((d, TN_IN), lambda j, b, i: (0, j)),
        ],
        out_specs=pl.BlockSpec((1, TM_IN, TN_IN), lambda j, b, i: (b, i, j)),
        out_shape=jax.ShapeDtypeStruct((bsz, n, nout), out_dtype),
        compiler_params=_params("parallel", "parallel", "parallel"),
        name="in_proj",
    )(x, mod, mod, mod, mod, gain, w)


def _halo_specs(width, col, n):
    nb = n // SUBLANES
    per = TM // SUBLANES
    return [
        pl.BlockSpec((1, TM, width), lambda b, i: (b, i, col)),
        pl.BlockSpec((1, SUBLANES, width), lambda b, i: (b, jnp.maximum(i * per - 1, 0), col)),
        pl.BlockSpec((1, SUBLANES, width), lambda b, i: (b, jnp.minimum((i + 1) * per, nb - 1), col)),
    ]


def _prep_kernel(nct, nt, rkv_ref, rkv_p, rkv_n, lo_ref, lo_p, lo_n, gd_ref, gd_p, gd_n,
                 mup_ref, mun_ref, du_ref, w0_ref, au_ref, a0_ref, kkw_ref, ka_ref, ones_ref,
                 wf_ref, wb_ref, kdf_ref, kdb_ref, bf_ref, bb_ref, v_ref, kk_ref, r_ref, gdo_ref):
    i = pl.program_id(1)
    first = jnp.logical_or(i == 0, i == nct)
    last = jnp.logical_or(i == nct - 1, i == nt - 1)
    row = lax.broadcasted_iota(jnp.int32, (TM, 1), 0)
    c = rkv_ref.shape[2] // 3

    def mix(main, prev8, next8, lo, hi):
        pm = main[0]
        prow = jnp.where(first, 0.0, prev8[0, SUBLANES - 1:SUBLANES, :])
        nrow = jnp.where(last, 0.0, next8[0, 0:1, :])
        prev = jnp.where(row == 0, prow, pltpu.roll(pm, 1, 0))
        nxt = jnp.where(row == TM - 1, nrow, pltpu.roll(pm, TM - 1, 0))
        return pm + mup_ref[:, lo:hi] * (prev - pm) + mun_ref[:, lo:hi] * (nxt - pm)

    rkv = mix(rkv_ref, rkv_p, rkv_n, 0, 3 * c)
    lora = mix(lo_ref, lo_p, lo_n, 3 * c, 3 * c + 2 * LANES)
    gdo_ref[0] = mix(gd_ref, gd_p, gd_n, 3 * c + 2 * LANES, 3 * c + 4 * LANES)

    r, k, v = rkv[:, 0:c], rkv[:, c:2 * c], rkv[:, 2 * c:3 * c]
    wd, ad = lora[:, 0:LANES], lora[:, LANES:2 * LANES]
    dec = _mm(jnp.tanh(wd), du_ref[...], split_a=True, split_b=True) + w0_ref[...]
    log_decay = -float(np.exp(-0.5)) * jax.nn.sigmoid(dec)
    a = jax.nn.sigmoid(_mm(ad, au_ref[...], split_a=True, split_b=True) + a0_ref[...])
    kk = k * kkw_ref[...]
    kk = kk * lax.rsqrt(jnp.maximum(_mm(kk * kk, ones_ref[...], split_a=True), KK_EPS))
    ka = ka_ref[...]
    for z, (w_o, kd_o, b_o) in enumerate(((wf_ref, kdf_ref, bf_ref), (wb_ref, kdb_ref, bb_ref))):
        az = a[:, z * c:(z + 1) * c]
        w_o[0] = log_decay[:, z * c:(z + 1) * c]
        kd_o[0] = k * (1.0 + (az - 1.0) * ka)
        b_o[0] = kk * az
    v_ref[0] = v
    kk_ref[0] = kk
    r_ref[0] = r


def _rwkv_prep(p, lw, nct):
    bsz, n, _ = p.shape
    c = lw["k_k"].shape[1]
    nt = n // TM
    tok = jax.ShapeDtypeStruct((bsz, n, c), F32)
    vec = lambda a: _const_spec(a.shape)
    consts = [lw["mu_prev"], lw["mu_next"], lw["decay_up"], lw["decay_w0"], lw["iclr_up"], lw["iclr_a0"],
              lw["k_k"], lw["k_a"], lw["ones_bd"]]
    return pl.pallas_call(
        functools.partial(_prep_kernel, nct, nt),
        grid=(bsz, nt),
        in_specs=_halo_specs(3 * c, 0, n) + _halo_specs(2 * LANES, (6 * c) // (2 * LANES), n)
        + _halo_specs(2 * LANES, (6 * c) // (2 * LANES) + 1, n) + [vec(a) for a in consts],
        out_specs=[pl.BlockSpec((1, TM, c), lambda b, i: (b, i, 0))] * 9
        + [pl.BlockSpec((1, TM, 2 * LANES), lambda b, i: (b, i, 0))],
        out_shape=[tok] * 9 + [jax.ShapeDtypeStruct((bsz, n, 2 * LANES), F32)],
        compiler_params=_params("parallel", "parallel"),
        name="rwkv_prep",
    )(p, p, p, p, p, p, p, p, p, *consts)


def _split(a):
    hi = a.astype(BF16)
    return hi, (a - hi.astype(F32)).astype(BF16)


def _mm(a, b, split_a=False, split_b=False):
    def halves(x, split):
        if x.dtype == BF16 or not split:
            return x.astype(BF16), None
        return _split(x)

    (ah, al), (bh, bl) = halves(a, split_a), halves(b, split_b)
    lhs, rhs = [ah], [bh]
    if al is not None:
        lhs.append(al)
        rhs.append(bh)
    if bl is not None:
        lhs.append(ah)
        rhs.append(bl)
    if len(lhs) == 1:
        return jnp.dot(ah, bh, preferred_element_type=F32)
    return jnp.dot(jnp.concatenate(lhs, axis=1), jnp.concatenate(rhs, axis=0), preferred_element_type=F32)


def _chunk_scan_kernel(ldf, kdf, bf, vf, kkf, rf, ldb, kdb, bb, vb, kkb, rb, yf_ref, yb_ref, h_ref):
    @pl.when(pl.program_id(0) == 0)
    def _():
        h_ref[...] = jnp.zeros_like(h_ref)

    bsz, cs, c = ldf.shape
    t_i = lax.broadcasted_iota(jnp.int32, (cs, LANES), 0)
    lane = lax.broadcasted_iota(jnp.int32, (cs, LANES), 1)
    s_i = jnp.bitwise_and(lane, HEAD - 1)
    m_a = (lane < HEAD).astype(F32).astype(BF16)
    m_b = (lane >= HEAD).astype(F32).astype(BF16)
    r2 = lax.broadcasted_iota(jnp.int32, (LANES, LANES), 0)
    l2 = lax.broadcasted_iota(jnp.int32, (LANES, LANES), 1)
    block = ((r2 >= HEAD) == (l2 >= HEAD)).astype(F32)
    eye = (r2 == l2).astype(F32)
    zeros = jnp.zeros((cs, LANES), BF16)

    def bd(x):
        k = x.shape[1] // LANES
        return jnp.concatenate([x * jnp.concatenate([m_a] * k, axis=1), x * jnp.concatenate([m_b] * k, axis=1)],
                               axis=0)

    def unit(d, ld, kd, b_, v, kk, r, h):
        before = (s_i < t_i) if d == 0 else (s_i > t_i)
        upto = (s_i <= t_i) if d == 0 else (s_i >= t_i)
        tri = upto.astype(F32).astype(BF16)
        lh, ll = _split(ld)
        big_l = jnp.dot(tri, jnp.concatenate([lh, ll], axis=0), preferred_element_type=F32)
        yield
        ltot = big_l[cs - 1:cs] if d == 0 else big_l[0:1]
        kap = kk * jnp.exp(big_l - ld)
        rt = r * jnp.exp(big_l)
        einv = jnp.exp(-big_l)
        kt, bt = kd * einv, b_ * einv
        efin = jnp.exp(ltot - big_l)
        khat, bhat = kd * efin, b_ * efin
        ktb, btb, vb16 = kt.astype(BF16), bt.astype(BF16), v.astype(BF16)
        sc = _dot_nt(jnp.concatenate([kap, rt], axis=0).astype(BF16),
                     jnp.concatenate([ktb * m_a, ktb * m_b, btb * m_a, btb * m_b], axis=0))
        yield
        a_k = jnp.where(before, sc[:cs, :LANES], 0.0)
        n_p = jnp.where(before, -sc[:cs, LANES:], 0.0)
        m_k = jnp.where(upto, sc[cs:, :LANES], 0.0)
        m_nb = jnp.where(upto, -sc[cs:, LANES:], 0.0)
        x = jnp.concatenate([kap, _mm(a_k, bd(vb16))], axis=1)
        yield
        for level in range(6):
            xh, xl = _split(x)
            nb = n_p.astype(BF16)
            x = x + jnp.dot(jnp.concatenate([nb, nb], axis=1), jnp.concatenate([bd(xh), bd(xl)], axis=0),
                            preferred_element_type=F32)
            if level < 5:
                n_p = jnp.dot(nb, bd(nb), preferred_element_type=F32)
            yield
        vz = jnp.concatenate([vb16, zeros], axis=1)
        uw = jnp.concatenate([x[:, LANES:], x[:, :LANES]], axis=1).astype(BF16)
        o1 = _mm(jnp.concatenate([m_k, m_nb], axis=1), jnp.concatenate([bd(vz), bd(uw)], axis=0))
        o2 = _mm(jnp.concatenate([khat, -bhat], axis=0).T, jnp.concatenate([vz, uw], axis=0))
        yield
        y0, q = o1[:, :LANES], rt + o1[:, LANES:]
        psi = o2[:, :LANES] * block
        phi = o2[:, LANES:] * block + eye * jnp.exp(ltot)
        y = y0 + _mm(q, h)
        return y, _mm(phi, h, split_a=True, split_b=True) + psi

    dirs = ((ldf, kdf, bf, vf, kkf, rf, yf_ref), (ldb, kdb, bb, vb, kkb, rb, yb_ref))
    keys = [(d, b, p) for d in range(2) for b in range(bsz) for p in range(c // LANES)]
    lanes_of = lambda p: slice(p * LANES, (p + 1) * LANES)
    for lo in range(0, len(keys), SCAN_BATCH):
        units = {k: unit(k[0], *[ref[k[1], :, lanes_of(k[2])] for ref in dirs[k[0]][:6]], h_ref[k[1], k[0], k[2]])
                 for k in keys[lo:lo + SCAN_BATCH]}
        for (d, b, p), (y, h_new) in _run_interleaved(units).items():
            dirs[d][6][b, :, lanes_of(p)] = y
            h_ref[b, d, p] = h_new


def _rwkv_scan(ld_f, ld_b, kd_f, kd_b, b_f, b_b, v, kk, r, nct_c):
    bsz, n, c = v.shape
    nc = n // CHUNK

    def rev(g):
        return jnp.where(g < nct_c, nct_c - 1 - g, nc - 1 - g + nct_c)

    fwd = pl.BlockSpec((bsz, CHUNK, c), lambda g: (0, g, 0))
    bwd = pl.BlockSpec((bsz, CHUNK, c), lambda g: (0, rev(g), 0))
    tok = jax.ShapeDtypeStruct((bsz, n, c), F32)
    return pl.pallas_call(
        _chunk_scan_kernel,
        grid=(nc,),
        in_specs=[fwd] * 6 + [bwd] * 6,
        out_specs=[fwd, bwd],
        out_shape=[tok, tok],
        scratch_shapes=[pltpu.VMEM((bsz, 2, c // LANES, LANES, LANES), F32)],
        compiler_params=_params("arbitrary"),
        name="rwkv_scan",
    )(ld_f, kd_f, b_f, v, kk, r, ld_b, kd_b, b_b, v, kk, r)


def _readout_kernel(yf, yb, r, kdf, kdb, v, gd, gnw, gnb, rk, gup, ones_ref, a_ref):
    ones = ones_ref[...]
    inv = 1.0 / HEAD
    y = yf[0] + yb[0]
    yc = y - _mm(y, ones, split_a=True) * inv
    var = _mm(yc * yc, ones, split_a=True) * inv
    yn = yc * lax.rsqrt(var + GN_EPS) * gnw[...] + gnb[...]
    bonus = _mm(r[0] * rk[...] * (kdf[0] + kdb[0]), ones, split_a=True) * v[0]
    g = _mm(jax.nn.sigmoid(gd[0]), gup[...])
    a_ref[0] = (yn + bonus) * g


def _rwkv_readout(y_f, y_b, r, kd_f, kd_b, v, gd, lw):
    bsz, n, c = v.shape
    tok = pl.BlockSpec((1, TM, c), lambda b, i: (b, i, 0))
    consts = [lw["gn_w"], lw["gn_b"], lw["r_k"], lw["gate_up"], lw["ones_bd"]]
    return pl.pallas_call(
        _readout_kernel,
        grid=(bsz, n // TM),
        in_specs=[tok] * 6 + [pl.BlockSpec((1, TM, 2 * LANES), lambda b, i: (b, i, 0))]
        + [_const_spec(a.shape) for a in consts],
        out_specs=tok,
        out_shape=jax.ShapeDtypeStruct((bsz, n, c), F32),
        compiler_params=_params("parallel", "parallel"),
        name="rwkv_readout",
    )(y_f, y_b, r, kd_f, kd_b, v, gd, *consts)


def _pool_kernel(nct, len_ctx, len_lat, main, prev8, next8, pw_ref, scale_ref, o_ref):
    i = pl.program_id(1)
    is_lat = i >= nct
    seq_len = jnp.where(is_lat, len_lat, len_ctx)
    t0 = jnp.where(is_lat, i - nct, i) * TM
    n = TM + 2 * SUBLANES
    pm = main[0]
    ext = jnp.concatenate([prev8[0], pm, next8[0]], axis=0)
    pos = t0 - SUBLANES + lax.broadcasted_iota(jnp.int32, (n, 1), 0)
    e = jnp.where(jnp.logical_and(pos >= 0, pos < seq_len), ext, 0.0)
    a2 = e + pltpu.roll(e, 1, 0)
    a4 = pltpu.roll(a2, 1, 0) + pltpu.roll(a2, n - 1, 0)
    a8 = pltpu.roll(a4, 2, 0) + pltpu.roll(a4, n - 2, 0)
    a16 = pltpu.roll(a8, 4, 0) + pltpu.roll(a8, n - 4, 0)
    t = t0 + lax.broadcasted_iota(jnp.int32, (TM, 1), 0)
    lane = lax.broadcasted_iota(jnp.int32, pm.shape, 1)
    mean = None
    for g, (w, acc) in reversed(list(enumerate(zip(POOL_WINDOWS, (a2, a4, a8, a16))))):
        lo = jnp.maximum(t - w // 2, 0)
        hi = jnp.minimum(t + (w - w // 2) - 1, seq_len - 1)
        m = acc[SUBLANES:SUBLANES + TM] / (hi - lo + 1).astype(F32)
        mean = m if mean is None else jnp.where(lane < (g + 1) * POOL_GROUP, m, mean)
    o_ref[0] = _mm(mean - pm, pw_ref[...], split_a=True, split_b=True) * scale_ref[...]


def _pool(p, lw, nct, len_ctx, len_lat, col):
    bsz, n, _ = p.shape
    width = lw["pool_w"].shape[0]
    return pl.pallas_call(
        functools.partial(_pool_kernel, nct, len_ctx, len_lat),
        grid=(bsz, n // TM),
        in_specs=_halo_specs(width, col, n) + [_const_spec(lw["pool_w"].shape), _const_spec(lw["pool_scale"].shape)],
        out_specs=pl.BlockSpec((1, TM, width), lambda b, i: (b, i, 0)),
        out_shape=jax.ShapeDtypeStruct((bsz, n, width), F32),
        compiler_params=_params("parallel", "parallel"),
        name="pool",
    )(p, p, p, lw["pool_w"], lw["pool_scale"])


def _qk_norm_kernel(q_ref, k_ref, v_ref, qg_ref, kg_ref, ones_ref, qo, ko, vo):
    ones = ones_ref[...]
    inv = 1.0 / HEAD
    q, k = q_ref[0], k_ref[0]
    qn = q * lax.rsqrt(_mm(q * q, ones, split_a=True) * inv + NORM_EPS) * qg_ref[...]
    kn = k * lax.rsqrt(_mm(k * k, ones, split_a=True) * inv + NORM_EPS) * kg_ref[...]
    for o_ref, val in ((qo, qn * HEAD ** -0.5), (ko, kn), (vo, v_ref[0])):
        val = val.astype(BF16)
        for h in range(o_ref.shape[1]):
            o_ref[0, h] = val[:, h * HEAD:(h + 1) * HEAD]


def _qk_norm(p, lw, col):
    bsz, n, _ = p.shape
    c = lw["q_gain"].shape[1]
    tok = jax.ShapeDtypeStruct((bsz, c // HEAD, n, HEAD), BF16)
    ospec = pl.BlockSpec((1, c // HEAD, TM, HEAD), lambda b, i: (b, 0, i, 0))
    consts = [lw["q_gain"], lw["k_gain"], lw["ones_bd"]]
    return pl.pallas_call(
        _qk_norm_kernel,
        grid=(bsz, n // TM),
        in_specs=[pl.BlockSpec((1, TM, c), lambda b, i, j=j: (b, i, col + j)) for j in range(3)]
        + [_const_spec(a.shape) for a in consts],
        out_specs=[ospec] * 3,
        out_shape=[tok] * 3,
        compiler_params=_params("parallel", "parallel"),
        name="qk_norm",
    )(p, p, p, *consts)


def _run_interleaved(units):
    done = {}
    while units:
        for k in list(units):
            try:
                next(units[k])
            except StopIteration as stop:
                done[k] = stop.value
                del units[k]
    return done


def _attend(q, key_sets, bias):
    scores = [_dot_nt(q, k) for k, _ in key_sets]
    yield
    if bias is not None:
        scores[0] = scores[0] + bias
    m = functools.reduce(jnp.maximum, [jnp.max(s, axis=-1, keepdims=True) for s in scores])
    yield
    ps = [jnp.exp(s - m) for s in scores]
    den = functools.reduce(jnp.add, [jnp.sum(p, axis=-1, keepdims=True) for p in ps])
    num = functools.reduce(jnp.add, [jnp.dot(p.astype(BF16), v, preferred_element_type=F32)
                                      for p, (_, v) in zip(ps, key_sets)])
    yield
    return num / den


def _nat_kernel(rows, q_ref, kp, kc, kn, vp, vc, vn, kx_ref, vx_ref, bias_ref, o_ref, ks, vs):
    i = pl.program_id(1) - 1
    heads, tq = q_ref.shape[1], q_ref.shape[2]
    rb = tq // GRID_W
    nloc = WIN_H * GRID_W

    @pl.when(i < 0)
    def _():
        units = {h: _attend(q_ref[0, h], [(kx_ref[0, h], vx_ref[0, h])], None) for h in range(heads)}
        for h, o in _run_interleaved(units).items():
            o_ref[0, :, h * HEAD:(h + 1) * HEAD] = o

    @pl.when(i >= 0)
    def _():
        for j, (kr, vr) in enumerate(((kp, vp), (kc, vc), (kn, vn))):
            ks[:, j * tq:(j + 1) * tq, :] = kr[0]
            vs[:, j * tq:(j + 1) * tq, :] = vr[0]
        for rr in range(rb):
            r = i * rb + rr
            rs = jnp.clip(r - WIN_H // 2, 0, rows - WIN_H)
            off = r - rs
            start = pl.multiple_of((rs - i * rb + rb) * GRID_W, GRID_W)
            qs = slice(rr * GRID_W, (rr + 1) * GRID_W)
            units = {h: _attend(q_ref[0, h, qs], [(ks[h, pl.ds(start, nloc)], vs[h, pl.ds(start, nloc)]),
                                                  (kx_ref[0, h], vx_ref[0, h])], bias_ref[h, off])
                     for h in range(heads)}
            for h, o in _run_interleaved(units).items():
                o_ref[0, qs, h * HEAD:(h + 1) * HEAD] = o


def _nat_attention(qn, kn, vn, bias, len_ctx, len_lat):
    bsz, heads, n, hd = qn.shape
    tq = len_ctx
    assert tq == (WIN_H // 2) * GRID_W and len_lat % tq == 0
    rows = len_lat // GRID_W
    nblk = len_lat // tq

    def blk(shift):
        return pl.BlockSpec((1, heads, tq, hd),
                            lambda b, i: (b, 0, jnp.where(i == 0, 0, 1 + jnp.clip(i - 1 + shift, 0, nblk - 1)), 0))

    ctx = pl.BlockSpec((1, heads, tq, hd), lambda b, i: (b, 0, 0, 0))
    return pl.pallas_call(
        functools.partial(_nat_kernel, rows),
        grid=(bsz, 1 + nblk),
        in_specs=[blk(0), blk(-1), blk(0), blk(1), blk(-1), blk(0), blk(1), ctx, ctx, _const_spec(bias.shape)],
        out_specs=pl.BlockSpec((1, tq, heads * hd), lambda b, i: (b, i, 0)),
        out_shape=jax.ShapeDtypeStruct((bsz, n, heads * hd), F32),
        scratch_shapes=[pltpu.VMEM((heads, 3 * tq, hd), BF16), pltpu.VMEM((heads, 3 * tq, hd), BF16)],
        compiler_params=_params("parallel", "parallel"),
        name="nat_attention",
    )(qn, kn, kn, kn, vn, vn, vn, kn, vn, bias)


def _nat_bias_table(rpb):
    qc = np.arange(GRID_W)[:, None]
    kc = np.arange(GRID_W)[None, :]
    cs = np.clip(qc - WIN_W // 2, 0, GRID_W - WIN_W)
    valid = (kc >= cs) & (kc < cs + WIN_W)
    dc = kc - qc + WIN_W - 1
    pick = (np.arange(2 * WIN_W - 1)[:, None, None] == dc[None]) & valid[None]
    cols = jnp.einsum("hdm,mqk->hdqk", rpb, jnp.asarray(pick, F32), precision=lax.Precision.HIGHEST)
    cols = jnp.where(valid[None, None], cols, MASK_BIAS)
    t = jnp.stack([cols[:, WIN_H - 1 - off:2 * WIN_H - 1 - off] for off in range(WIN_H)], axis=1)
    return t.transpose(0, 1, 3, 2, 4).reshape(rpb.shape[0], WIN_H, GRID_W, WIN_H * GRID_W)


def _merge_kernel(a, bp, cn, ga, gb, gc, x, g1, wa, wb, wc, wo, o_ref):
    sig = lambda g: jax.nn.sigmoid(g[0].astype(F32))
    m = sig(ga) * _dot_bf(a[0], wa[...]) + sig(gb) * _dot_bf(bp[0], wb[...]) + sig(gc) * _dot_bf(cn[0], wc[...])
    o_ref[0] = x[0] + g1[0] * _dot_bf(m, wo[...])


def _merge(a, bp, cn, gates, x, mod, lw, nct):
    bsz, n, d = x.shape
    tok = lambda w, col=0: pl.BlockSpec((1, TM, w), lambda b, i: (b, i, col))
    ws = [lw["w_rwkv_o"], lw["w_pool_o"], lw["w_nat_o"], lw["w_out"]]
    return pl.pallas_call(
        _merge_kernel,
        grid=(bsz, n // TM),
        in_specs=[tok(a.shape[2]), tok(bp.shape[2]), tok(cn.shape[2]), tok(d, 0), tok(d, 1), tok(d, 2), tok(d),
                  pl.BlockSpec((1, 1, d), _mod_spec(nct, 2))] + [_const_spec(w.shape) for w in ws],
        out_specs=tok(d),
        out_shape=jax.ShapeDtypeStruct((bsz, n, d), F32),
        compiler_params=_params("parallel", "parallel"),
        name="merge",
    )(a, bp, cn, gates, gates, gates, x, mod, *ws)


def _ffn_kernel(x_ref, sh, sc, g2, gain, w1, w2, o_ref):
    x = x_ref[0]
    h = _norm_mod(x, gain[...], sh[0], sc[0])
    u = jnp.dot(h.astype(BF16), w1[...], preferred_element_type=F32)
    hid = w2.shape[0]
    gate, up = u[:, :hid], u[:, hid:]
    act = gate * jax.nn.sigmoid(gate) * up
    o_ref[0] = x + g2[0] * jnp.dot(act.astype(BF16), w2[...], preferred_element_type=F32)


def _ffn(x, mod, lw, nct):
    bsz, n, d = x.shape
    tok = pl.BlockSpec((1, TM, d), lambda b, i: (b, i, 0))
    mspec = lambda k: pl.BlockSpec((1, 1, d), _mod_spec(nct, k))
    return pl.pallas_call(
        _ffn_kernel,
        grid=(bsz, n // TM),
        in_specs=[tok, mspec(3), mspec(4), mspec(5), _const_spec(lw["norm2"].shape),
                  _const_spec(lw["w_ffn_in"].shape), _const_spec(lw["w_ffn_out"].shape)],
        out_specs=tok,
        out_shape=jax.ShapeDtypeStruct((bsz, n, d), F32),
        compiler_params=_params("parallel", "parallel"),
        name="ffn",
    )(x, mod, mod, mod, lw["norm2"], lw["w_ffn_in"], lw["w_ffn_out"])


def _block_diag(blocks):
    n = len(blocks)
    rows = []
    for i, blk in enumerate(blocks):
        rows.append(jnp.concatenate([blk if j == i else jnp.zeros((blk.shape[0], blocks[j].shape[1]), blk.dtype)
                                     for j in range(n)], axis=1))
    return jnp.concatenate(rows, axis=0)


def _pad_cols(a, width):
    return jnp.pad(a, ((0, 0), (0, width - a.shape[1])))


def _layer_weights(l, prm):
    c = prm["k_k"].shape[1]
    lora = prm["decay_up"].shape[2]
    gl = prm["gate_up"].shape[1]
    pool = prm["pool_scale"].shape[1]
    d = prm["w_out"].shape[1]
    w_in = prm["w_in"][l]
    o_lora, o_gd, o_pool = 3 * c, 3 * c + 4 * lora, 3 * c + 4 * lora + gl
    o_q = o_pool + pool
    o_gate = o_q + 3 * c
    assert 4 * lora == 2 * LANES and gl <= 2 * LANES and pool == 2 * LANES and o_gate + 3 * d == w_in.shape[1]
    w_mix = jnp.concatenate([
        w_in[:, 0:o_lora], w_in[:, o_q:o_gate], w_in[:, o_lora:o_gd],
        _pad_cols(w_in[:, o_gd:o_pool], 2 * LANES), w_in[:, o_pool:o_q]], axis=1).astype(BF16)
    mu = lambda m: _pad_cols(m[l][None, :o_pool], o_pool + 2 * LANES - gl)
    heads = c // HEAD
    row = lambda a: a.reshape(1, -1)
    return {
        "w_mix": w_mix, "w_gate": w_in[:, o_gate:].astype(BF16),
        "norm1": row(prm["norm1"][l]), "norm2": row(prm["norm2"][l]),
        "mu_prev": mu(prm["mu_prev"]), "mu_next": mu(prm["mu_next"]),
        "decay_up": _block_diag([prm["decay_up"][l, 0], prm["decay_up"][l, 1]]),
        "decay_w0": row(prm["decay_w0"][l]),
        "iclr_up": _block_diag([prm["iclr_up"][l, 0], prm["iclr_up"][l, 1]]),
        "iclr_a0": row(prm["iclr_a0"][l]),
        "k_k": row(prm["k_k"][l]), "k_a": row(prm["k_a"][l]), "r_k": row(prm["r_k"][l]),
        "gn_w": row(prm["gn_w"][l]), "gn_b": row(prm["gn_b"][l]),
        "gate_up": jnp.pad(prm["gate_up"][l], ((0, 2 * LANES - gl), (0, 0))),
        "ones_bd": jnp.kron(jnp.eye(heads, dtype=F32), jnp.ones((HEAD, HEAD), F32)).astype(BF16),
        "pool_w": _block_diag([prm["pool_w"][l, g] for g in range(len(POOL_WINDOWS))]),
        "pool_scale": row(prm["pool_scale"][l]),
        "q_gain": row(jnp.tile(prm["q_gain"][l], heads)), "k_gain": row(jnp.tile(prm["k_gain"][l], heads)),
        "nat_bias": _nat_bias_table(prm["rpb"][l]),
        "w_rwkv_o": prm["w_rwkv_o"][l].astype(BF16), "w_pool_o": prm["w_pool_o"][l].astype(BF16),
        "w_nat_o": prm["w_nat_o"][l].astype(BF16), "w_out": prm["w_out"][l].astype(BF16),
        "w_ffn_in": prm["w_ffn_in"][l].astype(BF16), "w_ffn_out": prm["w_ffn_out"][l].astype(BF16),
    }


def kernel(x, c, ctx, c_ctx, w_mod, b_mod, norm1, norm2, w_in, mu_prev, mu_next, decay_w0, decay_up, iclr_a0, iclr_up, gate_up, k_k, k_a, r_k, gn_w, gn_b, pool_w, pool_scale, q_gain, k_gain, rpb, w_rwkv_o, w_pool_o, w_nat_o, w_out, w_ffn_in, w_ffn_out):
    prm = dict(norm1=norm1, norm2=norm2, w_in=w_in, mu_prev=mu_prev, mu_next=mu_next, decay_w0=decay_w0,
               decay_up=decay_up, iclr_a0=iclr_a0, iclr_up=iclr_up, gate_up=gate_up, k_k=k_k, k_a=k_a, r_k=r_k,
               gn_w=gn_w, gn_b=gn_b, pool_w=pool_w, pool_scale=pool_scale, q_gain=q_gain, k_gain=k_gain, rpb=rpb,
               w_rwkv_o=w_rwkv_o, w_pool_o=w_pool_o, w_nat_o=w_nat_o, w_out=w_out, w_ffn_in=w_ffn_in,
               w_ffn_out=w_ffn_out)
    bsz, len_lat, d = x.shape
    len_ctx = ctx.shape[1]
    depth = w_mod.shape[0]
    assert len_ctx % TM == 0 and len_lat % TM == 0 and bsz + 1 <= SUBLANES
    nct = len_ctx // TM
    cdim = k_k.shape[1]

    s_rows = jnp.concatenate([c, c_ctx[None, :], jnp.zeros((SUBLANES - bsz - 1, d), F32)], axis=0)
    mod_all = _modulation(s_rows, w_mod, b_mod)
    xa = jnp.concatenate([ctx, x], axis=1)

    for l in range(depth):
        lw = _layer_weights(l, prm)
        m_lat = mod_all[l, :bsz]
        m_ctx = jnp.broadcast_to(mod_all[l, bsz][None], m_lat.shape)
        mod = jnp.stack([m_ctx, m_lat], axis=1).reshape(bsz * 2 * 6, 1, d)

        p = _in_proj(xa, mod, lw["norm1"], lw["w_mix"], len_ctx, F32)
        gates = _in_proj(xa, mod, lw["norm1"], lw["w_gate"], len_ctx, BF16)
        w_f, w_b, kd_f, kd_b, b_f, b_b, v, kk, r, gd = _rwkv_prep(p, lw, nct)
        y_f, y_b = _rwkv_scan(w_f, w_b, kd_f, kd_b, b_f, b_b, v, kk, r, len_ctx // CHUNK)
        a_br = _rwkv_readout(y_f, y_b, r, kd_f, kd_b, v, gd, lw)
        b_br = _pool(p, lw, nct, len_ctx, len_lat, (6 * cdim + 4 * LANES) // (2 * LANES))
        qn, kn, vn = _qk_norm(p, lw, 3)
        c_br = _nat_attention(qn, kn, vn, lw["nat_bias"], len_ctx, len_lat)
        xa = _merge(a_br, b_br, c_br, gates, xa, mod, lw, nct)
        xa = _ffn(xa, mod, lw, nct)
    return xa[:, len_ctx:]
```

```python
import functools

import numpy as np
import jax
import jax.numpy as jnp
from jax import lax
from jax.experimental import pallas as pl
from jax.experimental.pallas import tpu as pltpu

F32 = jnp.float32
BF16 = jnp.bfloat16

HEAD = 64
NORM_EPS = 1e-6
GN_EPS = 64e-5
KK_EPS = 1e-24
POOL_WINDOWS = (2, 4, 8, 16)
POOL_GROUP = 64
GRID_W = 64
WIN_H = 8
WIN_W = 16
MASK_BIAS = -1e30

LANES = 128
SUBLANES = 8
VMEM_LIMIT = 56 * 1024 * 1024

TM = 256
NAT_ROWS = 4
SCAN_BATCH = 12
CHUNK = 64
TM_IN = 640
TN_IN = 3072
TN_MOD = 1536


def _dot_hi(a, b):
    return jnp.dot(a, b, precision=lax.Precision.HIGHEST, preferred_element_type=F32)


def _dot_bf(a, b):
    return jnp.dot(a.astype(BF16), b.astype(BF16), preferred_element_type=F32)


def _dot_nt(a, b):
    return lax.dot_general(a, b, (((1,), (1,)), ((), ())), preferred_element_type=F32)


def _params(*sem):
    return pltpu.CompilerParams(dimension_semantics=sem, vmem_limit_bytes=VMEM_LIMIT)


def _const_spec(shape):
    nd = len(shape)
    return pl.BlockSpec(shape, lambda *_: (0,) * nd, pipeline_mode=pl.Buffered(1))


def _mod_kernel(s_ref, w_ref, b_ref, o_ref):
    s = s_ref[...]
    s = s * jax.nn.sigmoid(s)
    o_ref[0] = _dot_hi(s, w_ref[0]) + b_ref[0]


def _modulation(s_rows, w_mod, b_mod):
    depth, d, n = w_mod.shape
    return pl.pallas_call(
        _mod_kernel,
        grid=(depth, n // TN_MOD),
        in_specs=[
            pl.BlockSpec((SUBLANES, d), lambda l, j: (0, 0)),
            pl.BlockSpec((1, d, TN_MOD), lambda l, j: (l, 0, j)),
            pl.BlockSpec((1, 1, TN_MOD), lambda l, j: (l, 0, j)),
        ],
        out_specs=pl.BlockSpec((1, SUBLANES, TN_MOD), lambda l, j: (l, 0, j)),
        out_shape=jax.ShapeDtypeStruct((depth, SUBLANES, n), F32),
        compiler_params=_params("parallel", "parallel"),
        name="modulation",
    )(s_rows, w_mod, b_mod.reshape(depth, 1, n))


def _mod_spec(nct, k):
    return lambda b, i: ((b * 2 + (i >= nct).astype(jnp.int32)) * 6 + k, 0, 0)


def _norm_mod(x, gain, shift, scale):
    ms = jnp.mean(x * x, axis=-1, keepdims=True)
    return x * lax.rsqrt(ms + NORM_EPS) * gain * (1.0 + scale) + shift


def _in_proj_kernel(len_ctx, x_ref, sh_c, sc_c, sh_l, sc_l, g_ref, w_ref, o_ref):
    tm = x_ref.shape[1]
    row = pl.program_id(2) * tm + lax.broadcasted_iota(jnp.int32, (tm, 1), 0)
    is_ctx = row < len_ctx
    shift = jnp.where(is_ctx, sh_c[0], sh_l[0])
    scale = jnp.where(is_ctx, sc_c[0], sc_l[0])
    h = _norm_mod(x_ref[0], g_ref[...], shift, scale)
    o_ref[0] = jnp.dot(h.astype(BF16), w_ref[...], preferred_element_type=F32).astype(o_ref.dtype)


def _in_proj(x, mod, gain, w, len_ctx, out_dtype):
    bsz, n, d = x.shape
    nout = w.shape[1]
    assert n % TM_IN == 0 and nout % TN_IN == 0
    mspec = lambda is_lat, k: pl.BlockSpec((1, 1, d), lambda j, b, i: ((b * 2 + is_lat) * 6 + k, 0, 0))
    return pl.pallas_call(
        functools.partial(_in_proj_kernel, len_ctx),
        grid=(nout // TN_IN, bsz, n // TM_IN),
        in_specs=[
            pl.BlockSpec((1, TM_IN, d), lambda j, b, i: (b, i, 0)),
            mspec(0, 0), mspec(0, 1), mspec(1, 0), mspec(1, 1),
            pl.BlockSpec((1, d), lambda j, b, i: (0, 0)),
            pl.BlockSpec((d, TN_IN), lambda j, b, i: (0, j)),
        ],
        out_specs=pl.BlockSpec((1, TM_IN, TN_IN), lambda j, b, i: (b, i, j)),
        out_shape=jax.ShapeDtypeStruct((bsz, n, nout), out_dtype),
        compiler_params=_params("parallel", "parallel", "parallel"),
        name="in_proj",
    )(x, mod, mod, mod, mod, gain, w)


def _halo_specs(width, col, n):
    nb = n // SUBLANES
    per = TM // SUBLANES
    return [
        pl.BlockSpec((1, TM, width), lambda b, i: (b, i, col)),
        pl.BlockSpec((1, SUBLANES, width), lambda b, i: (b, jnp.maximum(i * per - 1, 0), col)),
        pl.BlockSpec((1, SUBLANES, width), lambda b, i: (b, jnp.minimum((i + 1) * per, nb - 1), col)),
    ]


def _prep_kernel(nct, nt, rkv_ref, rkv_p, rkv_n, lo_ref, lo_p, lo_n, gd_ref, gd_p, gd_n,
                 mup_ref, mun_ref, du_ref, w0_ref, au_ref, a0_ref, kkw_ref, ka_ref, ones_ref,
                 wf_ref, wb_ref, kdf_ref, kdb_ref, bf_ref, bb_ref, v_ref, kk_ref, r_ref, gdo_ref):
    i = pl.program_id(1)
    first = jnp.logical_or(i == 0, i == nct)
    last = jnp.logical_or(i == nct - 1, i == nt - 1)
    row = lax.broadcasted_iota(jnp.int32, (TM, 1), 0)
    c = rkv_ref.shape[2] // 3

    def mix(main, prev8, next8, lo, hi):
        pm = main[0]
        prow = jnp.where(first, 0.0, prev8[0, SUBLANES - 1:SUBLANES, :])
        nrow = jnp.where(last, 0.0, next8[0, 0:1, :])
        prev = jnp.where(row == 0, prow, pltpu.roll(pm, 1, 0))
        nxt = jnp.where(row == TM - 1, nrow, pltpu.roll(pm, TM - 1, 0))
        return pm + mup_ref[:, lo:hi] * (prev - pm) + mun_ref[:, lo:hi] * (nxt - pm)

    rkv = mix(rkv_ref, rkv_p, rkv_n, 0, 3 * c)
    lora = mix(lo_ref, lo_p, lo_n, 3 * c, 3 * c + 2 * LANES)
    gdo_ref[0] = mix(gd_ref, gd_p, gd_n, 3 * c + 2 * LANES, 3 * c + 4 * LANES)

    r, k, v = rkv[:, 0:c], rkv[:, c:2 * c], rkv[:, 2 * c:3 * c]
    wd, ad = lora[:, 0:LANES], lora[:, LANES:2 * LANES]
    dec = _mm(jnp.tanh(wd), du_ref[...], split_a=True, split_b=True) + w0_ref[...]
    log_decay = -float(np.exp(-0.5)) * jax.nn.sigmoid(dec)
    a = jax.nn.sigmoid(_mm(ad, au_ref[...], split_a=True, split_b=True) + a0_ref[...])
    kk = k * kkw_ref[...]
    kk = kk * lax.rsqrt(jnp.maximum(_mm(kk * kk, ones_ref[...], split_a=True), KK_EPS))
    ka = ka_ref[...]
    for z, (w_o, kd_o, b_o) in enumerate(((wf_ref, kdf_ref, bf_ref), (wb_ref, kdb_ref, bb_ref))):
        az = a[:, z * c:(z + 1) * c]
        w_o[0] = log_decay[:, z * c:(z + 1) * c]
        kd_o[0] = k * (1.0 + (az - 1.0) * ka)
        b_o[0] = kk * az
    v_ref[0] = v
    kk_ref[0] = kk
    r_ref[0] = r


def _mixer_prep_kernel(nct, nt, len_ctx, len_lat, *refs):
    n_in = (18, 5, 6)
    n_out = (10, 1, 3)
    cuts = np.cumsum(n_in + n_out).tolist()
    prep_in, pool_in, qk_in, prep_out, pool_out, qk_out = [refs[a:b] for a, b in zip([0] + cuts[:-1], cuts)]
    _prep_kernel(nct, nt, *prep_in, *prep_out)
    _pool_kernel(nct, len_ctx, len_lat, *pool_in, *pool_out)
    _qk_norm_kernel(*qk_in, *qk_out)


def _mixer_prep(p, lw, nct, len_ctx, len_lat):
    bsz, n, _ = p.shape
    c = lw["k_k"].shape[1]
    nt = n // TM
    wide = 2 * LANES
    lora_col = (6 * c) // wide
    prep_consts = [lw["mu_prev"], lw["mu_next"], lw["decay_up"], lw["decay_w0"], lw["iclr_up"], lw["iclr_a0"],
                   lw["k_k"], lw["k_a"], lw["ones_bd"]]
    pool_consts = [lw["pool_w"], lw["pool_scale"]]
    qk_consts = [lw["q_gain"], lw["k_gain"], lw["ones_bd"]]
    consts = lambda arrs: [_const_spec(a.shape) for a in arrs]
    tok = lambda w: pl.BlockSpec((1, TM, w), lambda b, i: (b, i, 0))
    heads = c // HEAD
    head_major = pl.BlockSpec((1, heads, TM, HEAD), lambda b, i: (b, 0, i, 0))
    return pl.pallas_call(
        functools.partial(_mixer_prep_kernel, nct, nt, len_ctx, len_lat),
        grid=(bsz, nt),
        in_specs=_halo_specs(3 * c, 0, n) + _halo_specs(wide, lora_col, n) + _halo_specs(wide, lora_col + 1, n)
        + consts(prep_consts) + _halo_specs(wide, lora_col + 2, n) + consts(pool_consts)
        + [pl.BlockSpec((1, TM, c), lambda b, i, j=j: (b, i, 3 + j)) for j in range(3)] + consts(qk_consts),
        out_specs=[tok(c)] * 9 + [tok(wide), tok(wide)] + [head_major] * 3,
        out_shape=[jax.ShapeDtypeStruct((bsz, n, c), F32)] * 9 + [jax.ShapeDtypeStruct((bsz, n, wide), F32)] * 2
        + [jax.ShapeDtypeStruct((bsz, heads, n, HEAD), BF16)] * 3,
        compiler_params=_params("parallel", "parallel"),
        name="mixer_prep",
    )(*[p] * 9, *prep_consts, *[p] * 3, *pool_consts, *[p] * 3, *qk_consts)


def _split(a):
    hi = a.astype(BF16)
    return hi, (a - hi.astype(F32)).astype(BF16)


def _mm(a, b, split_a=False, split_b=False):
    def halves(x, split):
        if x.dtype == BF16 or not split:
            return x.astype(BF16), None
        return _split(x)

    (ah, al), (bh, bl) = halves(a, split_a), halves(b, split_b)
    lhs, rhs = [ah], [bh]
    if al is not None:
        lhs.append(al)
        rhs.append(bh)
    if bl is not None:
        lhs.append(ah)
        rhs.append(bl)
    if len(lhs) == 1:
        return jnp.dot(ah, bh, preferred_element_type=F32)
    return jnp.dot(jnp.concatenate(lhs, axis=1), jnp.concatenate(rhs, axis=0), preferred_element_type=F32)


def _chunk_scan_kernel(ldf, kdf, bf, vf, kkf, rf, ldb, kdb, bb, vb, kkb, rb, yf_ref, yb_ref, h_ref):
    @pl.when(pl.program_id(0) == 0)
    def _():
        h_ref[...] = jnp.zeros_like(h_ref)

    bsz, cs, c = ldf.shape
    t_i = lax.broadcasted_iota(jnp.int32, (cs, LANES), 0)
    lane = lax.broadcasted_iota(jnp.int32, (cs, LANES), 1)
    s_i = jnp.bitwise_and(lane, HEAD - 1)
    m_a = (lane < HEAD).astype(F32).astype(BF16)
    m_b = (lane >= HEAD).astype(F32).astype(BF16)
    r2 = lax.broadcasted_iota(jnp.int32, (LANES, LANES), 0)
    l2 = lax.broadcasted_iota(jnp.int32, (LANES, LANES), 1)
    block = ((r2 >= HEAD) == (l2 >= HEAD)).astype(F32)
    eye = (r2 == l2).astype(F32)
    zeros = jnp.zeros((cs, LANES), BF16)

    def bd(x):
        k = x.shape[1] // LANES
        return jnp.concatenate([x * jnp.concatenate([m_a] * k, axis=1), x * jnp.concatenate([m_b] * k, axis=1)],
                               axis=0)

    def unit(d, ld, kd, b_, v, kk, r, h):
        before = (s_i < t_i) if d == 0 else (s_i > t_i)
        upto = (s_i <= t_i) if d == 0 else (s_i >= t_i)
        tri = upto.astype(F32).astype(BF16)
        lh, ll = _split(ld)
        big_l = jnp.dot(tri, jnp.concatenate([lh, ll], axis=0), preferred_element_type=F32)
        yield
        ltot = big_l[cs - 1:cs] if d == 0 else big_l[0:1]
        kap = kk * jnp.exp(big_l - ld)
        rt = r * jnp.exp(big_l)
        einv = jnp.exp(-big_l)
        kt, bt = kd * einv, b_ * einv
        efin = jnp.exp(ltot - big_l)
        khat, bhat = kd * efin, b_ * efin
        ktb, btb, vb16 = kt.astype(BF16), bt.astype(BF16), v.astype(BF16)
        sc = _dot_nt(jnp.concatenate([kap, rt], axis=0).astype(BF16),
                     jnp.concatenate([ktb * m_a, ktb * m_b, btb * m_a, btb * m_b], axis=0))
        yield
        a_k = jnp.where(before, sc[:cs, :LANES], 0.0)
        n_p = jnp.where(before, -sc[:cs, LANES:], 0.0)
        m_k = jnp.where(upto, sc[cs:, :LANES], 0.0)
        m_nb = jnp.where(upto, -sc[cs:, LANES:], 0.0)
        x = jnp.concatenate([kap, _mm(a_k, bd(vb16))], axis=1)
        yield
        for level in range(6):
            xh, xl = _split(x)
            nb = n_p.astype(BF16)
            x = x + jnp.dot(jnp.concatenate([nb, nb], axis=1), jnp.concatenate([bd(xh), bd(xl)], axis=0),
                            preferred_element_type=F32)
            if level < 5:
                n_p = jnp.dot(nb, bd(nb), preferred_element_type=F32)
            yield
        vz = jnp.concatenate([vb16, zeros], axis=1)
        uw = jnp.concatenate([x[:, LANES:], x[:, :LANES]], axis=1).astype(BF16)
        o1 = _mm(jnp.concatenate([m_k, m_nb], axis=1), jnp.concatenate([bd(vz), bd(uw)], axis=0))
        o2 = _mm(jnp.concatenate([khat, -bhat], axis=0).T, jnp.concatenate([vz, uw], axis=0))
        yield
        y0, q = o1[:, :LANES], rt + o1[:, LANES:]
        psi = o2[:, :LANES] * block
        phi = o2[:, LANES:] * block + eye * jnp.exp(ltot)
        y = y0 + _mm(q, h)
        return y, _mm(phi, h, split_a=True, split_b=True) + psi

    dirs = ((ldf, kdf, bf, vf, kkf, rf, yf_ref), (ldb, kdb, bb, vb, kkb, rb, yb_ref))
    keys = [(d, b, p) for d in range(2) for b in range(bsz) for p in range(c // LANES)]
    lanes_of = lambda p: slice(p * LANES, (p + 1) * LANES)
    for lo in range(0, len(keys), SCAN_BATCH):
        units = {k: unit(k[0], *[ref[k[1], :, lanes_of(k[2])] for ref in dirs[k[0]][:6]], h_ref[k[1], k[0], k[2]])
                 for k in keys[lo:lo + SCAN_BATCH]}
        for (d, b, p), (y, h_new) in _run_interleaved(units).items():
            dirs[d][6][b, :, lanes_of(p)] = y
            h_ref[b, d, p] = h_new


def _rwkv_scan(ld_f, ld_b, kd_f, kd_b, b_f, b_b, v, kk, r, nct_c):
    bsz, n, c = v.shape
    nc = n // CHUNK

    def rev(g):
        return jnp.where(g < nct_c, nct_c - 1 - g, nc - 1 - g + nct_c)

    fwd = pl.BlockSpec((bsz, CHUNK, c), lambda g: (0, g, 0))
    bwd = pl.BlockSpec((bsz, CHUNK, c), lambda g: (0, rev(g), 0))
    tok = jax.ShapeDtypeStruct((bsz, n, c), F32)
    return pl.pallas_call(
        _chunk_scan_kernel,
        grid=(nc,),
        in_specs=[fwd] * 6 + [bwd] * 6,
        out_specs=[fwd, bwd],
        out_shape=[tok, tok],
        scratch_shapes=[pltpu.VMEM((bsz, 2, c // LANES, LANES, LANES), F32)],
        compiler_params=_params("arbitrary"),
        name="rwkv_scan",
    )(ld_f, kd_f, b_f, v, kk, r, ld_b, kd_b, b_b, v, kk, r)


def _readout(yf, yb, r, kdf, kdb, v, gd, gnw, gnb, rk, gup, ones_ref):
    ones = ones_ref[...]
    inv = 1.0 / HEAD
    y = yf[0] + yb[0]
    yc = y - _mm(y, ones, split_a=True) * inv
    var = _mm(yc * yc, ones, split_a=True) * inv
    yn = yc * lax.rsqrt(var + GN_EPS) * gnw[...] + gnb[...]
    bonus = _mm(r[0] * rk[...] * (kdf[0] + kdb[0]), ones, split_a=True) * v[0]
    return (yn + bonus) * _mm(jax.nn.sigmoid(gd[0]), gup[...])


def _pool_kernel(nct, len_ctx, len_lat, main, prev8, next8, pw_ref, scale_ref, o_ref):
    i = pl.program_id(1)
    is_lat = i >= nct
    seq_len = jnp.where(is_lat, len_lat, len_ctx)
    t0 = jnp.where(is_lat, i - nct, i) * TM
    n = TM + 2 * SUBLANES
    pm = main[0]
    ext = jnp.concatenate([prev8[0], pm, next8[0]], axis=0)
    pos = t0 - SUBLANES + lax.broadcasted_iota(jnp.int32, (n, 1), 0)
    e = jnp.where(jnp.logical_and(pos >= 0, pos < seq_len), ext, 0.0)
    a2 = e + pltpu.roll(e, 1, 0)
    a4 = pltpu.roll(a2, 1, 0) + pltpu.roll(a2, n - 1, 0)
    a8 = pltpu.roll(a4, 2, 0) + pltpu.roll(a4, n - 2, 0)
    a16 = pltpu.roll(a8, 4, 0) + pltpu.roll(a8, n - 4, 0)
    t = t0 + lax.broadcasted_iota(jnp.int32, (TM, 1), 0)
    lane = lax.broadcasted_iota(jnp.int32, pm.shape, 1)
    mean = None
    for g, (w, acc) in reversed(list(enumerate(zip(POOL_WINDOWS, (a2, a4, a8, a16))))):
        lo = jnp.maximum(t - w // 2, 0)
        hi = jnp.minimum(t + (w - w // 2) - 1, seq_len - 1)
        m = acc[SUBLANES:SUBLANES + TM] / (hi - lo + 1).astype(F32)
        mean = m if mean is None else jnp.where(lane < (g + 1) * POOL_GROUP, m, mean)
    o_ref[0] = _mm(mean - pm, pw_ref[...], split_a=True, split_b=True) * scale_ref[...]


def _qk_norm_kernel(q_ref, k_ref, v_ref, qg_ref, kg_ref, ones_ref, qo, ko, vo):
    ones = ones_ref[...]
    inv = 1.0 / HEAD
    q, k = q_ref[0], k_ref[0]
    qn = q * lax.rsqrt(_mm(q * q, ones, split_a=True) * inv + NORM_EPS) * qg_ref[...]
    kn = k * lax.rsqrt(_mm(k * k, ones, split_a=True) * inv + NORM_EPS) * kg_ref[...]
    for o_ref, val in ((qo, qn * HEAD ** -0.5), (ko, kn), (vo, v_ref[0])):
        val = val.astype(BF16)
        for h in range(o_ref.shape[1]):
            o_ref[0, h] = val[:, h * HEAD:(h + 1) * HEAD]


def _run_interleaved(units):
    done = {}
    while units:
        for k in list(units):
            try:
                next(units[k])
            except StopIteration as stop:
                done[k] = stop.value
                del units[k]
    return done


def _attend(q, key_sets, bias):
    scores = [_dot_nt(q, k) for k, _ in key_sets]
    yield
    if bias is not None:
        scores[0] = scores[0] + bias
    m = functools.reduce(jnp.maximum, [jnp.max(s, axis=-1, keepdims=True) for s in scores])
    yield
    ps = [jnp.exp(s - m) for s in scores]
    den = functools.reduce(jnp.add, [jnp.sum(p, axis=-1, keepdims=True) for p in ps])
    num = functools.reduce(jnp.add, [jnp.dot(p.astype(BF16), v, preferred_element_type=F32)
                                      for p, (_, v) in zip(ps, key_sets)])
    yield
    return num / den


def _nat_kernel(rows, q_ref, kp, kc, kn, vp, vc, vn, kx_ref, vx_ref, bias_ref, o_ref, ks, vs):
    i = pl.program_id(1) - 1
    heads, tq = q_ref.shape[1], q_ref.shape[2]
    rb = tq // GRID_W
    nloc = WIN_H * GRID_W

    @pl.when(i < 0)
    def _():
        units = {h: _attend(q_ref[0, h], [(kx_ref[0, h], vx_ref[0, h])], None) for h in range(heads)}
        for h, o in _run_interleaved(units).items():
            o_ref[0, :, h * HEAD:(h + 1) * HEAD] = o

    @pl.when(i >= 0)
    def _():
        for j, (kr, vr) in enumerate(((kp, vp), (kc, vc), (kn, vn))):
            ks[:, j * tq:(j + 1) * tq, :] = kr[0]
            vs[:, j * tq:(j + 1) * tq, :] = vr[0]
        for r0 in range(0, rb, NAT_ROWS):
            units = {}
            for rr in range(r0, r0 + NAT_ROWS):
                r = i * rb + rr
                rs = jnp.clip(r - WIN_H // 2, 0, rows - WIN_H)
                off = r - rs
                start = pl.multiple_of((rs - i * rb + rb) * GRID_W, GRID_W)
                qs = slice(rr * GRID_W, (rr + 1) * GRID_W)
                for h in range(heads):
                    units[rr, h] = _attend(q_ref[0, h, qs],
                                           [(ks[h, pl.ds(start, nloc)], vs[h, pl.ds(start, nloc)]),
                                            (kx_ref[0, h], vx_ref[0, h])], bias_ref[h, off])
            for (rr, h), o in _run_interleaved(units).items():
                o_ref[0, rr * GRID_W:(rr + 1) * GRID_W, h * HEAD:(h + 1) * HEAD] = o


def _nat_attention(qn, kn, vn, bias, len_ctx, len_lat):
    bsz, heads, n, hd = qn.shape
    tq = len_ctx
    assert tq == (WIN_H // 2) * GRID_W and len_lat % tq == 0
    rows = len_lat // GRID_W
    nblk = len_lat // tq

    def blk(shift):
        return pl.BlockSpec((1, heads, tq, hd),
                            lambda b, i: (b, 0, jnp.where(i == 0, 0, 1 + jnp.clip(i - 1 + shift, 0, nblk - 1)), 0))

    ctx = pl.BlockSpec((1, heads, tq, hd), lambda b, i: (b, 0, 0, 0))
    return pl.pallas_call(
        functools.partial(_nat_kernel, rows),
        grid=(bsz, 1 + nblk),
        in_specs=[blk(0), blk(-1), blk(0), blk(1), blk(-1), blk(0), blk(1), ctx, ctx, _const_spec(bias.shape)],
        out_specs=pl.BlockSpec((1, tq, heads * hd), lambda b, i: (b, i, 0)),
        out_shape=jax.ShapeDtypeStruct((bsz, n, heads * hd), F32),
        scratch_shapes=[pltpu.VMEM((heads, 3 * tq, hd), BF16), pltpu.VMEM((heads, 3 * tq, hd), BF16)],
        compiler_params=_params("parallel", "parallel"),
        name="nat_attention",
    )(qn, kn, kn, kn, vn, vn, vn, kn, vn, bias)


def _nat_bias_table(rpb):
    qc = np.arange(GRID_W)[:, None]
    kc = np.arange(GRID_W)[None, :]
    cs = np.clip(qc - WIN_W // 2, 0, GRID_W - WIN_W)
    valid = (kc >= cs) & (kc < cs + WIN_W)
    dc = kc - qc + WIN_W - 1
    pick = (np.arange(2 * WIN_W - 1)[:, None, None] == dc[None]) & valid[None]
    cols = jnp.einsum("hdm,mqk->hdqk", rpb, jnp.asarray(pick, F32), precision=lax.Precision.HIGHEST)
    cols = jnp.where(valid[None, None], cols, MASK_BIAS)
    t = jnp.stack([cols[:, WIN_H - 1 - off:2 * WIN_H - 1 - off] for off in range(WIN_H)], axis=1)
    return t.transpose(0, 1, 3, 2, 4).reshape(rpb.shape[0], WIN_H, GRID_W, WIN_H * GRID_W)


def _merge_kernel(*refs):
    a = _readout(*refs[:12])
    bp, cn, ga, gb, gc, x, g1, wa, wb, wc, wo, o_ref = refs[12:]
    sig = lambda g: jax.nn.sigmoid(g[0].astype(F32))
    m = sig(ga) * _dot_bf(a, wa[...]) + sig(gb) * _dot_bf(bp[0], wb[...]) + sig(gc) * _dot_bf(cn[0], wc[...])
    o_ref[0] = x[0] + g1[0] * _dot_bf(m, wo[...])


def _merge(rwkv, bp, cn, gates, x, mod, lw, nct):
    bsz, n, d = x.shape
    tok = lambda w, col=0: pl.BlockSpec((1, TM, w), lambda b, i: (b, i, col))
    consts = [lw["gn_w"], lw["gn_b"], lw["r_k"], lw["gate_up"], lw["ones_bd"]]
    ws = [lw["w_rwkv_o"], lw["w_pool_o"], lw["w_nat_o"], lw["w_out"]]
    return pl.pallas_call(
        _merge_kernel,
        grid=(bsz, n // TM),
        in_specs=[tok(a.shape[2]) for a in rwkv] + [_const_spec(a.shape) for a in consts]
        + [tok(bp.shape[2]), tok(cn.shape[2]), tok(d, 0), tok(d, 1), tok(d, 2), tok(d),
           pl.BlockSpec((1, 1, d), _mod_spec(nct, 2))] + [_const_spec(w.shape) for w in ws],
        out_specs=tok(d),
        out_shape=jax.ShapeDtypeStruct((bsz, n, d), F32),
        compiler_params=_params("parallel", "parallel"),
        name="merge",
    )(*rwkv, *consts, bp, cn, gates, gates, gates, x, mod, *ws)


def _ffn_kernel(x_ref, sh, sc, g2, gain, w1, w2, o_ref):
    x = x_ref[0]
    h = _norm_mod(x, gain[...], sh[0], sc[0])
    u = jnp.dot(h.astype(BF16), w1[...], preferred_element_type=F32)
    hid = w2.shape[0]
    gate, up = u[:, :hid], u[:, hid:]
    act = gate * jax.nn.sigmoid(gate) * up
    o_ref[0] = x + g2[0] * jnp.dot(act.astype(BF16), w2[...], preferred_element_type=F32)


def _ffn(x, mod, lw, nct):
    bsz, n, d = x.shape
    tok = pl.BlockSpec((1, TM, d), lambda b, i: (b, i, 0))
    mspec = lambda k: pl.BlockSpec((1, 1, d), _mod_spec(nct, k))
    return pl.pallas_call(
        _ffn_kernel,
        grid=(bsz, n // TM),
        in_specs=[tok, mspec(3), mspec(4), mspec(5), _const_spec(lw["norm2"].shape),
                  _const_spec(lw["w_ffn_in"].shape), _const_spec(lw["w_ffn_out"].shape)],
        out_specs=tok,
        out_shape=jax.ShapeDtypeStruct((bsz, n, d), F32),
        compiler_params=_params("parallel", "parallel"),
        name="ffn",
    )(x, mod, mod, mod, lw["norm2"], lw["w_ffn_in"], lw["w_ffn_out"])


def _block_diag(blocks):
    n = len(blocks)
    rows = []
    for i, blk in enumerate(blocks):
        rows.append(jnp.concatenate([blk if j == i else jnp.zeros((blk.shape[0], blocks[j].shape[1]), blk.dtype)
                                     for j in range(n)], axis=1))
    return jnp.concatenate(rows, axis=0)


def _pad_cols(a, width):
    return jnp.pad(a, ((0, 0), (0, width - a.shape[1])))


def _layer_weights(l, prm):
    c = prm["k_k"].shape[1]
    lora = prm["decay_up"].shape[2]
    gl = prm["gate_up"].shape[1]
    pool = prm["pool_scale"].shape[1]
    d = prm["w_out"].shape[1]
    w_in = prm["w_in"][l]
    o_lora, o_gd, o_pool = 3 * c, 3 * c + 4 * lora, 3 * c + 4 * lora + gl
    o_q = o_pool + pool
    o_gate = o_q + 3 * c
    assert 4 * lora == 2 * LANES and gl <= 2 * LANES and pool == 2 * LANES and o_gate + 3 * d == w_in.shape[1]
    w_mix = jnp.concatenate([
        w_in[:, 0:o_lora], w_in[:, o_q:o_gate], w_in[:, o_lora:o_gd],
        _pad_cols(w_in[:, o_gd:o_pool], 2 * LANES), w_in[:, o_pool:o_q]], axis=1).astype(BF16)
    mu = lambda m: _pad_cols(m[l][None, :o_pool], o_pool + 2 * LANES - gl)
    heads = c // HEAD
    row = lambda a: a.reshape(1, -1)
    return {
        "w_mix": w_mix, "w_gate": w_in[:, o_gate:].astype(BF16),
        "norm1": row(prm["norm1"][l]), "norm2": row(prm["norm2"][l]),
        "mu_prev": mu(prm["mu_prev"]), "mu_next": mu(prm["mu_next"]),
        "decay_up": _block_diag([prm["decay_up"][l, 0], prm["decay_up"][l, 1]]),
        "decay_w0": row(prm["decay_w0"][l]),
        "iclr_up": _block_diag([prm["iclr_up"][l, 0], prm["iclr_up"][l, 1]]),
        "iclr_a0": row(prm["iclr_a0"][l]),
        "k_k": row(prm["k_k"][l]), "k_a": row(prm["k_a"][l]), "r_k": row(prm["r_k"][l]),
        "gn_w": row(prm["gn_w"][l]), "gn_b": row(prm["gn_b"][l]),
        "gate_up": jnp.pad(prm["gate_up"][l], ((0, 2 * LANES - gl), (0, 0))),
        "ones_bd": jnp.kron(jnp.eye(heads, dtype=F32), jnp.ones((HEAD, HEAD), F32)).astype(BF16),
        "pool_w": _block_diag([prm["pool_w"][l, g] for g in range(len(POOL_WINDOWS))]),
        "pool_scale": row(prm["pool_scale"][l]),
        "q_gain": row(jnp.tile(prm["q_gain"][l], heads)), "k_gain": row(jnp.tile(prm["k_gain"][l], heads)),
        "nat_bias": _nat_bias_table(prm["rpb"][l]),
        "w_rwkv_o": prm["w_rwkv_o"][l].astype(BF16), "w_pool_o": prm["w_pool_o"][l].astype(BF16),
        "w_nat_o": prm["w_nat_o"][l].astype(BF16), "w_out": prm["w_out"][l].astype(BF16),
        "w_ffn_in": prm["w_ffn_in"][l].astype(BF16), "w_ffn_out": prm["w_ffn_out"][l].astype(BF16),
    }


def kernel(x, c, ctx, c_ctx, w_mod, b_mod, norm1, norm2, w_in, mu_prev, mu_next, decay_w0, decay_up, iclr_a0, iclr_up, gate_up, k_k, k_a, r_k, gn_w, gn_b, pool_w, pool_scale, q_gain, k_gain, rpb, w_rwkv_o, w_pool_o, w_nat_o, w_out, w_ffn_in, w_ffn_out):
    prm = dict(norm1=norm1, norm2=norm2, w_in=w_in, mu_prev=mu_prev, mu_next=mu_next, decay_w0=decay_w0,
               decay_up=decay_up, iclr_a0=iclr_a0, iclr_up=iclr_up, gate_up=gate_up, k_k=k_k, k_a=k_a, r_k=r_k,
               gn_w=gn_w, gn_b=gn_b, pool_w=pool_w, pool_scale=pool_scale, q_gain=q_gain, k_gain=k_gain, rpb=rpb,
               w_rwkv_o=w_rwkv_o, w_pool_o=w_pool_o, w_nat_o=w_nat_o, w_out=w_out, w_ffn_in=w_ffn_in,
               w_ffn_out=w_ffn_out)
    bsz, len_lat, d = x.shape
    len_ctx = ctx.shape[1]
    depth = w_mod.shape[0]
    assert len_ctx % TM == 0 and len_lat % TM == 0 and bsz + 1 <= SUBLANES
    nct = len_ctx // TM

    s_rows = jnp.concatenate([c, c_ctx[None, :], jnp.zeros((SUBLANES - bsz - 1, d), F32)], axis=0)
    mod_all = _modulation(s_rows, w_mod, b_mod)
    xa = jnp.concatenate([ctx, x], axis=1)

    for l in range(depth):
        lw = _layer_weights(l, prm)
        m_lat = mod_all[l, :bsz]
        m_ctx = jnp.broadcast_to(mod_all[l, bsz][None], m_lat.shape)
        mod = jnp.stack([m_ctx, m_lat], axis=1).reshape(bsz * 2 * 6, 1, d)

        p = _in_proj(xa, mod, lw["norm1"], lw["w_mix"], len_ctx, F32)
        gates = _in_proj(xa, mod, lw["norm1"], lw["w_gate"], len_ctx, BF16)
        ld_f, ld_b, kd_f, kd_b, b_f, b_b, v, kk, r, gd, b_br, qn, kn, vn = _mixer_prep(p, lw, nct, len_ctx, len_lat)
        y_f, y_b = _rwkv_scan(ld_f, ld_b, kd_f, kd_b, b_f, b_b, v, kk, r, len_ctx // CHUNK)
        c_br = _nat_attention(qn, kn, vn, lw["nat_bias"], len_ctx, len_lat)
        xa = _merge((y_f, y_b, r, kd_f, kd_b, v, gd), b_br, c_br, gates, xa, mod, lw, nct)
        xa = _ffn(xa, mod, lw, nct)
    return xa[:, len_ctx:]
```

```python
import functools

import numpy as np
import jax
import jax.numpy as jnp
from jax import lax
from jax.experimental import pallas as pl
from jax.experimental.pallas import tpu as pltpu

F32 = jnp.float32
BF16 = jnp.bfloat16

HEAD = 64
NORM_EPS = 1e-6
GN_EPS = 64e-5
KK_EPS = 1e-24
POOL_WINDOWS = (2, 4, 8, 16)
POOL_GROUP = 64
GRID_W = 64
WIN_H = 8
WIN_W = 16
MASK_BIAS = -1e30

LANES = 128
SUBLANES = 8
VMEM_LIMIT = 56 * 1024 * 1024

TM = 256
NAT_ROWS = 4
SCAN_BATCH = 12
CHUNK = 64
TM_IN = 640
TM_ROW = 640
TN_IN = 3072
TN_MOD = 1536


def _dot_hi(a, b):
    return jnp.dot(a, b, precision=lax.Precision.HIGHEST, preferred_element_type=F32)


def _dot_bf(a, b):
    return jnp.dot(a.astype(BF16), b.astype(BF16), preferred_element_type=F32)


def _dot_nt(a, b):
    return lax.dot_general(a, b, (((1,), (1,)), ((), ())), preferred_element_type=F32)


def _params(*sem):
    return pltpu.CompilerParams(dimension_semantics=sem, vmem_limit_bytes=VMEM_LIMIT)


def _const_spec(shape):
    nd = len(shape)
    return pl.BlockSpec(shape, lambda *_: (0,) * nd, pipeline_mode=pl.Buffered(1))


def _mod_kernel(s_ref, w_ref, b_ref, o_ref):
    s = s_ref[...]
    s = s * jax.nn.sigmoid(s)
    o_ref[0] = _dot_hi(s, w_ref[0]) + b_ref[0]


def _modulation(s_rows, w_mod, b_mod):
    depth, d, n = w_mod.shape
    return pl.pallas_call(
        _mod_kernel,
        grid=(depth, n // TN_MOD),
        in_specs=[
            pl.BlockSpec((SUBLANES, d), lambda l, j: (0, 0)),
            pl.BlockSpec((1, d, TN_MOD), lambda l, j: (l, 0, j)),
            pl.BlockSpec((1, 1, TN_MOD), lambda l, j: (l, 0, j)),
        ],
        out_specs=pl.BlockSpec((1, SUBLANES, TN_MOD), lambda l, j: (l, 0, j)),
        out_shape=jax.ShapeDtypeStruct((depth, SUBLANES, n), F32),
        compiler_params=_params("parallel", "parallel"),
        name="modulation",
    )(s_rows, w_mod, b_mod.reshape(depth, 1, n))


def _norm_mod(x, gain, shift, scale):
    ms = jnp.mean(x * x, axis=-1, keepdims=True)
    return x * lax.rsqrt(ms + NORM_EPS) * gain * (1.0 + scale) + shift


def _in_proj_kernel(len_ctx, x_ref, sh_c, sc_c, sh_l, sc_l, g_ref, w_ref, o_ref):
    tm = x_ref.shape[1]
    h = _norm_mod(x_ref[0], g_ref[...], _row_mod(len_ctx, tm, sh_c, sh_l, axis=2), _row_mod(len_ctx, tm, sc_c, sc_l, axis=2))
    o_ref[0] = jnp.dot(h.astype(BF16), w_ref[...], preferred_element_type=F32).astype(o_ref.dtype)


def _in_proj(x, mod, gain, w, len_ctx, out_dtype):
    bsz, n, d = x.shape
    nout = w.shape[1]
    assert n % TM_IN == 0 and nout % TN_IN == 0
    mspec = lambda is_lat, k: pl.BlockSpec((1, 1, d), lambda j, b, i: ((b * 2 + is_lat) * 6 + k, 0, 0))
    return pl.pallas_call(
        functools.partial(_in_proj_kernel, len_ctx),
        grid=(nout // TN_IN, bsz, n // TM_IN),
        in_specs=[
            pl.BlockSpec((1, TM_IN, d), lambda j, b, i: (b, i, 0)),
            mspec(0, 0), mspec(0, 1), mspec(1, 0), mspec(1, 1),
            pl.BlockSpec((1, d), lambda j, b, i: (0, 0)),
            pl.BlockSpec((d, TN_IN), lambda j, b, i: (0, j)),
        ],
        out_specs=pl.BlockSpec((1, TM_IN, TN_IN), lambda j, b, i: (b, i, j)),
        out_shape=jax.ShapeDtypeStruct((bsz, n, nout), out_dtype),
        compiler_params=_params("parallel", "parallel", "parallel"),
        name="in_proj",
    )(x, mod, mod, mod, mod, gain, w)


def _halo_specs(width, col, n):
    nb = n // SUBLANES
    per = TM // SUBLANES
    return [
        pl.BlockSpec((1, TM, width), lambda b, i: (b, i, col)),
        pl.BlockSpec((1, SUBLANES, width), lambda b, i: (b, jnp.maximum(i * per - 1, 0), col)),
        pl.BlockSpec((1, SUBLANES, width), lambda b, i: (b, jnp.minimum((i + 1) * per, nb - 1), col)),
    ]


def _prep_kernel(nct, nt, rkv_ref, rkv_p, rkv_n, lo_ref, lo_p, lo_n, gd_ref, gd_p, gd_n,
                 mup_ref, mun_ref, du_ref, w0_ref, au_ref, a0_ref, kkw_ref, ka_ref, ones_ref,
                 wf_ref, wb_ref, kdf_ref, kdb_ref, bf_ref, bb_ref, v_ref, kk_ref, r_ref, gdo_ref):
    i = pl.program_id(1)
    first = jnp.logical_or(i == 0, i == nct)
    last = jnp.logical_or(i == nct - 1, i == nt - 1)
    row = lax.broadcasted_iota(jnp.int32, (TM, 1), 0)
    c = rkv_ref.shape[2] // 3

    def mix(main, prev8, next8, lo, hi):
        pm = main[0]
        prow = jnp.where(first, 0.0, prev8[0, SUBLANES - 1:SUBLANES, :])
        nrow = jnp.where(last, 0.0, next8[0, 0:1, :])
        prev = jnp.where(row == 0, prow, pltpu.roll(pm, 1, 0))
        nxt = jnp.where(row == TM - 1, nrow, pltpu.roll(pm, TM - 1, 0))
        return pm + mup_ref[:, lo:hi] * (prev - pm) + mun_ref[:, lo:hi] * (nxt - pm)

    rkv = mix(rkv_ref, rkv_p, rkv_n, 0, 3 * c)
    lora = mix(lo_ref, lo_p, lo_n, 3 * c, 3 * c + 2 * LANES)
    gdo_ref[0] = mix(gd_ref, gd_p, gd_n, 3 * c + 2 * LANES, 3 * c + 4 * LANES)

    r, k, v = rkv[:, 0:c], rkv[:, c:2 * c], rkv[:, 2 * c:3 * c]
    wd, ad = lora[:, 0:LANES], lora[:, LANES:2 * LANES]
    dec = _mm(jnp.tanh(wd), du_ref[...], split_a=True, split_b=True) + w0_ref[...]
    log_decay = -float(np.exp(-0.5)) * jax.nn.sigmoid(dec)
    a = jax.nn.sigmoid(_mm(ad, au_ref[...], split_a=True, split_b=True) + a0_ref[...])
    kk = k * kkw_ref[...]
    kk = kk * lax.rsqrt(jnp.maximum(_mm(kk * kk, ones_ref[...], split_a=True), KK_EPS))
    ka = ka_ref[...]
    for z, (w_o, kd_o, b_o) in enumerate(((wf_ref, kdf_ref, bf_ref), (wb_ref, kdb_ref, bb_ref))):
        az = a[:, z * c:(z + 1) * c]
        w_o[0] = log_decay[:, z * c:(z + 1) * c]
        kd_o[0] = k * (1.0 + (az - 1.0) * ka)
        b_o[0] = kk * az
    v_ref[0] = v
    kk_ref[0] = kk
    r_ref[0] = r


def _mixer_prep_kernel(nct, nt, len_ctx, len_lat, *refs):
    n_in = (18, 5, 6)
    n_out = (10, 1, 3)
    cuts = np.cumsum(n_in + n_out).tolist()
    prep_in, pool_in, qk_in, prep_out, pool_out, qk_out = [refs[a:b] for a, b in zip([0] + cuts[:-1], cuts)]
    _prep_kernel(nct, nt, *prep_in, *prep_out)
    _pool_kernel(nct, len_ctx, len_lat, *pool_in, *pool_out)
    _qk_norm_kernel(*qk_in, *qk_out)


def _mixer_prep(p, lw, nct, len_ctx, len_lat):
    bsz, n, _ = p.shape
    c = lw["k_k"].shape[1]
    nt = n // TM
    wide = 2 * LANES
    lora_col = (6 * c) // wide
    prep_consts = [lw["mu_prev"], lw["mu_next"], lw["decay_up"], lw["decay_w0"], lw["iclr_up"], lw["iclr_a0"],
                   lw["k_k"], lw["k_a"], lw["ones_bd"]]
    pool_consts = [lw["pool_w"], lw["pool_scale"]]
    qk_consts = [lw["q_gain"], lw["k_gain"], lw["ones_bd"]]
    consts = lambda arrs: [_const_spec(a.shape) for a in arrs]
    tok = lambda w: pl.BlockSpec((1, TM, w), lambda b, i: (b, i, 0))
    heads = c // HEAD
    head_major = pl.BlockSpec((1, heads, TM, HEAD), lambda b, i: (b, 0, i, 0))
    return pl.pallas_call(
        functools.partial(_mixer_prep_kernel, nct, nt, len_ctx, len_lat),
        grid=(bsz, nt),
        in_specs=_halo_specs(3 * c, 0, n) + _halo_specs(wide, lora_col, n) + _halo_specs(wide, lora_col + 1, n)
        + consts(prep_consts) + _halo_specs(wide, lora_col + 2, n) + consts(pool_consts)
        + [pl.BlockSpec((1, TM, c), lambda b, i, j=j: (b, i, 3 + j)) for j in range(3)] + consts(qk_consts),
        out_specs=[tok(c)] * 9 + [tok(wide), tok(wide)] + [head_major] * 3,
        out_shape=[jax.ShapeDtypeStruct((bsz, n, c), F32)] * 9 + [jax.ShapeDtypeStruct((bsz, n, wide), F32)] * 2
        + [jax.ShapeDtypeStruct((bsz, heads, n, HEAD), BF16)] * 3,
        compiler_params=_params("parallel", "parallel"),
        name="mixer_prep",
    )(*[p] * 9, *prep_consts, *[p] * 3, *pool_consts, *[p] * 3, *qk_consts)


def _split(a):
    hi = a.astype(BF16)
    return hi, (a - hi.astype(F32)).astype(BF16)


def _mm(a, b, split_a=False, split_b=False):
    def halves(x, split):
        if x.dtype == BF16 or not split:
            return x.astype(BF16), None
        return _split(x)

    (ah, al), (bh, bl) = halves(a, split_a), halves(b, split_b)
    lhs, rhs = [ah], [bh]
    if al is not None:
        lhs.append(al)
        rhs.append(bh)
    if bl is not None:
        lhs.append(ah)
        rhs.append(bl)
    if len(lhs) == 1:
        return jnp.dot(ah, bh, preferred_element_type=F32)
    return jnp.dot(jnp.concatenate(lhs, axis=1), jnp.concatenate(rhs, axis=0), preferred_element_type=F32)


def _chunk_scan_kernel(ldf, kdf, bf, vf, kkf, rf, ldb, kdb, bb, vb, kkb, rb, yf_ref, yb_ref, h_ref):
    @pl.when(pl.program_id(0) == 0)
    def _():
        h_ref[...] = jnp.zeros_like(h_ref)

    bsz, cs, c = ldf.shape
    t_i = lax.broadcasted_iota(jnp.int32, (cs, LANES), 0)
    lane = lax.broadcasted_iota(jnp.int32, (cs, LANES), 1)
    s_i = jnp.bitwise_and(lane, HEAD - 1)
    m_a = (lane < HEAD).astype(F32).astype(BF16)
    m_b = (lane >= HEAD).astype(F32).astype(BF16)
    r2 = lax.broadcasted_iota(jnp.int32, (LANES, LANES), 0)
    l2 = lax.broadcasted_iota(jnp.int32, (LANES, LANES), 1)
    block = ((r2 >= HEAD) == (l2 >= HEAD)).astype(F32)
    eye = (r2 == l2).astype(F32)
    zeros = jnp.zeros((cs, LANES), BF16)

    def bd(x):
        k = x.shape[1] // LANES
        return jnp.concatenate([x * jnp.concatenate([m_a] * k, axis=1), x * jnp.concatenate([m_b] * k, axis=1)],
                               axis=0)

    def unit(d, ld, kd, b_, v, kk, r, h):
        before = (s_i < t_i) if d == 0 else (s_i > t_i)
        upto = (s_i <= t_i) if d == 0 else (s_i >= t_i)
        tri = upto.astype(F32).astype(BF16)
        lh, ll = _split(ld)
        big_l = jnp.dot(tri, jnp.concatenate([lh, ll], axis=0), preferred_element_type=F32)
        yield
        ltot = big_l[cs - 1:cs] if d == 0 else big_l[0:1]
        kap = kk * jnp.exp(big_l - ld)
        rt = r * jnp.exp(big_l)
        einv = jnp.exp(-big_l)
        kt, bt = kd * einv, b_ * einv
        efin = jnp.exp(ltot - big_l)
        khat, bhat = kd * efin, b_ * efin
        ktb, btb, vb16 = kt.astype(BF16), bt.astype(BF16), v.astype(BF16)
        sc = _dot_nt(jnp.concatenate([kap, rt], axis=0).astype(BF16),
                     jnp.concatenate([ktb * m_a, ktb * m_b, btb * m_a, btb * m_b], axis=0))
        yield
        a_k = jnp.where(before, sc[:cs, :LANES], 0.0)
        n_p = jnp.where(before, -sc[:cs, LANES:], 0.0)
        m_k = jnp.where(upto, sc[cs:, :LANES], 0.0)
        m_nb = jnp.where(upto, -sc[cs:, LANES:], 0.0)
        x = jnp.concatenate([kap, _mm(a_k, bd(vb16))], axis=1)
        yield
        for level in range(6):
            xh, xl = _split(x)
            nb = n_p.astype(BF16)
            x = x + jnp.dot(jnp.concatenate([nb, nb], axis=1), jnp.concatenate([bd(xh), bd(xl)], axis=0),
                            preferred_element_type=F32)
            if level < 5:
                n_p = jnp.dot(nb, bd(nb), preferred_element_type=F32)
            yield
        vz = jnp.concatenate([vb16, zeros], axis=1)
        uw = jnp.concatenate([x[:, LANES:], x[:, :LANES]], axis=1).astype(BF16)
        o1 = _mm(jnp.concatenate([m_k, m_nb], axis=1), jnp.concatenate([bd(vz), bd(uw)], axis=0))
        o2 = _mm(jnp.concatenate([khat, -bhat], axis=0).T, jnp.concatenate([vz, uw], axis=0))
        yield
        y0, q = o1[:, :LANES], rt + o1[:, LANES:]
        psi = o2[:, :LANES] * block
        phi = o2[:, LANES:] * block + eye * jnp.exp(ltot)
        y = y0 + _mm(q, h)
        return y, _mm(phi, h, split_a=True, split_b=True) + psi

    dirs = ((ldf, kdf, bf, vf, kkf, rf, yf_ref), (ldb, kdb, bb, vb, kkb, rb, yb_ref))
    keys = [(d, b, p) for d in range(2) for b in range(bsz) for p in range(c // LANES)]
    lanes_of = lambda p: slice(p * LANES, (p + 1) * LANES)
    for lo in range(0, len(keys), SCAN_BATCH):
        units = {k: unit(k[0], *[ref[k[1], :, lanes_of(k[2])] for ref in dirs[k[0]][:6]], h_ref[k[1], k[0], k[2]])
                 for k in keys[lo:lo + SCAN_BATCH]}
        for (d, b, p), (y, h_new) in _run_interleaved(units).items():
            dirs[d][6][b, :, lanes_of(p)] = y
            h_ref[b, d, p] = h_new


def _rwkv_scan(ld_f, ld_b, kd_f, kd_b, b_f, b_b, v, kk, r, nct_c):
    bsz, n, c = v.shape
    nc = n // CHUNK

    def rev(g):
        return jnp.where(g < nct_c, nct_c - 1 - g, nc - 1 - g + nct_c)

    fwd = pl.BlockSpec((bsz, CHUNK, c), lambda g: (0, g, 0))
    bwd = pl.BlockSpec((bsz, CHUNK, c), lambda g: (0, rev(g), 0))
    tok = jax.ShapeDtypeStruct((bsz, n, c), F32)
    return pl.pallas_call(
        _chunk_scan_kernel,
        grid=(nc,),
        in_specs=[fwd] * 6 + [bwd] * 6,
        out_specs=[fwd, bwd],
        out_shape=[tok, tok],
        scratch_shapes=[pltpu.VMEM((bsz, 2, c // LANES, LANES, LANES), F32)],
        compiler_params=_params("arbitrary"),
        name="rwkv_scan",
    )(ld_f, kd_f, b_f, v, kk, r, ld_b, kd_b, b_b, v, kk, r)


def _readout(yf, yb, r, kdf, kdb, v, gd, gnw, gnb, rk, gup, ones_ref):
    ones = ones_ref[...]
    inv = 1.0 / HEAD
    y = yf[0] + yb[0]
    yc = y - _mm(y, ones, split_a=True) * inv
    var = _mm(yc * yc, ones, split_a=True) * inv
    yn = yc * lax.rsqrt(var + GN_EPS) * gnw[...] + gnb[...]
    bonus = _mm(r[0] * rk[...] * (kdf[0] + kdb[0]), ones, split_a=True) * v[0]
    return (yn + bonus) * _mm(jax.nn.sigmoid(gd[0]), gup[...])


def _pool_kernel(nct, len_ctx, len_lat, main, prev8, next8, pw_ref, scale_ref, o_ref):
    i = pl.program_id(1)
    is_lat = i >= nct
    seq_len = jnp.where(is_lat, len_lat, len_ctx)
    t0 = jnp.where(is_lat, i - nct, i) * TM
    n = TM + 2 * SUBLANES
    pm = main[0]
    ext = jnp.concatenate([prev8[0], pm, next8[0]], axis=0)
    pos = t0 - SUBLANES + lax.broadcasted_iota(jnp.int32, (n, 1), 0)
    e = jnp.where(jnp.logical_and(pos >= 0, pos < seq_len), ext, 0.0)
    a2 = e + pltpu.roll(e, 1, 0)
    a4 = pltpu.roll(a2, 1, 0) + pltpu.roll(a2, n - 1, 0)
    a8 = pltpu.roll(a4, 2, 0) + pltpu.roll(a4, n - 2, 0)
    a16 = pltpu.roll(a8, 4, 0) + pltpu.roll(a8, n - 4, 0)
    t = t0 + lax.broadcasted_iota(jnp.int32, (TM, 1), 0)
    lane = lax.broadcasted_iota(jnp.int32, pm.shape, 1)
    mean = None
    for g, (w, acc) in reversed(list(enumerate(zip(POOL_WINDOWS, (a2, a4, a8, a16))))):
        lo = jnp.maximum(t - w // 2, 0)
        hi = jnp.minimum(t + (w - w // 2) - 1, seq_len - 1)
        m = acc[SUBLANES:SUBLANES + TM] / (hi - lo + 1).astype(F32)
        mean = m if mean is None else jnp.where(lane < (g + 1) * POOL_GROUP, m, mean)
    o_ref[0] = _mm(mean - pm, pw_ref[...], split_a=True, split_b=True) * scale_ref[...]


def _qk_norm_kernel(q_ref, k_ref, v_ref, qg_ref, kg_ref, ones_ref, qo, ko, vo):
    ones = ones_ref[...]
    inv = 1.0 / HEAD
    q, k = q_ref[0], k_ref[0]
    qn = q * lax.rsqrt(_mm(q * q, ones, split_a=True) * inv + NORM_EPS) * qg_ref[...]
    kn = k * lax.rsqrt(_mm(k * k, ones, split_a=True) * inv + NORM_EPS) * kg_ref[...]
    for o_ref, val in ((qo, qn * HEAD ** -0.5), (ko, kn), (vo, v_ref[0])):
        val = val.astype(BF16)
        for h in range(o_ref.shape[1]):
            o_ref[0, h] = val[:, h * HEAD:(h + 1) * HEAD]


def _run_interleaved(units):
    done = {}
    while units:
        for k in list(units):
            try:
                next(units[k])
            except StopIteration as stop:
                done[k] = stop.value
                del units[k]
    return done


def _attend(q, key_sets, bias):
    scores = [_dot_nt(q, k) for k, _ in key_sets]
    yield
    if bias is not None:
        scores[0] = scores[0] + bias
    m = functools.reduce(jnp.maximum, [jnp.max(s, axis=-1, keepdims=True) for s in scores])
    yield
    ps = [jnp.exp(s - m) for s in scores]
    den = functools.reduce(jnp.add, [jnp.sum(p, axis=-1, keepdims=True) for p in ps])
    num = functools.reduce(jnp.add, [jnp.dot(p.astype(BF16), v, preferred_element_type=F32)
                                      for p, (_, v) in zip(ps, key_sets)])
    yield
    return num / den


def _nat_kernel(rows, q_ref, kp, kc, kn, vp, vc, vn, kx_ref, vx_ref, bias_ref, o_ref, ks, vs):
    i = pl.program_id(1) - 1
    heads, tq = q_ref.shape[1], q_ref.shape[2]
    rb = tq // GRID_W
    nloc = WIN_H * GRID_W

    @pl.when(i < 0)
    def _():
        units = {h: _attend(q_ref[0, h], [(kx_ref[0, h], vx_ref[0, h])], None) for h in range(heads)}
        for h, o in _run_interleaved(units).items():
            o_ref[0, :, h * HEAD:(h + 1) * HEAD] = o

    @pl.when(i >= 0)
    def _():
        for j, (kr, vr) in enumerate(((kp, vp), (kc, vc), (kn, vn))):
            ks[:, j * tq:(j + 1) * tq, :] = kr[0]
            vs[:, j * tq:(j + 1) * tq, :] = vr[0]
        for r0 in range(0, rb, NAT_ROWS):
            units = {}
            for rr in range(r0, r0 + NAT_ROWS):
                r = i * rb + rr
                rs = jnp.clip(r - WIN_H // 2, 0, rows - WIN_H)
                off = r - rs
                start = pl.multiple_of((rs - i * rb + rb) * GRID_W, GRID_W)
                qs = slice(rr * GRID_W, (rr + 1) * GRID_W)
                for h in range(heads):
                    units[rr, h] = _attend(q_ref[0, h, qs],
                                           [(ks[h, pl.ds(start, nloc)], vs[h, pl.ds(start, nloc)]),
                                            (kx_ref[0, h], vx_ref[0, h])], bias_ref[h, off])
            for (rr, h), o in _run_interleaved(units).items():
                o_ref[0, rr * GRID_W:(rr + 1) * GRID_W, h * HEAD:(h + 1) * HEAD] = o


def _nat_attention(qn, kn, vn, bias, len_ctx, len_lat):
    bsz, heads, n, hd = qn.shape
    tq = len_ctx
    assert tq == (WIN_H // 2) * GRID_W and len_lat % tq == 0
    rows = len_lat // GRID_W
    nblk = len_lat // tq

    def blk(shift):
        return pl.BlockSpec((1, heads, tq, hd),
                            lambda b, i: (b, 0, jnp.where(i == 0, 0, 1 + jnp.clip(i - 1 + shift, 0, nblk - 1)), 0))

    ctx = pl.BlockSpec((1, heads, tq, hd), lambda b, i: (b, 0, 0, 0))
    return pl.pallas_call(
        functools.partial(_nat_kernel, rows),
        grid=(bsz, 1 + nblk),
        in_specs=[blk(0), blk(-1), blk(0), blk(1), blk(-1), blk(0), blk(1), ctx, ctx, _const_spec(bias.shape)],
        out_specs=pl.BlockSpec((1, tq, heads * hd), lambda b, i: (b, i, 0)),
        out_shape=jax.ShapeDtypeStruct((bsz, n, heads * hd), F32),
        scratch_shapes=[pltpu.VMEM((heads, 3 * tq, hd), BF16), pltpu.VMEM((heads, 3 * tq, hd), BF16)],
        compiler_params=_params("parallel", "parallel"),
        name="nat_attention",
    )(qn, kn, kn, kn, vn, vn, vn, kn, vn, bias)


def _nat_bias_table(rpb):
    qc = np.arange(GRID_W)[:, None]
    kc = np.arange(GRID_W)[None, :]
    cs = np.clip(qc - WIN_W // 2, 0, GRID_W - WIN_W)
    valid = (kc >= cs) & (kc < cs + WIN_W)
    dc = kc - qc + WIN_W - 1
    pick = (np.arange(2 * WIN_W - 1)[:, None, None] == dc[None]) & valid[None]
    cols = jnp.einsum("hdm,mqk->hdqk", rpb, jnp.asarray(pick, F32), precision=lax.Precision.HIGHEST)
    cols = jnp.where(valid[None, None], cols, MASK_BIAS)
    t = jnp.stack([cols[:, WIN_H - 1 - off:2 * WIN_H - 1 - off] for off in range(WIN_H)], axis=1)
    return t.transpose(0, 1, 3, 2, 4).reshape(rpb.shape[0], WIN_H, GRID_W, WIN_H * GRID_W)


def _row_mod(len_ctx, tm, ctx_ref, lat_ref, axis=1):
    row = pl.program_id(axis) * tm + lax.broadcasted_iota(jnp.int32, (tm, 1), 0)
    return jnp.where(row < len_ctx, ctx_ref[0], lat_ref[0])


def _row_mod_specs(d, k):
    return [pl.BlockSpec((1, 1, d), lambda b, i, s=s: ((b * 2 + s) * 6 + k, 0, 0)) for s in range(2)]


def _merge_kernel(len_ctx, *refs):
    a = _readout(*refs[:12])
    bp, cn, ga, gb, gc, x, g1c, g1l, wa, wb, wc, wo, o_ref = refs[12:]
    sig = lambda g: jax.nn.sigmoid(g[0].astype(F32))
    m = sig(ga) * _dot_bf(a, wa[...]) + sig(gb) * _dot_bf(bp[0], wb[...]) + sig(gc) * _dot_bf(cn[0], wc[...])
    o_ref[0] = x[0] + _row_mod(len_ctx, x.shape[1], g1c, g1l) * _dot_bf(m, wo[...])


def _merge(rwkv, bp, cn, gates, x, mod, lw, len_ctx):
    bsz, n, d = x.shape
    assert n % TM_ROW == 0
    tok = lambda w, col=0: pl.BlockSpec((1, TM_ROW, w), lambda b, i: (b, i, col))
    consts = [lw["gn_w"], lw["gn_b"], lw["r_k"], lw["gate_up"], lw["ones_bd"]]
    ws = [lw["w_rwkv_o"], lw["w_pool_o"], lw["w_nat_o"], lw["w_out"]]
    return pl.pallas_call(
        functools.partial(_merge_kernel, len_ctx),
        grid=(bsz, n // TM_ROW),
        in_specs=[tok(a.shape[2]) for a in rwkv] + [_const_spec(a.shape) for a in consts]
        + [tok(bp.shape[2]), tok(cn.shape[2]), tok(d, 0), tok(d, 1), tok(d, 2), tok(d)]
        + _row_mod_specs(d, 2) + [_const_spec(w.shape) for w in ws],
        out_specs=tok(d),
        out_shape=jax.ShapeDtypeStruct((bsz, n, d), F32),
        compiler_params=_params("parallel", "parallel"),
        name="merge",
    )(*rwkv, *consts, bp, cn, gates, gates, gates, x, mod, mod, *ws)


def _ffn_kernel(len_ctx, x_ref, sh_c, sh_l, sc_c, sc_l, g2_c, g2_l, gain, w1, w2, o_ref):
    x = x_ref[0]
    tm = x.shape[0]
    h = _norm_mod(x, gain[...], _row_mod(len_ctx, tm, sh_c, sh_l), _row_mod(len_ctx, tm, sc_c, sc_l))
    u = jnp.dot(h.astype(BF16), w1[...], preferred_element_type=F32)
    hid = w2.shape[0]
    gate, up = u[:, :hid], u[:, hid:]
    act = gate * jax.nn.sigmoid(gate) * up
    o_ref[0] = x + _row_mod(len_ctx, tm, g2_c, g2_l) * jnp.dot(act.astype(BF16), w2[...],
                                                              preferred_element_type=F32)


def _ffn(x, mod, lw, len_ctx):
    bsz, n, d = x.shape
    assert n % TM_ROW == 0
    tok = pl.BlockSpec((1, TM_ROW, d), lambda b, i: (b, i, 0))
    return pl.pallas_call(
        functools.partial(_ffn_kernel, len_ctx),
        grid=(bsz, n // TM_ROW),
        in_specs=[tok] + _row_mod_specs(d, 3) + _row_mod_specs(d, 4) + _row_mod_specs(d, 5)
        + [_const_spec(lw["norm2"].shape), _const_spec(lw["w_ffn_in"].shape), _const_spec(lw["w_ffn_out"].shape)],
        out_specs=tok,
        out_shape=jax.ShapeDtypeStruct((bsz, n, d), F32),
        compiler_params=_params("parallel", "parallel"),
        name="ffn",
    )(x, *[mod] * 6, lw["norm2"], lw["w_ffn_in"], lw["w_ffn_out"])


def _block_diag(blocks):
    n = len(blocks)
    rows = []
    for i, blk in enumerate(blocks):
        rows.append(jnp.concatenate([blk if j == i else jnp.zeros((blk.shape[0], blocks[j].shape[1]), blk.dtype)
                                     for j in range(n)], axis=1))
    return jnp.concatenate(rows, axis=0)


def _pad_cols(a, width):
    return jnp.pad(a, ((0, 0), (0, width - a.shape[1])))


def _layer_weights(l, prm):
    c = prm["k_k"].shape[1]
    lora = prm["decay_up"].shape[2]
    gl = prm["gate_up"].shape[1]
    pool = prm["pool_scale"].shape[1]
    d = prm["w_out"].shape[1]
    w_in = prm["w_in"][l]
    o_lora, o_gd, o_pool = 3 * c, 3 * c + 4 * lora, 3 * c + 4 * lora + gl
    o_q = o_pool + pool
    o_gate = o_q + 3 * c
    assert 4 * lora == 2 * LANES and gl <= 2 * LANES and pool == 2 * LANES and o_gate + 3 * d == w_in.shape[1]
    w_mix = jnp.concatenate([
        w_in[:, 0:o_lora], w_in[:, o_q:o_gate], w_in[:, o_lora:o_gd],
        _pad_cols(w_in[:, o_gd:o_pool], 2 * LANES), w_in[:, o_pool:o_q]], axis=1).astype(BF16)
    mu = lambda m: _pad_cols(m[l][None, :o_pool], o_pool + 2 * LANES - gl)
    heads = c // HEAD
    row = lambda a: a.reshape(1, -1)
    return {
        "w_mix": w_mix, "w_gate": w_in[:, o_gate:].astype(BF16),
        "norm1": row(prm["norm1"][l]), "norm2": row(prm["norm2"][l]),
        "mu_prev": mu(prm["mu_prev"]), "mu_next": mu(prm["mu_next"]),
        "decay_up": _block_diag([prm["decay_up"][l, 0], prm["decay_up"][l, 1]]),
        "decay_w0": row(prm["decay_w0"][l]),
        "iclr_up": _block_diag([prm["iclr_up"][l, 0], prm["iclr_up"][l, 1]]),
        "iclr_a0": row(prm["iclr_a0"][l]),
        "k_k": row(prm["k_k"][l]), "k_a": row(prm["k_a"][l]), "r_k": row(prm["r_k"][l]),
        "gn_w": row(prm["gn_w"][l]), "gn_b": row(prm["gn_b"][l]),
        "gate_up": jnp.pad(prm["gate_up"][l], ((0, 2 * LANES - gl), (0, 0))),
        "ones_bd": jnp.kron(jnp.eye(heads, dtype=F32), jnp.ones((HEAD, HEAD), F32)).astype(BF16),
        "pool_w": _block_diag([prm["pool_w"][l, g] for g in range(len(POOL_WINDOWS))]),
        "pool_scale": row(prm["pool_scale"][l]),
        "q_gain": row(jnp.tile(prm["q_gain"][l], heads)), "k_gain": row(jnp.tile(prm["k_gain"][l], heads)),
        "nat_bias": _nat_bias_table(prm["rpb"][l]),
        "w_rwkv_o": prm["w_rwkv_o"][l].astype(BF16), "w_pool_o": prm["w_pool_o"][l].astype(BF16),
        "w_nat_o": prm["w_nat_o"][l].astype(BF16), "w_out": prm["w_out"][l].astype(BF16),
        "w_ffn_in": prm["w_ffn_in"][l].astype(BF16), "w_ffn_out": prm["w_ffn_out"][l].astype(BF16),
    }


def kernel(x, c, ctx, c_ctx, w_mod, b_mod, norm1, norm2, w_in, mu_prev, mu_next, decay_w0, decay_up, iclr_a0, iclr_up, gate_up, k_k, k_a, r_k, gn_w, gn_b, pool_w, pool_scale, q_gain, k_gain, rpb, w_rwkv_o, w_pool_o, w_nat_o, w_out, w_ffn_in, w_ffn_out):
    prm = dict(norm1=norm1, norm2=norm2, w_in=w_in, mu_prev=mu_prev, mu_next=mu_next, decay_w0=decay_w0,
               decay_up=decay_up, iclr_a0=iclr_a0, iclr_up=iclr_up, gate_up=gate_up, k_k=k_k, k_a=k_a, r_k=r_k,
               gn_w=gn_w, gn_b=gn_b, pool_w=pool_w, pool_scale=pool_scale, q_gain=q_gain, k_gain=k_gain, rpb=rpb,
               w_rwkv_o=w_rwkv_o, w_pool_o=w_pool_o, w_nat_o=w_nat_o, w_out=w_out, w_ffn_in=w_ffn_in,
               w_ffn_out=w_ffn_out)
    bsz, len_lat, d = x.shape
    len_ctx = ctx.shape[1]
    depth = w_mod.shape[0]
    assert len_ctx % TM == 0 and len_lat % TM == 0 and bsz + 1 <= SUBLANES
    nct = len_ctx // TM

    s_rows = jnp.concatenate([c, c_ctx[None, :], jnp.zeros((SUBLANES - bsz - 1, d), F32)], axis=0)
    mod_all = _modulation(s_rows, w_mod, b_mod)
    xa = jnp.concatenate([ctx, x], axis=1)

    for l in range(depth):
        lw = _layer_weights(l, prm)
        m_lat = mod_all[l, :bsz]
        m_ctx = jnp.broadcast_to(mod_all[l, bsz][None], m_lat.shape)
        mod = jnp.stack([m_ctx, m_lat], axis=1).reshape(bsz * 2 * 6, 1, d)

        p = _in_proj(xa, mod, lw["norm1"], lw["w_mix"], len_ctx, F32)
        gates = _in_proj(xa, mod, lw["norm1"], lw["w_gate"], len_ctx, BF16)
        ld_f, ld_b, kd_f, kd_b, b_f, b_b, v, kk, r, gd, b_br, qn, kn, vn = _mixer_prep(p, lw, nct, len_ctx, len_lat)
        y_f, y_b = _rwkv_scan(ld_f, ld_b, kd_f, kd_b, b_f, b_b, v, kk, r, len_ctx // CHUNK)
        c_br = _nat_attention(qn, kn, vn, lw["nat_bias"], len_ctx, len_lat)
        xa = _merge((y_f, y_b, r, kd_f, kd_b, v, gd), b_br, c_br, gates, xa, mod, lw, len_ctx)
        xa = _ffn(xa, mod, lw, len_ctx)
    return xa[:, len_ctx:]
```

```python
import functools

import numpy as np
import jax
import jax.numpy as jnp
from jax import lax
from jax.experimental import pallas as pl
from jax.experimental.pallas import tpu as pltpu

F32 = jnp.float32
BF16 = jnp.bfloat16

HEAD = 64
NORM_EPS = 1e-6
GN_EPS = 64e-5
KK_EPS = 1e-24
POOL_WINDOWS = (2, 4, 8, 16)
POOL_GROUP = 64
GRID_W = 64
WIN_H = 8
WIN_W = 16
MASK_BIAS = -1e30

LANES = 128
SUBLANES = 8
VMEM_LIMIT = 56 * 1024 * 1024

TM = 256
NAT_ROWS = 4
CHUNK = 64
SCAN_BLOCK = 256
TM_IN = 640
TM_ROW = 640
TN_IN = 3072
TN_MOD = 1536


def _dot_hi(a, b):
    return jnp.dot(a, b, precision=lax.Precision.HIGHEST, preferred_element_type=F32)


def _dot_bf(a, b):
    return jnp.dot(a.astype(BF16), b.astype(BF16), preferred_element_type=F32)


def _dot_nt(a, b):
    return lax.dot_general(a, b, (((1,), (1,)), ((), ())), preferred_element_type=F32)


def _params(*sem):
    return pltpu.CompilerParams(dimension_semantics=sem, vmem_limit_bytes=VMEM_LIMIT)


def _const_spec(shape):
    nd = len(shape)
    return pl.BlockSpec(shape, lambda *_: (0,) * nd, pipeline_mode=pl.Buffered(1))


def _mod_kernel(s_ref, w_ref, b_ref, o_ref):
    s = s_ref[...]
    s = s * jax.nn.sigmoid(s)
    o_ref[0] = _dot_hi(s, w_ref[0]) + b_ref[0]


def _modulation(s_rows, w_mod, b_mod):
    depth, d, n = w_mod.shape
    return pl.pallas_call(
        _mod_kernel,
        grid=(depth, n // TN_MOD),
        in_specs=[
            pl.BlockSpec((SUBLANES, d), lambda l, j: (0, 0)),
            pl.BlockSpec((1, d, TN_MOD), lambda l, j: (l, 0, j)),
            pl.BlockSpec((1, 1, TN_MOD), lambda l, j: (l, 0, j)),
        ],
        out_specs=pl.BlockSpec((1, SUBLANES, TN_MOD), lambda l, j: (l, 0, j)),
        out_shape=jax.ShapeDtypeStruct((depth, SUBLANES, n), F32),
        compiler_params=_params("parallel", "parallel"),
        name="modulation",
    )(s_rows, w_mod, b_mod.reshape(depth, 1, n))


def _norm_mod(x, gain, shift, scale):
    ms = jnp.mean(x * x, axis=-1, keepdims=True)
    return x * lax.rsqrt(ms + NORM_EPS) * gain * (1.0 + scale) + shift


def _in_proj_kernel(len_ctx, x_ref, sh_c, sc_c, sh_l, sc_l, g_ref, w_ref, o_ref):
    tm = x_ref.shape[1]
    h = _norm_mod(x_ref[0], g_ref[...], _row_mod(len_ctx, tm, sh_c, sh_l, axis=2), _row_mod(len_ctx, tm, sc_c, sc_l, axis=2))
    o_ref[0] = jnp.dot(h.astype(BF16), w_ref[...], preferred_element_type=F32).astype(o_ref.dtype)


def _in_proj(x, mod, gain, w, len_ctx, out_dtype):
    bsz, n, d = x.shape
    nout = w.shape[1]
    assert n % TM_IN == 0 and nout % TN_IN == 0
    mspec = lambda is_lat, k: pl.BlockSpec((1, 1, d), lambda j, b, i: ((b * 2 + is_lat) * 6 + k, 0, 0))
    return pl.pallas_call(
        functools.partial(_in_proj_kernel, len_ctx),
        grid=(nout // TN_IN, bsz, n // TM_IN),
        in_specs=[
            pl.BlockSpec((1, TM_IN, d), lambda j, b, i: (b, i, 0)),
            mspec(0, 0), mspec(0, 1), mspec(1, 0), mspec(1, 1),
            pl.BlockSpec((1, d), lambda j, b, i: (0, 0)),
            pl.BlockSpec((d, TN_IN), lambda j, b, i: (0, j)),
        ],
        out_specs=pl.BlockSpec((1, TM_IN, TN_IN), lambda j, b, i: (b, i, j)),
        out_shape=jax.ShapeDtypeStruct((bsz, n, nout), out_dtype),
        compiler_params=_params("parallel", "parallel", "parallel"),
        name="in_proj",
    )(x, mod, mod, mod, mod, gain, w)


def _halo_specs(width, col, n):
    nb = n // SUBLANES
    per = TM // SUBLANES
    return [
        pl.BlockSpec((1, TM, width), lambda b, i: (b, i, col)),
        pl.BlockSpec((1, SUBLANES, width), lambda b, i: (b, jnp.maximum(i * per - 1, 0), col)),
        pl.BlockSpec((1, SUBLANES, width), lambda b, i: (b, jnp.minimum((i + 1) * per, nb - 1), col)),
    ]


def _prep_kernel(nct, nt, rkv_ref, rkv_p, rkv_n, lo_ref, lo_p, lo_n, gd_ref, gd_p, gd_n,
                 mup_ref, mun_ref, du_ref, w0_ref, au_ref, a0_ref, kkw_ref, ka_ref, ones_ref,
                 wf_ref, wb_ref, kdf_ref, kdb_ref, bf_ref, bb_ref, v_ref, kk_ref, r_ref, gdo_ref):
    i = pl.program_id(1)
    first = jnp.logical_or(i == 0, i == nct)
    last = jnp.logical_or(i == nct - 1, i == nt - 1)
    row = lax.broadcasted_iota(jnp.int32, (TM, 1), 0)
    c = rkv_ref.shape[2] // 3

    def mix(main, prev8, next8, lo, hi):
        pm = main[0]
        prow = jnp.where(first, 0.0, prev8[0, SUBLANES - 1:SUBLANES, :])
        nrow = jnp.where(last, 0.0, next8[0, 0:1, :])
        prev = jnp.where(row == 0, prow, pltpu.roll(pm, 1, 0))
        nxt = jnp.where(row == TM - 1, nrow, pltpu.roll(pm, TM - 1, 0))
        return pm + mup_ref[:, lo:hi] * (prev - pm) + mun_ref[:, lo:hi] * (nxt - pm)

    rkv = mix(rkv_ref, rkv_p, rkv_n, 0, 3 * c)
    lora = mix(lo_ref, lo_p, lo_n, 3 * c, 3 * c + 2 * LANES)
    gdo_ref[0] = mix(gd_ref, gd_p, gd_n, 3 * c + 2 * LANES, 3 * c + 4 * LANES)

    r, k, v = rkv[:, 0:c], rkv[:, c:2 * c], rkv[:, 2 * c:3 * c]
    wd, ad = lora[:, 0:LANES], lora[:, LANES:2 * LANES]
    dec = _mm(jnp.tanh(wd), du_ref[...], split_a=True, split_b=True) + w0_ref[...]
    log_decay = -float(np.exp(-0.5)) * jax.nn.sigmoid(dec)
    a = jax.nn.sigmoid(_mm(ad, au_ref[...], split_a=True, split_b=True) + a0_ref[...])
    kk = k * kkw_ref[...]
    kk = kk * lax.rsqrt(jnp.maximum(_mm(kk * kk, ones_ref[...], split_a=True), KK_EPS))
    ka = ka_ref[...]
    for z, (w_o, kd_o, b_o) in enumerate(((wf_ref, kdf_ref, bf_ref), (wb_ref, kdb_ref, bb_ref))):
        az = a[:, z * c:(z + 1) * c]
        w_o[0] = log_decay[:, z * c:(z + 1) * c]
        kd_o[0] = k * (1.0 + (az - 1.0) * ka)
        b_o[0] = kk * az
    v_ref[0] = v
    kk_ref[0] = kk
    r_ref[0] = r


def _mixer_prep_kernel(nct, nt, len_ctx, len_lat, *refs):
    n_in = (18, 5, 6)
    n_out = (10, 1, 3)
    cuts = np.cumsum(n_in + n_out).tolist()
    prep_in, pool_in, qk_in, prep_out, pool_out, qk_out = [refs[a:b] for a, b in zip([0] + cuts[:-1], cuts)]
    _prep_kernel(nct, nt, *prep_in, *prep_out)
    _pool_kernel(nct, len_ctx, len_lat, *pool_in, *pool_out)
    _qk_norm_kernel(*qk_in, *qk_out)


def _mixer_prep(p, lw, nct, len_ctx, len_lat):
    bsz, n, _ = p.shape
    c = lw["k_k"].shape[1]
    nt = n // TM
    wide = 2 * LANES
    lora_col = (6 * c) // wide
    prep_consts = [lw["mu_prev"], lw["mu_next"], lw["decay_up"], lw["decay_w0"], lw["iclr_up"], lw["iclr_a0"],
                   lw["k_k"], lw["k_a"], lw["ones_bd"]]
    pool_consts = [lw["pool_w"], lw["pool_scale"]]
    qk_consts = [lw["q_gain"], lw["k_gain"], lw["ones_bd"]]
    consts = lambda arrs: [_const_spec(a.shape) for a in arrs]
    tok = lambda w: pl.BlockSpec((1, TM, w), lambda b, i: (b, i, 0))
    heads = c // HEAD
    head_major = pl.BlockSpec((1, heads, TM, HEAD), lambda b, i: (b, 0, i, 0))
    return pl.pallas_call(
        functools.partial(_mixer_prep_kernel, nct, nt, len_ctx, len_lat),
        grid=(bsz, nt),
        in_specs=_halo_specs(3 * c, 0, n) + _halo_specs(wide, lora_col, n) + _halo_specs(wide, lora_col + 1, n)
        + consts(prep_consts) + _halo_specs(wide, lora_col + 2, n) + consts(pool_consts)
        + [pl.BlockSpec((1, TM, c), lambda b, i, j=j: (b, i, 3 + j)) for j in range(3)] + consts(qk_consts),
        out_specs=[tok(c)] * 9 + [tok(wide), tok(wide)] + [head_major] * 3,
        out_shape=[jax.ShapeDtypeStruct((bsz, n, c), F32)] * 9 + [jax.ShapeDtypeStruct((bsz, n, wide), F32)] * 2
        + [jax.ShapeDtypeStruct((bsz, heads, n, HEAD), BF16)] * 3,
        compiler_params=_params("parallel", "parallel"),
        name="mixer_prep",
    )(*[p] * 9, *prep_consts, *[p] * 3, *pool_consts, *[p] * 3, *qk_consts)


def _split(a):
    hi = a.astype(BF16)
    return hi, (a - hi.astype(F32)).astype(BF16)


def _mm(a, b, split_a=False, split_b=False):
    def halves(x, split):
        if x.dtype == BF16 or not split:
            return x.astype(BF16), None
        return _split(x)

    (ah, al), (bh, bl) = halves(a, split_a), halves(b, split_b)
    lhs, rhs = [ah], [bh]
    if al is not None:
        lhs.append(al)
        rhs.append(bh)
    if bl is not None:
        lhs.append(ah)
        rhs.append(bl)
    if len(lhs) == 1:
        return jnp.dot(ah, bh, preferred_element_type=F32)
    return jnp.dot(jnp.concatenate(lhs, axis=1), jnp.concatenate(rhs, axis=0), preferred_element_type=F32)


def _chunk_scan_kernel(ldf, kdf, bf, vf, kkf, rf, ldb, kdb, bb, vb, kkb, rb, yf_ref, yb_ref, h_ref):
    @pl.when(pl.program_id(0) == 0)
    def _():
        h_ref[...] = jnp.zeros_like(h_ref)

    bsz, block_len, c = ldf.shape
    cs = CHUNK
    dirs = ((ldf, kdf, bf, vf, kkf, rf, yf_ref), (ldb, kdb, bb, vb, kkb, rb, yb_ref))
    t_i = lax.broadcasted_iota(jnp.int32, (cs, LANES), 0)
    lane = lax.broadcasted_iota(jnp.int32, (cs, LANES), 1)
    s_i = jnp.bitwise_and(lane, HEAD - 1)
    m_a = (lane < HEAD).astype(F32).astype(BF16)
    m_b = (lane >= HEAD).astype(F32).astype(BF16)
    r2 = lax.broadcasted_iota(jnp.int32, (LANES, LANES), 0)
    l2 = lax.broadcasted_iota(jnp.int32, (LANES, LANES), 1)
    block = ((r2 >= HEAD) == (l2 >= HEAD)).astype(F32)
    eye = (r2 == l2).astype(F32)
    zeros = jnp.zeros((cs, LANES), BF16)

    def bd(x):
        k = x.shape[1] // LANES
        return jnp.concatenate([x * jnp.concatenate([m_a] * k, axis=1), x * jnp.concatenate([m_b] * k, axis=1)],
                               axis=0)

    def unit(d, b, p, rows):
        ls = slice(p * LANES, (p + 1) * LANES)
        ld, kd, b_, v, kk, r = (ref[b, rows, ls] for ref in dirs[d][:6])
        before = (s_i < t_i) if d == 0 else (s_i > t_i)
        upto = (s_i <= t_i) if d == 0 else (s_i >= t_i)
        tri = upto.astype(F32).astype(BF16)
        lh, ll = _split(ld)
        big_l = jnp.dot(tri, jnp.concatenate([lh, ll], axis=0), preferred_element_type=F32)
        yield
        ltot = big_l[cs - 1:cs] if d == 0 else big_l[0:1]
        kap = kk * jnp.exp(big_l - ld)
        rt = r * jnp.exp(big_l)
        einv = jnp.exp(-big_l)
        kt, bt = kd * einv, b_ * einv
        efin = jnp.exp(ltot - big_l)
        khat, bhat = kd * efin, b_ * efin
        ktb, btb, vb16 = kt.astype(BF16), bt.astype(BF16), v.astype(BF16)
        sc = _dot_nt(jnp.concatenate([kap, rt], axis=0).astype(BF16),
                     jnp.concatenate([ktb * m_a, ktb * m_b, btb * m_a, btb * m_b], axis=0))
        yield
        a_k = jnp.where(before, sc[:cs, :LANES], 0.0)
        n_p = jnp.where(before, -sc[:cs, LANES:], 0.0)
        m_k = jnp.where(upto, sc[cs:, :LANES], 0.0)
        m_nb = jnp.where(upto, -sc[cs:, LANES:], 0.0)
        x = jnp.concatenate([kap, _mm(a_k, bd(vb16))], axis=1)
        yield
        for level in range(6):
            xh, xl = _split(x)
            nb = n_p.astype(BF16)
            x = x + jnp.dot(jnp.concatenate([nb, nb], axis=1), jnp.concatenate([bd(xh), bd(xl)], axis=0),
                            preferred_element_type=F32)
            if level < 5:
                n_p = jnp.dot(nb, bd(nb), preferred_element_type=F32)
            yield
        vz = jnp.concatenate([vb16, zeros], axis=1)
        uw = jnp.concatenate([x[:, LANES:], x[:, :LANES]], axis=1).astype(BF16)
        o1 = _mm(jnp.concatenate([m_k, m_nb], axis=1), jnp.concatenate([bd(vz), bd(uw)], axis=0))
        o2 = _mm(jnp.concatenate([khat, -bhat], axis=0).T, jnp.concatenate([vz, uw], axis=0))
        yield
        h = h_ref[b, d, p]
        y0, q = o1[:, :LANES], rt + o1[:, LANES:]
        psi = o2[:, :LANES] * block
        phi = o2[:, LANES:] * block + eye * jnp.exp(ltot)
        dirs[d][6][b, rows, ls] = y0 + _mm(q, h)
        h_ref[b, d, p] = _mm(phi, h, split_a=True, split_b=True) + psi

    n_sub = block_len // cs
    units = {}
    for s in range(n_sub):
        for d in range(2):
            ci = s if d == 0 else n_sub - 1 - s
            for b in range(bsz):
                for p in range(c // LANES):
                    units[s, d, b, p] = unit(d, b, p, slice(ci * cs, (ci + 1) * cs))
    _run_interleaved(units)


def _rwkv_scan(ld_f, ld_b, kd_f, kd_b, b_f, b_b, v, kk, r, len_ctx):
    bsz, n, c = v.shape
    assert len_ctx % SCAN_BLOCK == 0 and n % SCAN_BLOCK == 0 and SCAN_BLOCK % CHUNK == 0
    nc, nct_c = n // SCAN_BLOCK, len_ctx // SCAN_BLOCK

    def rev(g):
        return jnp.where(g < nct_c, nct_c - 1 - g, nc - 1 - g + nct_c)

    fwd = pl.BlockSpec((bsz, SCAN_BLOCK, c), lambda g: (0, g, 0))
    bwd = pl.BlockSpec((bsz, SCAN_BLOCK, c), lambda g: (0, rev(g), 0))
    tok = jax.ShapeDtypeStruct((bsz, n, c), F32)
    return pl.pallas_call(
        _chunk_scan_kernel,
        grid=(nc,),
        in_specs=[fwd] * 6 + [bwd] * 6,
        out_specs=[fwd, bwd],
        out_shape=[tok, tok],
        scratch_shapes=[pltpu.VMEM((bsz, 2, c // LANES, LANES, LANES), F32)],
        compiler_params=_params("arbitrary"),
        name="rwkv_scan",
    )(ld_f, kd_f, b_f, v, kk, r, ld_b, kd_b, b_b, v, kk, r)


def _readout(yf, yb, r, kdf, kdb, v, gd, gnw, gnb, rk, gup, ones_ref):
    ones = ones_ref[...]
    inv = 1.0 / HEAD
    y = yf[0] + yb[0]
    yc = y - _mm(y, ones, split_a=True) * inv
    var = _mm(yc * yc, ones, split_a=True) * inv
    yn = yc * lax.rsqrt(var + GN_EPS) * gnw[...] + gnb[...]
    bonus = _mm(r[0] * rk[...] * (kdf[0] + kdb[0]), ones, split_a=True) * v[0]
    return (yn + bonus) * _mm(jax.nn.sigmoid(gd[0]), gup[...])


def _pool_kernel(nct, len_ctx, len_lat, main, prev8, next8, pw_ref, scale_ref, o_ref):
    i = pl.program_id(1)
    is_lat = i >= nct
    seq_len = jnp.where(is_lat, len_lat, len_ctx)
    t0 = jnp.where(is_lat, i - nct, i) * TM
    n = TM + 2 * SUBLANES
    pm = main[0]
    ext = jnp.concatenate([prev8[0], pm, next8[0]], axis=0)
    pos = t0 - SUBLANES + lax.broadcasted_iota(jnp.int32, (n, 1), 0)
    e = jnp.where(jnp.logical_and(pos >= 0, pos < seq_len), ext, 0.0)
    a2 = e + pltpu.roll(e, 1, 0)
    a4 = pltpu.roll(a2, 1, 0) + pltpu.roll(a2, n - 1, 0)
    a8 = pltpu.roll(a4, 2, 0) + pltpu.roll(a4, n - 2, 0)
    a16 = pltpu.roll(a8, 4, 0) + pltpu.roll(a8, n - 4, 0)
    t = t0 + lax.broadcasted_iota(jnp.int32, (TM, 1), 0)
    lane = lax.broadcasted_iota(jnp.int32, pm.shape, 1)
    mean = None
    for g, (w, acc) in reversed(list(enumerate(zip(POOL_WINDOWS, (a2, a4, a8, a16))))):
        lo = jnp.maximum(t - w // 2, 0)
        hi = jnp.minimum(t + (w - w // 2) - 1, seq_len - 1)
        m = acc[SUBLANES:SUBLANES + TM] / (hi - lo + 1).astype(F32)
        mean = m if mean is None else jnp.where(lane < (g + 1) * POOL_GROUP, m, mean)
    o_ref[0] = _mm(mean - pm, pw_ref[...], split_a=True, split_b=True) * scale_ref[...]


def _qk_norm_kernel(q_ref, k_ref, v_ref, qg_ref, kg_ref, ones_ref, qo, ko, vo):
    ones = ones_ref[...]
    inv = 1.0 / HEAD
    q, k = q_ref[0], k_ref[0]
    qn = q * lax.rsqrt(_mm(q * q, ones, split_a=True) * inv + NORM_EPS) * qg_ref[...]
    kn = k * lax.rsqrt(_mm(k * k, ones, split_a=True) * inv + NORM_EPS) * kg_ref[...]
    for o_ref, val in ((qo, qn * HEAD ** -0.5), (ko, kn), (vo, v_ref[0])):
        val = val.astype(BF16)
        for h in range(o_ref.shape[1]):
            o_ref[0, h] = val[:, h * HEAD:(h + 1) * HEAD]


def _run_interleaved(units):
    done = {}
    while units:
        for k in list(units):
            try:
                next(units[k])
            except StopIteration as stop:
                done[k] = stop.value
                del units[k]
    return done


def _attend(q, key_sets, bias):
    scores = [_dot_nt(q, k) for k, _ in key_sets]
    yield
    if bias is not None:
        scores[0] = scores[0] + bias
    m = functools.reduce(jnp.maximum, [jnp.max(s, axis=-1, keepdims=True) for s in scores])
    yield
    ps = [jnp.exp(s - m) for s in scores]
    den = functools.reduce(jnp.add, [jnp.sum(p, axis=-1, keepdims=True) for p in ps])
    num = functools.reduce(jnp.add, [jnp.dot(p.astype(BF16), v, preferred_element_type=F32)
                                      for p, (_, v) in zip(ps, key_sets)])
    yield
    return num / den


def _nat_kernel(rows, q_ref, kp, kc, kn, vp, vc, vn, kx_ref, vx_ref, bias_ref, o_ref, ks, vs):
    i = pl.program_id(1) - 1
    heads, tq = q_ref.shape[1], q_ref.shape[2]
    rb = tq // GRID_W
    nloc = WIN_H * GRID_W

    @pl.when(i < 0)
    def _():
        units = {h: _attend(q_ref[0, h], [(kx_ref[0, h], vx_ref[0, h])], None) for h in range(heads)}
        for h, o in _run_interleaved(units).items():
            o_ref[0, :, h * HEAD:(h + 1) * HEAD] = o

    @pl.when(i >= 0)
    def _():
        for j, (kr, vr) in enumerate(((kp, vp), (kc, vc), (kn, vn))):
            ks[:, j * tq:(j + 1) * tq, :] = kr[0]
            vs[:, j * tq:(j + 1) * tq, :] = vr[0]
        for r0 in range(0, rb, NAT_ROWS):
            units = {}
            for rr in range(r0, r0 + NAT_ROWS):
                r = i * rb + rr
                rs = jnp.clip(r - WIN_H // 2, 0, rows - WIN_H)
                off = r - rs
                start = pl.multiple_of((rs - i * rb + rb) * GRID_W, GRID_W)
                qs = slice(rr * GRID_W, (rr + 1) * GRID_W)
                for h in range(heads):
                    units[rr, h] = _attend(q_ref[0, h, qs],
                                           [(ks[h, pl.ds(start, nloc)], vs[h, pl.ds(start, nloc)]),
                                            (kx_ref[0, h], vx_ref[0, h])], bias_ref[h, off])
            for (rr, h), o in _run_interleaved(units).items():
                o_ref[0, rr * GRID_W:(rr + 1) * GRID_W, h * HEAD:(h + 1) * HEAD] = o


def _nat_attention(qn, kn, vn, bias, len_ctx, len_lat):
    bsz, heads, n, hd = qn.shape
    tq = len_ctx
    assert tq == (WIN_H // 2) * GRID_W and len_lat % tq == 0
    rows = len_lat // GRID_W
    nblk = len_lat // tq

    def blk(shift):
        return pl.BlockSpec((1, heads, tq, hd),
                            lambda b, i: (b, 0, jnp.where(i == 0, 0, 1 + jnp.clip(i - 1 + shift, 0, nblk - 1)), 0))

    ctx = pl.BlockSpec((1, heads, tq, hd), lambda b, i: (b, 0, 0, 0))
    return pl.pallas_call(
        functools.partial(_nat_kernel, rows),
        grid=(bsz, 1 + nblk),
        in_specs=[blk(0), blk(-1), blk(0), blk(1), blk(-1), blk(0), blk(1), ctx, ctx, _const_spec(bias.shape)],
        out_specs=pl.BlockSpec((1, tq, heads * hd), lambda b, i: (b, i, 0)),
        out_shape=jax.ShapeDtypeStruct((bsz, n, heads * hd), F32),
        scratch_shapes=[pltpu.VMEM((heads, 3 * tq, hd), BF16), pltpu.VMEM((heads, 3 * tq, hd), BF16)],
        compiler_params=_params("parallel", "parallel"),
        name="nat_attention",
    )(qn, kn, kn, kn, vn, vn, vn, kn, vn, bias)


def _nat_bias_table(rpb):
    qc = np.arange(GRID_W)[:, None]
    kc = np.arange(GRID_W)[None, :]
    cs = np.clip(qc - WIN_W // 2, 0, GRID_W - WIN_W)
    valid = (kc >= cs) & (kc < cs + WIN_W)
    dc = kc - qc + WIN_W - 1
    pick = (np.arange(2 * WIN_W - 1)[:, None, None] == dc[None]) & valid[None]
    cols = jnp.einsum("hdm,mqk->hdqk", rpb, jnp.asarray(pick, F32), precision=lax.Precision.HIGHEST)
    cols = jnp.where(valid[None, None], cols, MASK_BIAS)
    t = jnp.stack([cols[:, WIN_H - 1 - off:2 * WIN_H - 1 - off] for off in range(WIN_H)], axis=1)
    return t.transpose(0, 1, 3, 2, 4).reshape(rpb.shape[0], WIN_H, GRID_W, WIN_H * GRID_W)


def _row_mod(len_ctx, tm, ctx_ref, lat_ref, axis=1):
    row = pl.program_id(axis) * tm + lax.broadcasted_iota(jnp.int32, (tm, 1), 0)
    return jnp.where(row < len_ctx, ctx_ref[0], lat_ref[0])


def _row_mod_specs(d, k):
    return [pl.BlockSpec((1, 1, d), lambda b, i, s=s: ((b * 2 + s) * 6 + k, 0, 0)) for s in range(2)]


def _merge_kernel(len_ctx, *refs):
    a = _readout(*refs[:12])
    bp, cn, ga, gb, gc, x, g1c, g1l, wa, wb, wc, wo, o_ref = refs[12:]
    sig = lambda g: jax.nn.sigmoid(g[0].astype(F32))
    m = sig(ga) * _dot_bf(a, wa[...]) + sig(gb) * _dot_bf(bp[0], wb[...]) + sig(gc) * _dot_bf(cn[0], wc[...])
    o_ref[0] = x[0] + _row_mod(len_ctx, x.shape[1], g1c, g1l) * _dot_bf(m, wo[...])


def _merge(rwkv, bp, cn, gates, x, mod, lw, len_ctx):
    bsz, n, d = x.shape
    assert n % TM_ROW == 0
    tok = lambda w, col=0: pl.BlockSpec((1, TM_ROW, w), lambda b, i: (b, i, col))
    consts = [lw["gn_w"], lw["gn_b"], lw["r_k"], lw["gate_up"], lw["ones_bd"]]
    ws = [lw["w_rwkv_o"], lw["w_pool_o"], lw["w_nat_o"], lw["w_out"]]
    return pl.pallas_call(
        functools.partial(_merge_kernel, len_ctx),
        grid=(bsz, n // TM_ROW),
        in_specs=[tok(a.shape[2]) for a in rwkv] + [_const_spec(a.shape) for a in consts]
        + [tok(bp.shape[2]), tok(cn.shape[2]), tok(d, 0), tok(d, 1), tok(d, 2), tok(d)]
        + _row_mod_specs(d, 2) + [_const_spec(w.shape) for w in ws],
        out_specs=tok(d),
        out_shape=jax.ShapeDtypeStruct((bsz, n, d), F32),
        compiler_params=_params("parallel", "parallel"),
        name="merge",
    )(*rwkv, *consts, bp, cn, gates, gates, gates, x, mod, mod, *ws)


def _ffn_kernel(len_ctx, x_ref, sh_c, sh_l, sc_c, sc_l, g2_c, g2_l, gain, w1, w2, o_ref):
    x = x_ref[0]
    tm = x.shape[0]
    h = _norm_mod(x, gain[...], _row_mod(len_ctx, tm, sh_c, sh_l), _row_mod(len_ctx, tm, sc_c, sc_l))
    u = jnp.dot(h.astype(BF16), w1[...], preferred_element_type=F32)
    hid = w2.shape[0]
    gate, up = u[:, :hid], u[:, hid:]
    act = gate * jax.nn.sigmoid(gate) * up
    o_ref[0] = x + _row_mod(len_ctx, tm, g2_c, g2_l) * jnp.dot(act.astype(BF16), w2[...],
                                                              preferred_element_type=F32)


def _ffn(x, mod, lw, len_ctx):
    bsz, n, d = x.shape
    assert n % TM_ROW == 0
    tok = pl.BlockSpec((1, TM_ROW, d), lambda b, i: (b, i, 0))
    return pl.pallas_call(
        functools.partial(_ffn_kernel, len_ctx),
        grid=(bsz, n // TM_ROW),
        in_specs=[tok] + _row_mod_specs(d, 3) + _row_mod_specs(d, 4) + _row_mod_specs(d, 5)
        + [_const_spec(lw["norm2"].shape), _const_spec(lw["w_ffn_in"].shape), _const_spec(lw["w_ffn_out"].shape)],
        out_specs=tok,
        out_shape=jax.ShapeDtypeStruct((bsz, n, d), F32),
        compiler_params=_params("parallel", "parallel"),
        name="ffn",
    )(x, *[mod] * 6, lw["norm2"], lw["w_ffn_in"], lw["w_ffn_out"])


def _block_diag(blocks):
    n = len(blocks)
    rows = []
    for i, blk in enumerate(blocks):
        rows.append(jnp.concatenate([blk if j == i else jnp.zeros((blk.shape[0], blocks[j].shape[1]), blk.dtype)
                                     for j in range(n)], axis=1))
    return jnp.concatenate(rows, axis=0)


def _pad_cols(a, width):
    return jnp.pad(a, ((0, 0), (0, width - a.shape[1])))


def _layer_weights(l, prm):
    c = prm["k_k"].shape[1]
    lora = prm["decay_up"].shape[2]
    gl = prm["gate_up"].shape[1]
    pool = prm["pool_scale"].shape[1]
    d = prm["w_out"].shape[1]
    w_in = prm["w_in"][l]
    o_lora, o_gd, o_pool = 3 * c, 3 * c + 4 * lora, 3 * c + 4 * lora + gl
    o_q = o_pool + pool
    o_gate = o_q + 3 * c
    assert 4 * lora == 2 * LANES and gl <= 2 * LANES and pool == 2 * LANES and o_gate + 3 * d == w_in.shape[1]
    w_mix = jnp.concatenate([
        w_in[:, 0:o_lora], w_in[:, o_q:o_gate], w_in[:, o_lora:o_gd],
        _pad_cols(w_in[:, o_gd:o_pool], 2 * LANES), w_in[:, o_pool:o_q]], axis=1).astype(BF16)
    mu = lambda m: _pad_cols(m[l][None, :o_pool], o_pool + 2 * LANES - gl)
    heads = c // HEAD
    row = lambda a: a.reshape(1, -1)
    return {
        "w_mix": w_mix, "w_gate": w_in[:, o_gate:].astype(BF16),
        "norm1": row(prm["norm1"][l]), "norm2": row(prm["norm2"][l]),
        "mu_prev": mu(prm["mu_prev"]), "mu_next": mu(prm["mu_next"]),
        "decay_up": _block_diag([prm["decay_up"][l, 0], prm["decay_up"][l, 1]]),
        "decay_w0": row(prm["decay_w0"][l]),
        "iclr_up": _block_diag([prm["iclr_up"][l, 0], prm["iclr_up"][l, 1]]),
        "iclr_a0": row(prm["iclr_a0"][l]),
        "k_k": row(prm["k_k"][l]), "k_a": row(prm["k_a"][l]), "r_k": row(prm["r_k"][l]),
        "gn_w": row(prm["gn_w"][l]), "gn_b": row(prm["gn_b"][l]),
        "gate_up": jnp.pad(prm["gate_up"][l], ((0, 2 * LANES - gl), (0, 0))),
        "ones_bd": jnp.kron(jnp.eye(heads, dtype=F32), jnp.ones((HEAD, HEAD), F32)).astype(BF16),
        "pool_w": _block_diag([prm["pool_w"][l, g] for g in range(len(POOL_WINDOWS))]),
        "pool_scale": row(prm["pool_scale"][l]),
        "q_gain": row(jnp.tile(prm["q_gain"][l], heads)), "k_gain": row(jnp.tile(prm["k_gain"][l], heads)),
        "nat_bias": _nat_bias_table(prm["rpb"][l]),
        "w_rwkv_o": prm["w_rwkv_o"][l].astype(BF16), "w_pool_o": prm["w_pool_o"][l].astype(BF16),
        "w_nat_o": prm["w_nat_o"][l].astype(BF16), "w_out": prm["w_out"][l].astype(BF16),
        "w_ffn_in": prm["w_ffn_in"][l].astype(BF16), "w_ffn_out": prm["w_ffn_out"][l].astype(BF16),
    }


def kernel(x, c, ctx, c_ctx, w_mod, b_mod, norm1, norm2, w_in, mu_prev, mu_next, decay_w0, decay_up, iclr_a0, iclr_up, gate_up, k_k, k_a, r_k, gn_w, gn_b, pool_w, pool_scale, q_gain, k_gain, rpb, w_rwkv_o, w_pool_o, w_nat_o, w_out, w_ffn_in, w_ffn_out):
    prm = dict(norm1=norm1, norm2=norm2, w_in=w_in, mu_prev=mu_prev, mu_next=mu_next, decay_w0=decay_w0,
               decay_up=decay_up, iclr_a0=iclr_a0, iclr_up=iclr_up, gate_up=gate_up, k_k=k_k, k_a=k_a, r_k=r_k,
               gn_w=gn_w, gn_b=gn_b, pool_w=pool_w, pool_scale=pool_scale, q_gain=q_gain, k_gain=k_gain, rpb=rpb,
               w_rwkv_o=w_rwkv_o, w_pool_o=w_pool_o, w_nat_o=w_nat_o, w_out=w_out, w_ffn_in=w_ffn_in,
               w_ffn_out=w_ffn_out)
    bsz, len_lat, d = x.shape
    len_ctx = ctx.shape[1]
    depth = w_mod.shape[0]
    assert len_ctx % TM == 0 and len_lat % TM == 0 and bsz + 1 <= SUBLANES
    nct = len_ctx // TM

    s_rows = jnp.concatenate([c, c_ctx[None, :], jnp.zeros((SUBLANES - bsz - 1, d), F32)], axis=0)
    mod_all = _modulation(s_rows, w_mod, b_mod)
    xa = jnp.concatenate([ctx, x], axis=1)

    for l in range(depth):
        lw = _layer_weights(l, prm)
        m_lat = mod_all[l, :bsz]
        m_ctx = jnp.broadcast_to(mod_all[l, bsz][None], m_lat.shape)
        mod = jnp.stack([m_ctx, m_lat], axis=1).reshape(bsz * 2 * 6, 1, d)

        p = _in_proj(xa, mod, lw["norm1"], lw["w_mix"], len_ctx, F32)
        gates = _in_proj(xa, mod, lw["norm1"], lw["w_gate"], len_ctx, BF16)
        ld_f, ld_b, kd_f, kd_b, b_f, b_b, v, kk, r, gd, b_br, qn, kn, vn = _mixer_prep(p, lw, nct, len_ctx, len_lat)
        y_f, y_b = _rwkv_scan(ld_f, ld_b, kd_f, kd_b, b_f, b_b, v, kk, r, len_ctx)
        c_br = _nat_attention(qn, kn, vn, lw["nat_bias"], len_ctx, len_lat)
        xa = _merge((y_f, y_b, r, kd_f, kd_b, v, gd), b_br, c_br, gates, xa, mod, lw, len_ctx)
        xa = _ffn(xa, mod, lw, len_ctx)
    return xa[:, len_ctx:]
```

```python
import functools

import numpy as np
import jax
import jax.numpy as jnp
from jax import lax
from jax.experimental import pallas as pl
from jax.experimental.pallas import tpu as pltpu

F32 = jnp.float32
BF16 = jnp.bfloat16

HEAD = 64
NORM_EPS = 1e-6
GN_EPS = 64e-5
KK_EPS = 1e-24
POOL_WINDOWS = (2, 4, 8, 16)
POOL_GROUP = 64
GRID_W = 64
WIN_H = 8
WIN_W = 16
MASK_BIAS = -1e30

LANES = 128
SUBLANES = 8
VMEM_LIMIT = 56 * 1024 * 1024

TM = 256
NAT_ROWS = 4
CHUNK = 64
SCAN_BLOCK = 256
TM_IN = 640
TM_ROW = 640
TN_IN = 3072
TN_MOD = 1536


def _dot_hi(a, b):
    return jnp.dot(a, b, precision=lax.Precision.HIGHEST, preferred_element_type=F32)


def _dot_bf(a, b):
    return jnp.dot(a.astype(BF16), b.astype(BF16), preferred_element_type=F32)


def _dot_nt(a, b):
    return lax.dot_general(a, b, (((1,), (1,)), ((), ())), preferred_element_type=F32)


def _params(*sem):
    return pltpu.CompilerParams(dimension_semantics=sem, vmem_limit_bytes=VMEM_LIMIT)


def _const_spec(shape):
    nd = len(shape)
    return pl.BlockSpec(shape, lambda *_: (0,) * nd, pipeline_mode=pl.Buffered(1))


def _mod_kernel(s_ref, w_ref, b_ref, o_ref):
    s = s_ref[...]
    s = s * jax.nn.sigmoid(s)
    o_ref[0] = _dot_hi(s, w_ref[0]) + b_ref[0]


def _modulation(s_rows, w_mod, b_mod):
    depth, d, n = w_mod.shape
    return pl.pallas_call(
        _mod_kernel,
        grid=(depth, n // TN_MOD),
        in_specs=[
            pl.BlockSpec((SUBLANES, d), lambda l, j: (0, 0)),
            pl.BlockSpec((1, d, TN_MOD), lambda l, j: (l, 0, j)),
            pl.BlockSpec((1, 1, TN_MOD), lambda l, j: (l, 0, j)),
        ],
        out_specs=pl.BlockSpec((1, SUBLANES, TN_MOD), lambda l, j: (l, 0, j)),
        out_shape=jax.ShapeDtypeStruct((depth, SUBLANES, n), F32),
        compiler_params=_params("parallel", "parallel"),
        name="modulation",
    )(s_rows, w_mod, b_mod.reshape(depth, 1, n))


def _norm_mod(x, gain, shift, scale):
    ms = jnp.mean(x * x, axis=-1, keepdims=True)
    return x * lax.rsqrt(ms + NORM_EPS) * gain * (1.0 + scale) + shift


def _in_proj_kernel(len_ctx, x_ref, sh_c, sc_c, sh_l, sc_l, g_ref, w_ref, o_ref):
    tm = x_ref.shape[1]
    h = _norm_mod(x_ref[0], g_ref[...], _row_mod(len_ctx, tm, sh_c, sh_l, axis=2), _row_mod(len_ctx, tm, sc_c, sc_l, axis=2))
    o_ref[0] = jnp.dot(h.astype(BF16), w_ref[...], preferred_element_type=F32).astype(o_ref.dtype)


def _in_proj(x, mod, gain, w, len_ctx, out_dtype):
    bsz, n, d = x.shape
    nout = w.shape[1]
    assert n % TM_IN == 0 and nout % TN_IN == 0
    mspec = lambda is_lat, k: pl.BlockSpec((1, 1, d), lambda j, b, i: ((b * 2 + is_lat) * 6 + k, 0, 0))
    return pl.pallas_call(
        functools.partial(_in_proj_kernel, len_ctx),
        grid=(nout // TN_IN, bsz, n // TM_IN),
        in_specs=[
            pl.BlockSpec((1, TM_IN, d), lambda j, b, i: (b, i, 0)),
            mspec(0, 0), mspec(0, 1), mspec(1, 0), mspec(1, 1),
            pl.BlockSpec((1, d), lambda j, b, i: (0, 0)),
            pl.BlockSpec((d, TN_IN), lambda j, b, i: (0, j)),
        ],
        out_specs=pl.BlockSpec((1, TM_IN, TN_IN), lambda j, b, i: (b, i, j)),
        out_shape=jax.ShapeDtypeStruct((bsz, n, nout), out_dtype),
        compiler_params=_params("parallel", "parallel", "parallel"),
        name="in_proj",
    )(x, mod, mod, mod, mod, gain, w)


def _halo_specs(width, col, n):
    nb = n // SUBLANES
    per = TM // SUBLANES
    return [
        pl.BlockSpec((1, TM, width), lambda b, i: (b, i, col)),
        pl.BlockSpec((1, SUBLANES, width), lambda b, i: (b, jnp.maximum(i * per - 1, 0), col)),
        pl.BlockSpec((1, SUBLANES, width), lambda b, i: (b, jnp.minimum((i + 1) * per, nb - 1), col)),
    ]


def _prep_kernel(nct, nt, rkv_ref, rkv_p, rkv_n, lo_ref, lo_p, lo_n, gd_ref, gd_p, gd_n,
                 mup_ref, mun_ref, du_ref, w0_ref, au_ref, a0_ref, kkw_ref, ka_ref, ones_ref,
                 wf_ref, wb_ref, kdf_ref, kdb_ref, bf_ref, bb_ref, v_ref, kk_ref, r_ref, gdo_ref):
    i = pl.program_id(1)
    first = jnp.logical_or(i == 0, i == nct)
    last = jnp.logical_or(i == nct - 1, i == nt - 1)
    row = lax.broadcasted_iota(jnp.int32, (TM, 1), 0)
    c = rkv_ref.shape[2] // 3

    def mix(main, prev8, next8, lo, hi):
        pm = main[0]
        prow = jnp.where(first, 0.0, prev8[0, SUBLANES - 1:SUBLANES, :])
        nrow = jnp.where(last, 0.0, next8[0, 0:1, :])
        prev = jnp.where(row == 0, prow, pltpu.roll(pm, 1, 0))
        nxt = jnp.where(row == TM - 1, nrow, pltpu.roll(pm, TM - 1, 0))
        return pm + mup_ref[:, lo:hi] * (prev - pm) + mun_ref[:, lo:hi] * (nxt - pm)

    rkv = mix(rkv_ref, rkv_p, rkv_n, 0, 3 * c)
    lora = mix(lo_ref, lo_p, lo_n, 3 * c, 3 * c + 2 * LANES)
    gdo_ref[0] = mix(gd_ref, gd_p, gd_n, 3 * c + 2 * LANES, 3 * c + 4 * LANES)

    r, k, v = rkv[:, 0:c], rkv[:, c:2 * c], rkv[:, 2 * c:3 * c]
    wd, ad = lora[:, 0:LANES], lora[:, LANES:2 * LANES]
    dec = _mm(jnp.tanh(wd), du_ref[...], split_a=True, split_b=True) + w0_ref[...]
    log_decay = -float(np.exp(-0.5)) * jax.nn.sigmoid(dec)
    a = jax.nn.sigmoid(_mm(ad, au_ref[...], split_a=True, split_b=True) + a0_ref[...])
    kk = k * kkw_ref[...]
    kk = kk * lax.rsqrt(jnp.maximum(_mm(kk * kk, ones_ref[...]), KK_EPS))
    ka = ka_ref[...]
    for z, (w_o, kd_o, b_o) in enumerate(((wf_ref, kdf_ref, bf_ref), (wb_ref, kdb_ref, bb_ref))):
        az = a[:, z * c:(z + 1) * c]
        w_o[0] = log_decay[:, z * c:(z + 1) * c]
        kd_o[0] = k * (1.0 + (az - 1.0) * ka)
        b_o[0] = kk * az
    v_ref[0] = v
    kk_ref[0] = kk
    r_ref[0] = r


def _mixer_prep_kernel(nct, nt, len_ctx, len_lat, *refs):
    n_in = (18, 5, 6)
    n_out = (10, 1, 3)
    cuts = np.cumsum(n_in + n_out).tolist()
    prep_in, pool_in, qk_in, prep_out, pool_out, qk_out = [refs[a:b] for a, b in zip([0] + cuts[:-1], cuts)]
    _prep_kernel(nct, nt, *prep_in, *prep_out)
    _pool_kernel(nct, len_ctx, len_lat, *pool_in, *pool_out)
    _qk_norm_kernel(*qk_in, *qk_out)


def _mixer_prep(p, lw, nct, len_ctx, len_lat):
    bsz, n, _ = p.shape
    c = lw["k_k"].shape[1]
    nt = n // TM
    wide = 2 * LANES
    lora_col = (6 * c) // wide
    prep_consts = [lw["mu_prev"], lw["mu_next"], lw["decay_up"], lw["decay_w0"], lw["iclr_up"], lw["iclr_a0"],
                   lw["k_k"], lw["k_a"], lw["ones_bd"]]
    pool_consts = [lw["pool_w"], lw["pool_scale"]]
    qk_consts = [lw["q_gain"], lw["k_gain"], lw["ones_bd"]]
    consts = lambda arrs: [_const_spec(a.shape) for a in arrs]
    tok = lambda w: pl.BlockSpec((1, TM, w), lambda b, i: (b, i, 0))
    heads = c // HEAD
    head_major = pl.BlockSpec((1, heads, TM, HEAD), lambda b, i: (b, 0, i, 0))
    return pl.pallas_call(
        functools.partial(_mixer_prep_kernel, nct, nt, len_ctx, len_lat),
        grid=(bsz, nt),
        in_specs=_halo_specs(3 * c, 0, n) + _halo_specs(wide, lora_col, n) + _halo_specs(wide, lora_col + 1, n)
        + consts(prep_consts) + _halo_specs(wide, lora_col + 2, n) + consts(pool_consts)
        + [pl.BlockSpec((1, TM, c), lambda b, i, j=j: (b, i, 3 + j)) for j in range(3)] + consts(qk_consts),
        out_specs=[tok(c)] * 9 + [tok(wide), tok(wide)] + [head_major] * 3,
        out_shape=[jax.ShapeDtypeStruct((bsz, n, c), F32)] * 9 + [jax.ShapeDtypeStruct((bsz, n, wide), F32)] * 2
        + [jax.ShapeDtypeStruct((bsz, heads, n, HEAD), BF16)] * 3,
        compiler_params=_params("parallel", "parallel"),
        name="mixer_prep",
    )(*[p] * 9, *prep_consts, *[p] * 3, *pool_consts, *[p] * 3, *qk_consts)


def _split(a):
    hi = a.astype(BF16)
    return hi, (a - hi.astype(F32)).astype(BF16)


def _mm(a, b, split_a=False, split_b=False):
    def halves(x, split):
        if x.dtype == BF16 or not split:
            return x.astype(BF16), None
        return _split(x)

    (ah, al), (bh, bl) = halves(a, split_a), halves(b, split_b)
    lhs, rhs = [ah], [bh]
    if al is not None:
        lhs.append(al)
        rhs.append(bh)
    if bl is not None:
        lhs.append(ah)
        rhs.append(bl)
    if len(lhs) == 1:
        return jnp.dot(ah, bh, preferred_element_type=F32)
    return jnp.dot(jnp.concatenate(lhs, axis=1), jnp.concatenate(rhs, axis=0), preferred_element_type=F32)


def _chunk_scan_kernel(ldf, kdf, bf, vf, kkf, rf, ldb, kdb, bb, vb, kkb, rb, yf_ref, yb_ref, h_ref):
    @pl.when(pl.program_id(0) == 0)
    def _():
        h_ref[...] = jnp.zeros_like(h_ref)

    bsz, block_len, c = ldf.shape
    cs = CHUNK
    dirs = ((ldf, kdf, bf, vf, kkf, rf, yf_ref), (ldb, kdb, bb, vb, kkb, rb, yb_ref))
    t_i = lax.broadcasted_iota(jnp.int32, (cs, LANES), 0)
    lane = lax.broadcasted_iota(jnp.int32, (cs, LANES), 1)
    s_i = jnp.bitwise_and(lane, HEAD - 1)
    m_a = (lane < HEAD).astype(F32).astype(BF16)
    m_b = (lane >= HEAD).astype(F32).astype(BF16)
    r2 = lax.broadcasted_iota(jnp.int32, (LANES, LANES), 0)
    l2 = lax.broadcasted_iota(jnp.int32, (LANES, LANES), 1)
    block = ((r2 >= HEAD) == (l2 >= HEAD)).astype(F32)
    eye = (r2 == l2).astype(F32)
    zeros = jnp.zeros((cs, LANES), BF16)

    def bd(x):
        k = x.shape[1] // LANES
        return jnp.concatenate([x * jnp.concatenate([m_a] * k, axis=1), x * jnp.concatenate([m_b] * k, axis=1)],
                               axis=0)

    def unit(d, b, p, rows):
        ls = slice(p * LANES, (p + 1) * LANES)
        ld, kd, b_, v, kk, r = (ref[b, rows, ls] for ref in dirs[d][:6])
        before = (s_i < t_i) if d == 0 else (s_i > t_i)
        upto = (s_i <= t_i) if d == 0 else (s_i >= t_i)
        tri = upto.astype(F32).astype(BF16)
        lh, ll = _split(ld)
        big_l = jnp.dot(tri, jnp.concatenate([lh, ll], axis=0), preferred_element_type=F32)
        yield
        ltot = big_l[cs - 1:cs] if d == 0 else big_l[0:1]
        kap = kk * jnp.exp(big_l - ld)
        rt = r * jnp.exp(big_l)
        einv = jnp.exp(-big_l)
        kt, bt = kd * einv, b_ * einv
        efin = jnp.exp(ltot - big_l)
        khat, bhat = kd * efin, b_ * efin
        ktb, btb, vb16 = kt.astype(BF16), bt.astype(BF16), v.astype(BF16)
        sc = _dot_nt(jnp.concatenate([kap, rt], axis=0).astype(BF16),
                     jnp.concatenate([ktb * m_a, ktb * m_b, btb * m_a, btb * m_b], axis=0))
        yield
        a_k = jnp.where(before, sc[:cs, :LANES], 0.0)
        n_p = jnp.where(before, -sc[:cs, LANES:], 0.0)
        m_k = jnp.where(upto, sc[cs:, :LANES], 0.0)
        m_nb = jnp.where(upto, -sc[cs:, LANES:], 0.0)
        x = jnp.concatenate([kap, _mm(a_k, bd(vb16))], axis=1)
        yield
        for level in range(6):
            xh, xl = _split(x)
            nb = n_p.astype(BF16)
            x = x + jnp.dot(jnp.concatenate([nb, nb], axis=1), jnp.concatenate([bd(xh), bd(xl)], axis=0),
                            preferred_element_type=F32)
            if level < 5:
                n_p = jnp.dot(nb, bd(nb), preferred_element_type=F32)
            yield
        vz = jnp.concatenate([vb16, zeros], axis=1)
        uw = jnp.concatenate([x[:, LANES:], x[:, :LANES]], axis=1).astype(BF16)
        o1 = _mm(jnp.concatenate([m_k, m_nb], axis=1), jnp.concatenate([bd(vz), bd(uw)], axis=0))
        o2 = _mm(jnp.concatenate([khat, -bhat], axis=0).T, jnp.concatenate([vz, uw], axis=0))
        yield
        h = h_ref[b, d, p]
        y0, q = o1[:, :LANES], rt + o1[:, LANES:]
        psi = o2[:, :LANES] * block
        phi = o2[:, LANES:] * block + eye * jnp.exp(ltot)
        dirs[d][6][b, rows, ls] = y0 + _mm(q, h)
        h_ref[b, d, p] = _mm(phi, h, split_a=True, split_b=True) + psi

    n_sub = block_len // cs
    units = {}
    for s in range(n_sub):
        for d in range(2):
            ci = s if d == 0 else n_sub - 1 - s
            for b in range(bsz):
                for p in range(c // LANES):
                    units[s, d, b, p] = unit(d, b, p, slice(ci * cs, (ci + 1) * cs))
    _run_interleaved(units)


def _rwkv_scan(ld_f, ld_b, kd_f, kd_b, b_f, b_b, v, kk, r, len_ctx):
    bsz, n, c = v.shape
    assert len_ctx % SCAN_BLOCK == 0 and n % SCAN_BLOCK == 0 and SCAN_BLOCK % CHUNK == 0
    nc, nct_c = n // SCAN_BLOCK, len_ctx // SCAN_BLOCK

    def rev(g):
        return jnp.where(g < nct_c, nct_c - 1 - g, nc - 1 - g + nct_c)

    fwd = pl.BlockSpec((bsz, SCAN_BLOCK, c), lambda g: (0, g, 0))
    bwd = pl.BlockSpec((bsz, SCAN_BLOCK, c), lambda g: (0, rev(g), 0))
    tok = jax.ShapeDtypeStruct((bsz, n, c), F32)
    return pl.pallas_call(
        _chunk_scan_kernel,
        grid=(nc,),
        in_specs=[fwd] * 6 + [bwd] * 6,
        out_specs=[fwd, bwd],
        out_shape=[tok, tok],
        scratch_shapes=[pltpu.VMEM((bsz, 2, c // LANES, LANES, LANES), F32)],
        compiler_params=_params("arbitrary"),
        name="rwkv_scan",
    )(ld_f, kd_f, b_f, v, kk, r, ld_b, kd_b, b_b, v, kk, r)


def _readout(yf, yb, r, kdf, kdb, v, gd, gnw, gnb, rk, gup, ones_ref):
    ones = ones_ref[...]
    inv = 1.0 / HEAD
    y = yf[0] + yb[0]
    yc = y - _mm(y, ones, split_a=True) * inv
    var = _mm(yc * yc, ones) * inv
    yn = yc * lax.rsqrt(var + GN_EPS) * gnw[...] + gnb[...]
    bonus = _mm(r[0] * rk[...] * (kdf[0] + kdb[0]), ones) * v[0]
    return (yn + bonus) * _mm(jax.nn.sigmoid(gd[0]), gup[...])


def _pool_kernel(nct, len_ctx, len_lat, main, prev8, next8, pw_ref, scale_ref, o_ref):
    i = pl.program_id(1)
    is_lat = i >= nct
    seq_len = jnp.where(is_lat, len_lat, len_ctx)
    t0 = jnp.where(is_lat, i - nct, i) * TM
    n = TM + 2 * SUBLANES
    pm = main[0]
    ext = jnp.concatenate([prev8[0], pm, next8[0]], axis=0)
    pos = t0 - SUBLANES + lax.broadcasted_iota(jnp.int32, (n, 1), 0)
    e = jnp.where(jnp.logical_and(pos >= 0, pos < seq_len), ext, 0.0)
    a2 = e + pltpu.roll(e, 1, 0)
    a4 = pltpu.roll(a2, 1, 0) + pltpu.roll(a2, n - 1, 0)
    a8 = pltpu.roll(a4, 2, 0) + pltpu.roll(a4, n - 2, 0)
    a16 = pltpu.roll(a8, 4, 0) + pltpu.roll(a8, n - 4, 0)
    t = t0 + lax.broadcasted_iota(jnp.int32, (TM, 1), 0)
    lane = lax.broadcasted_iota(jnp.int32, pm.shape, 1)
    mean = None
    for g, (w, acc) in reversed(list(enumerate(zip(POOL_WINDOWS, (a2, a4, a8, a16))))):
        lo = jnp.maximum(t - w // 2, 0)
        hi = jnp.minimum(t + (w - w // 2) - 1, seq_len - 1)
        m = acc[SUBLANES:SUBLANES + TM] / (hi - lo + 1).astype(F32)
        mean = m if mean is None else jnp.where(lane < (g + 1) * POOL_GROUP, m, mean)
    o_ref[0] = _mm(mean - pm, pw_ref[...], split_a=True, split_b=True) * scale_ref[...]


def _qk_norm_kernel(q_ref, k_ref, v_ref, qg_ref, kg_ref, ones_ref, qo, ko, vo):
    ones = ones_ref[...]
    inv = 1.0 / HEAD
    q, k = q_ref[0], k_ref[0]
    qn = q * lax.rsqrt(_mm(q * q, ones) * inv + NORM_EPS) * qg_ref[...]
    kn = k * lax.rsqrt(_mm(k * k, ones) * inv + NORM_EPS) * kg_ref[...]
    for o_ref, val in ((qo, qn * HEAD ** -0.5), (ko, kn), (vo, v_ref[0])):
        val = val.astype(BF16)
        for h in range(o_ref.shape[1]):
            o_ref[0, h] = val[:, h * HEAD:(h + 1) * HEAD]


def _run_interleaved(units):
    done = {}
    while units:
        for k in list(units):
            try:
                next(units[k])
            except StopIteration as stop:
                done[k] = stop.value
                del units[k]
    return done


def _attend(q, key_sets, bias):
    scores = [_dot_nt(q, k) for k, _ in key_sets]
    yield
    if bias is not None:
        scores[0] = scores[0] + bias
    m = functools.reduce(jnp.maximum, [jnp.max(s, axis=-1, keepdims=True) for s in scores])
    yield
    ps = [jnp.exp(s - m) for s in scores]
    den = functools.reduce(jnp.add, [jnp.sum(p, axis=-1, keepdims=True) for p in ps])
    num = functools.reduce(jnp.add, [jnp.dot(p.astype(BF16), v, preferred_element_type=F32)
                                      for p, (_, v) in zip(ps, key_sets)])
    yield
    return num / den


def _nat_kernel(rows, q_ref, kp, kc, kn, vp, vc, vn, kx_ref, vx_ref, bias_ref, o_ref, ks, vs):
    i = pl.program_id(1) - 1
    heads, tq = q_ref.shape[1], q_ref.shape[2]
    rb = tq // GRID_W
    nloc = WIN_H * GRID_W

    @pl.when(i < 0)
    def _():
        units = {h: _attend(q_ref[0, h], [(kx_ref[0, h], vx_ref[0, h])], None) for h in range(heads)}
        for h, o in _run_interleaved(units).items():
            o_ref[0, :, h * HEAD:(h + 1) * HEAD] = o

    @pl.when(i >= 0)
    def _():
        for j, (kr, vr) in enumerate(((kp, vp), (kc, vc), (kn, vn))):
            ks[:, j * tq:(j + 1) * tq, :] = kr[0]
            vs[:, j * tq:(j + 1) * tq, :] = vr[0]
        for r0 in range(0, rb, NAT_ROWS):
            units = {}
            for rr in range(r0, r0 + NAT_ROWS):
                r = i * rb + rr
                rs = jnp.clip(r - WIN_H // 2, 0, rows - WIN_H)
                off = r - rs
                start = pl.multiple_of((rs - i * rb + rb) * GRID_W, GRID_W)
                qs = slice(rr * GRID_W, (rr + 1) * GRID_W)
                for h in range(heads):
                    units[rr, h] = _attend(q_ref[0, h, qs],
                                           [(ks[h, pl.ds(start, nloc)], vs[h, pl.ds(start, nloc)]),
                                            (kx_ref[0, h], vx_ref[0, h])], bias_ref[h, off])
            for (rr, h), o in _run_interleaved(units).items():
                o_ref[0, rr * GRID_W:(rr + 1) * GRID_W, h * HEAD:(h + 1) * HEAD] = o


def _nat_attention(qn, kn, vn, bias, len_ctx, len_lat):
    bsz, heads, n, hd = qn.shape
    tq = len_ctx
    assert tq == (WIN_H // 2) * GRID_W and len_lat % tq == 0
    rows = len_lat // GRID_W
    nblk = len_lat // tq

    def blk(shift):
        return pl.BlockSpec((1, heads, tq, hd),
                            lambda b, i: (b, 0, jnp.where(i == 0, 0, 1 + jnp.clip(i - 1 + shift, 0, nblk - 1)), 0))

    ctx = pl.BlockSpec((1, heads, tq, hd), lambda b, i: (b, 0, 0, 0))
    return pl.pallas_call(
        functools.partial(_nat_kernel, rows),
        grid=(bsz, 1 + nblk),
        in_specs=[blk(0), blk(-1), blk(0), blk(1), blk(-1), blk(0), blk(1), ctx, ctx, _const_spec(bias.shape)],
        out_specs=pl.BlockSpec((1, tq, heads * hd), lambda b, i: (b, i, 0)),
        out_shape=jax.ShapeDtypeStruct((bsz, n, heads * hd), F32),
        scratch_shapes=[pltpu.VMEM((heads, 3 * tq, hd), BF16), pltpu.VMEM((heads, 3 * tq, hd), BF16)],
        compiler_params=_params("parallel", "parallel"),
        name="nat_attention",
    )(qn, kn, kn, kn, vn, vn, vn, kn, vn, bias)


def _nat_bias_table(rpb):
    qc = np.arange(GRID_W)[:, None]
    kc = np.arange(GRID_W)[None, :]
    cs = np.clip(qc - WIN_W // 2, 0, GRID_W - WIN_W)
    valid = (kc >= cs) & (kc < cs + WIN_W)
    dc = kc - qc + WIN_W - 1
    pick = (np.arange(2 * WIN_W - 1)[:, None, None] == dc[None]) & valid[None]
    cols = jnp.einsum("hdm,mqk->hdqk", rpb, jnp.asarray(pick, F32), precision=lax.Precision.HIGHEST)
    cols = jnp.where(valid[None, None], cols, MASK_BIAS)
    t = jnp.stack([cols[:, WIN_H - 1 - off:2 * WIN_H - 1 - off] for off in range(WIN_H)], axis=1)
    return t.transpose(0, 1, 3, 2, 4).reshape(rpb.shape[0], WIN_H, GRID_W, WIN_H * GRID_W)


def _row_mod(len_ctx, tm, ctx_ref, lat_ref, axis=1):
    row = pl.program_id(axis) * tm + lax.broadcasted_iota(jnp.int32, (tm, 1), 0)
    return jnp.where(row < len_ctx, ctx_ref[0], lat_ref[0])


def _row_mod_specs(d, k):
    return [pl.BlockSpec((1, 1, d), lambda b, i, s=s: ((b * 2 + s) * 6 + k, 0, 0)) for s in range(2)]


def _merge_kernel(len_ctx, *refs):
    a = _readout(*refs[:12])
    bp, cn, ga, gb, gc, x, g1c, g1l, wa, wb, wc, wo, o_ref = refs[12:]
    sig = lambda g: jax.nn.sigmoid(g[0].astype(F32))
    m = sig(ga) * _dot_bf(a, wa[...]) + sig(gb) * _dot_bf(bp[0], wb[...]) + sig(gc) * _dot_bf(cn[0], wc[...])
    o_ref[0] = x[0] + _row_mod(len_ctx, x.shape[1], g1c, g1l) * _dot_bf(m, wo[...])


def _merge(rwkv, bp, cn, gates, x, mod, lw, len_ctx):
    bsz, n, d = x.shape
    assert n % TM_ROW == 0
    tok = lambda w, col=0: pl.BlockSpec((1, TM_ROW, w), lambda b, i: (b, i, col))
    consts = [lw["gn_w"], lw["gn_b"], lw["r_k"], lw["gate_up"], lw["ones_bd"]]
    ws = [lw["w_rwkv_o"], lw["w_pool_o"], lw["w_nat_o"], lw["w_out"]]
    return pl.pallas_call(
        functools.partial(_merge_kernel, len_ctx),
        grid=(bsz, n // TM_ROW),
        in_specs=[tok(a.shape[2]) for a in rwkv] + [_const_spec(a.shape) for a in consts]
        + [tok(bp.shape[2]), tok(cn.shape[2]), tok(d, 0), tok(d, 1), tok(d, 2), tok(d)]
        + _row_mod_specs(d, 2) + [_const_spec(w.shape) for w in ws],
        out_specs=tok(d),
        out_shape=jax.ShapeDtypeStruct((bsz, n, d), F32),
        compiler_params=_params("parallel", "parallel"),
        name="merge",
    )(*rwkv, *consts, bp, cn, gates, gates, gates, x, mod, mod, *ws)


def _ffn_kernel(len_ctx, x_ref, sh_c, sh_l, sc_c, sc_l, g2_c, g2_l, gain, w1, w2, o_ref):
    x = x_ref[0]
    tm = x.shape[0]
    h = _norm_mod(x, gain[...], _row_mod(len_ctx, tm, sh_c, sh_l), _row_mod(len_ctx, tm, sc_c, sc_l))
    u = jnp.dot(h.astype(BF16), w1[...], preferred_element_type=F32)
    hid = w2.shape[0]
    gate, up = u[:, :hid], u[:, hid:]
    act = gate * jax.nn.sigmoid(gate) * up
    o_ref[0] = x + _row_mod(len_ctx, tm, g2_c, g2_l) * jnp.dot(act.astype(BF16), w2[...],
                                                              preferred_element_type=F32)


def _ffn(x, mod, lw, len_ctx):
    bsz, n, d = x.shape
    assert n % TM_ROW == 0
    tok = pl.BlockSpec((1, TM_ROW, d), lambda b, i: (b, i, 0))
    return pl.pallas_call(
        functools.partial(_ffn_kernel, len_ctx),
        grid=(bsz, n // TM_ROW),
        in_specs=[tok] + _row_mod_specs(d, 3) + _row_mod_specs(d, 4) + _row_mod_specs(d, 5)
        + [_const_spec(lw["norm2"].shape), _const_spec(lw["w_ffn_in"].shape), _const_spec(lw["w_ffn_out"].shape)],
        out_specs=tok,
        out_shape=jax.ShapeDtypeStruct((bsz, n, d), F32),
        compiler_params=_params("parallel", "parallel"),
        name="ffn",
    )(x, *[mod] * 6, lw["norm2"], lw["w_ffn_in"], lw["w_ffn_out"])


def _block_diag(blocks):
    n = len(blocks)
    rows = []
    for i, blk in enumerate(blocks):
        rows.append(jnp.concatenate([blk if j == i else jnp.zeros((blk.shape[0], blocks[j].shape[1]), blk.dtype)
                                     for j in range(n)], axis=1))
    return jnp.concatenate(rows, axis=0)


def _pad_cols(a, width):
    return jnp.pad(a, ((0, 0), (0, width - a.shape[1])))


def _layer_weights(l, prm):
    c = prm["k_k"].shape[1]
    lora = prm["decay_up"].shape[2]
    gl = prm["gate_up"].shape[1]
    pool = prm["pool_scale"].shape[1]
    d = prm["w_out"].shape[1]
    w_in = prm["w_in"][l]
    o_lora, o_gd, o_pool = 3 * c, 3 * c + 4 * lora, 3 * c + 4 * lora + gl
    o_q = o_pool + pool
    o_gate = o_q + 3 * c
    assert 4 * lora == 2 * LANES and gl <= 2 * LANES and pool == 2 * LANES and o_gate + 3 * d == w_in.shape[1]
    w_mix = jnp.concatenate([
        w_in[:, 0:o_lora], w_in[:, o_q:o_gate], w_in[:, o_lora:o_gd],
        _pad_cols(w_in[:, o_gd:o_pool], 2 * LANES), w_in[:, o_pool:o_q]], axis=1).astype(BF16)
    mu = lambda m: _pad_cols(m[l][None, :o_pool], o_pool + 2 * LANES - gl)
    heads = c // HEAD
    row = lambda a: a.reshape(1, -1)
    return {
        "w_mix": w_mix, "w_gate": w_in[:, o_gate:].astype(BF16),
        "norm1": row(prm["norm1"][l]), "norm2": row(prm["norm2"][l]),
        "mu_prev": mu(prm["mu_prev"]), "mu_next": mu(prm["mu_next"]),
        "decay_up": _block_diag([prm["decay_up"][l, 0], prm["decay_up"][l, 1]]),
        "decay_w0": row(prm["decay_w0"][l]),
        "iclr_up": _block_diag([prm["iclr_up"][l, 0], prm["iclr_up"][l, 1]]),
        "iclr_a0": row(prm["iclr_a0"][l]),
        "k_k": row(prm["k_k"][l]), "k_a": row(prm["k_a"][l]), "r_k": row(prm["r_k"][l]),
        "gn_w": row(prm["gn_w"][l]), "gn_b": row(prm["gn_b"][l]),
        "gate_up": jnp.pad(prm["gate_up"][l], ((0, 2 * LANES - gl), (0, 0))),
        "ones_bd": jnp.kron(jnp.eye(heads, dtype=F32), jnp.ones((HEAD, HEAD), F32)).astype(BF16),
        "pool_w": _block_diag([prm["pool_w"][l, g] for g in range(len(POOL_WINDOWS))]),
        "pool_scale": row(prm["pool_scale"][l]),
        "q_gain": row(jnp.tile(prm["q_gain"][l], heads)), "k_gain": row(jnp.tile(prm["k_gain"][l], heads)),
        "nat_bias": _nat_bias_table(prm["rpb"][l]),
        "w_rwkv_o": prm["w_rwkv_o"][l].astype(BF16), "w_pool_o": prm["w_pool_o"][l].astype(BF16),
        "w_nat_o": prm["w_nat_o"][l].astype(BF16), "w_out": prm["w_out"][l].astype(BF16),
        "w_ffn_in": prm["w_ffn_in"][l].astype(BF16), "w_ffn_out": prm["w_ffn_out"][l].astype(BF16),
    }


def kernel(x, c, ctx, c_ctx, w_mod, b_mod, norm1, norm2, w_in, mu_prev, mu_next, decay_w0, decay_up, iclr_a0, iclr_up, gate_up, k_k, k_a, r_k, gn_w, gn_b, pool_w, pool_scale, q_gain, k_gain, rpb, w_rwkv_o, w_pool_o, w_nat_o, w_out, w_ffn_in, w_ffn_out):
    prm = dict(norm1=norm1, norm2=norm2, w_in=w_in, mu_prev=mu_prev, mu_next=mu_next, decay_w0=decay_w0,
               decay_up=decay_up, iclr_a0=iclr_a0, iclr_up=iclr_up, gate_up=gate_up, k_k=k_k, k_a=k_a, r_k=r_k,
               gn_w=gn_w, gn_b=gn_b, pool_w=pool_w, pool_scale=pool_scale, q_gain=q_gain, k_gain=k_gain, rpb=rpb,
               w_rwkv_o=w_rwkv_o, w_pool_o=w_pool_o, w_nat_o=w_nat_o, w_out=w_out, w_ffn_in=w_ffn_in,
               w_ffn_out=w_ffn_out)
    bsz, len_lat, d = x.shape
    len_ctx = ctx.shape[1]
    depth = w_mod.shape[0]
    assert len_ctx % TM == 0 and len_lat % TM == 0 and bsz + 1 <= SUBLANES
    nct = len_ctx // TM

    s_rows = jnp.concatenate([c, c_ctx[None, :], jnp.zeros((SUBLANES - bsz - 1, d), F32)], axis=0)
    mod_all = _modulation(s_rows, w_mod, b_mod)
    xa = jnp.concatenate([ctx, x], axis=1)

    for l in range(depth):
        lw = _layer_weights(l, prm)
        m_lat = mod_all[l, :bsz]
        m_ctx = jnp.broadcast_to(mod_all[l, bsz][None], m_lat.shape)
        mod = jnp.stack([m_ctx, m_lat], axis=1).reshape(bsz * 2 * 6, 1, d)

        p = _in_proj(xa, mod, lw["norm1"], lw["w_mix"], len_ctx, F32)
        gates = _in_proj(xa, mod, lw["norm1"], lw["w_gate"], len_ctx, BF16)
        ld_f, ld_b, kd_f, kd_b, b_f, b_b, v, kk, r, gd, b_br, qn, kn, vn = _mixer_prep(p, lw, nct, len_ctx, len_lat)
        y_f, y_b = _rwkv_scan(ld_f, ld_b, kd_f, kd_b, b_f, b_b, v, kk, r, len_ctx)
        c_br = _nat_attention(qn, kn, vn, lw["nat_bias"], len_ctx, len_lat)
        xa = _merge((y_f, y_b, r, kd_f, kd_b, v, gd), b_br, c_br, gates, xa, mod, lw, len_ctx)
        xa = _ffn(xa, mod, lw, len_ctx)
    return xa[:, len_ctx:]
```

```python
import functools

import numpy as np
import jax
import jax.numpy as jnp
from jax import lax
from jax.experimental import pallas as pl
from jax.experimental.pallas import tpu as pltpu

F32 = jnp.float32
BF16 = jnp.bfloat16

HEAD = 64
NORM_EPS = 1e-6
GN_EPS = 64e-5
KK_EPS = 1e-24
POOL_WINDOWS = (2, 4, 8, 16)
POOL_GROUP = 64
GRID_W = 64
WIN_H = 8
WIN_W = 16
MASK_BIAS = -1e30

LANES = 128
SUBLANES = 8
VMEM_LIMIT = 56 * 1024 * 1024

TM = 256
NAT_ROWS = 4
CHUNK = 64
SCAN_BLOCK = 256
TM_IN = 640
TM_ROW = 640
TN_IN = 3072
TN_MOD = 1536


def _dot_hi(a, b):
    return jnp.dot(a, b, precision=lax.Precision.HIGHEST, preferred_element_type=F32)


def _dot_bf(a, b):
    return jnp.dot(a.astype(BF16), b.astype(BF16), preferred_element_type=F32)


def _dot_nt(a, b):
    return lax.dot_general(a, b, (((1,), (1,)), ((), ())), preferred_element_type=F32)


def _params(*sem):
    return pltpu.CompilerParams(dimension_semantics=sem, vmem_limit_bytes=VMEM_LIMIT)


def _const_spec(shape):
    nd = len(shape)
    return pl.BlockSpec(shape, lambda *_: (0,) * nd, pipeline_mode=pl.Buffered(1))


def _mod_kernel(s_ref, w_ref, b_ref, o_ref):
    s = s_ref[...]
    s = s * jax.nn.sigmoid(s)
    o_ref[0] = _dot_hi(s, w_ref[0]) + b_ref[0]


def _modulation(s_rows, w_mod, b_mod):
    depth, d, n = w_mod.shape
    return pl.pallas_call(
        _mod_kernel,
        grid=(depth, n // TN_MOD),
        in_specs=[
            pl.BlockSpec((SUBLANES, d), lambda l, j: (0, 0)),
            pl.BlockSpec((1, d, TN_MOD), lambda l, j: (l, 0, j)),
            pl.BlockSpec((1, 1, TN_MOD), lambda l, j: (l, 0, j)),
        ],
        out_specs=pl.BlockSpec((1, SUBLANES, TN_MOD), lambda l, j: (l, 0, j)),
        out_shape=jax.ShapeDtypeStruct((depth, SUBLANES, n), F32),
        compiler_params=_params("parallel", "parallel"),
        name="modulation",
    )(s_rows, w_mod, b_mod.reshape(depth, 1, n))


def _norm_mod(x, gain, shift, scale):
    ms = jnp.mean(x * x, axis=-1, keepdims=True)
    return x * lax.rsqrt(ms + NORM_EPS) * gain * (1.0 + scale) + shift


def _in_proj_kernel(len_ctx, x_ref, sh_c, sc_c, sh_l, sc_l, g_ref, w_ref, o_ref):
    tm = x_ref.shape[1]
    h = _norm_mod(x_ref[0], g_ref[...], _row_mod(len_ctx, tm, sh_c, sh_l, axis=2), _row_mod(len_ctx, tm, sc_c, sc_l, axis=2))
    o_ref[0] = jnp.dot(h.astype(BF16), w_ref[...], preferred_element_type=F32).astype(o_ref.dtype)


def _in_proj(x, mod, gain, w, len_ctx, out_dtype):
    bsz, n, d = x.shape
    nout = w.shape[1]
    assert n % TM_IN == 0 and nout % TN_IN == 0
    mspec = lambda is_lat, k: pl.BlockSpec((1, 1, d), lambda j, b, i: ((b * 2 + is_lat) * 6 + k, 0, 0))
    return pl.pallas_call(
        functools.partial(_in_proj_kernel, len_ctx),
        grid=(nout // TN_IN, bsz, n // TM_IN),
        in_specs=[
            pl.BlockSpec((1, TM_IN, d), lambda j, b, i: (b, i, 0)),
            mspec(0, 0), mspec(0, 1), mspec(1, 0), mspec(1, 1),
            pl.BlockSpec((1, d), lambda j, b, i: (0, 0)),
            pl.BlockSpec((d, TN_IN), lambda j, b, i: (0, j)),
        ],
        out_specs=pl.BlockSpec((1, TM_IN, TN_IN), lambda j, b, i: (b, i, j)),
        out_shape=jax.ShapeDtypeStruct((bsz, n, nout), out_dtype),
        compiler_params=_params("parallel", "parallel", "parallel"),
        name="in_proj",
    )(x, mod, mod, mod, mod, gain, w)


def _halo_specs(width, col, n):
    nb = n // SUBLANES
    per = TM // SUBLANES
    return [
        pl.BlockSpec((1, TM, width), lambda b, i: (b, i, col)),
        pl.BlockSpec((1, SUBLANES, width), lambda b, i: (b, jnp.maximum(i * per - 1, 0), col)),
        pl.BlockSpec((1, SUBLANES, width), lambda b, i: (b, jnp.minimum((i + 1) * per, nb - 1), col)),
    ]


def _prep_kernel(nct, nt, rkv_ref, rkv_p, rkv_n, lo_ref, lo_p, lo_n, gd_ref, gd_p, gd_n,
                 mup_ref, mun_ref, du_ref, w0_ref, au_ref, a0_ref, kkw_ref, ka_ref, ones_ref,
                 wf_ref, wb_ref, kdf_ref, kdb_ref, bf_ref, bb_ref, v_ref, kk_ref, r_ref, gdo_ref):
    i = pl.program_id(1)
    first = jnp.logical_or(i == 0, i == nct)
    last = jnp.logical_or(i == nct - 1, i == nt - 1)
    row = lax.broadcasted_iota(jnp.int32, (TM, 1), 0)
    c = rkv_ref.shape[2] // 3

    def mix(main, prev8, next8, lo, hi):
        pm = main[0]
        prow = jnp.where(first, 0.0, prev8[0, SUBLANES - 1:SUBLANES, :])
        nrow = jnp.where(last, 0.0, next8[0, 0:1, :])
        prev = jnp.where(row == 0, prow, pltpu.roll(pm, 1, 0))
        nxt = jnp.where(row == TM - 1, nrow, pltpu.roll(pm, TM - 1, 0))
        return pm + mup_ref[:, lo:hi] * (prev - pm) + mun_ref[:, lo:hi] * (nxt - pm)

    rkv = mix(rkv_ref, rkv_p, rkv_n, 0, 3 * c)
    lora = mix(lo_ref, lo_p, lo_n, 3 * c, 3 * c + 2 * LANES)
    gdo_ref[0] = mix(gd_ref, gd_p, gd_n, 3 * c + 2 * LANES, 3 * c + 4 * LANES)

    r, k, v = rkv[:, 0:c], rkv[:, c:2 * c], rkv[:, 2 * c:3 * c]
    wd, ad = lora[:, 0:LANES], lora[:, LANES:2 * LANES]
    dec = _mm(jnp.tanh(wd), du_ref[...], split_a=True, split_b=True) + w0_ref[...]
    log_decay = -float(np.exp(-0.5)) * jax.nn.sigmoid(dec)
    a = jax.nn.sigmoid(_mm(ad, au_ref[...], split_a=True, split_b=True) + a0_ref[...])
    kk = k * kkw_ref[...]
    kk = kk * lax.rsqrt(jnp.maximum(_mm(kk * kk, ones_ref[...]), KK_EPS))
    ka = ka_ref[...]
    for z, (w_o, kd_o, b_o) in enumerate(((wf_ref, kdf_ref, bf_ref), (wb_ref, kdb_ref, bb_ref))):
        az = a[:, z * c:(z + 1) * c]
        w_o[0] = log_decay[:, z * c:(z + 1) * c]
        kd_o[0] = k * (1.0 + (az - 1.0) * ka)
        b_o[0] = kk * az
    v_ref[0] = v
    kk_ref[0] = kk
    r_ref[0] = r


def _mixer_prep_kernel(nct, nt, len_ctx, len_lat, *refs):
    n_in = (18, 5, 6)
    n_out = (10, 1, 3)
    cuts = np.cumsum(n_in + n_out).tolist()
    prep_in, pool_in, qk_in, prep_out, pool_out, qk_out = [refs[a:b] for a, b in zip([0] + cuts[:-1], cuts)]
    _prep_kernel(nct, nt, *prep_in, *prep_out)
    _pool_kernel(nct, len_ctx, len_lat, *pool_in, *pool_out)
    _qk_norm_kernel(*qk_in, *qk_out)


def _mixer_prep(p, lw, nct, len_ctx, len_lat):
    bsz, n, _ = p.shape
    c = lw["k_k"].shape[1]
    nt = n // TM
    wide = 2 * LANES
    lora_col = (6 * c) // wide
    prep_consts = [lw["mu_prev"], lw["mu_next"], lw["decay_up"], lw["decay_w0"], lw["iclr_up"], lw["iclr_a0"],
                   lw["k_k"], lw["k_a"], lw["ones_bd"]]
    pool_consts = [lw["pool_w"], lw["pool_scale"]]
    qk_consts = [lw["q_gain"], lw["k_gain"], lw["ones_bd"]]
    consts = lambda arrs: [_const_spec(a.shape) for a in arrs]
    tok = lambda w: pl.BlockSpec((1, TM, w), lambda b, i: (b, i, 0))
    heads = c // HEAD
    head_major = pl.BlockSpec((1, heads, TM, HEAD), lambda b, i: (b, 0, i, 0))
    return pl.pallas_call(
        functools.partial(_mixer_prep_kernel, nct, nt, len_ctx, len_lat),
        grid=(bsz, nt),
        in_specs=_halo_specs(3 * c, 0, n) + _halo_specs(wide, lora_col, n) + _halo_specs(wide, lora_col + 1, n)
        + consts(prep_consts) + _halo_specs(wide, lora_col + 2, n) + consts(pool_consts)
        + [pl.BlockSpec((1, TM, c), lambda b, i, j=j: (b, i, 3 + j)) for j in range(3)] + consts(qk_consts),
        out_specs=[tok(c)] * 9 + [tok(wide), tok(wide)] + [head_major] * 3,
        out_shape=[jax.ShapeDtypeStruct((bsz, n, c), F32)] * 9 + [jax.ShapeDtypeStruct((bsz, n, wide), F32)] * 2
        + [jax.ShapeDtypeStruct((bsz, heads, n, HEAD), BF16)] * 3,
        compiler_params=_params("parallel", "parallel"),
        name="mixer_prep",
    )(*[p] * 9, *prep_consts, *[p] * 3, *pool_consts, *[p] * 3, *qk_consts)


def _split(a):
    hi = a.astype(BF16)
    return hi, (a - hi.astype(F32)).astype(BF16)


def _mm(a, b, split_a=False, split_b=False):
    def halves(x, split):
        if x.dtype == BF16 or not split:
            return x.astype(BF16), None
        return _split(x)

    (ah, al), (bh, bl) = halves(a, split_a), halves(b, split_b)
    lhs, rhs = [ah], [bh]
    if al is not None:
        lhs.append(al)
        rhs.append(bh)
    if bl is not None:
        lhs.append(ah)
        rhs.append(bl)
    if len(lhs) == 1:
        return jnp.dot(ah, bh, preferred_element_type=F32)
    return jnp.dot(jnp.concatenate(lhs, axis=1), jnp.concatenate(rhs, axis=0), preferred_element_type=F32)


def _chunk_scan_kernel(ldf, kdf, bf, vf, kkf, rf, ldb, kdb, bb, vb, kkb, rb, yf_ref, yb_ref, h_ref):
    @pl.when(pl.program_id(0) == 0)
    def _():
        h_ref[...] = jnp.zeros_like(h_ref)

    bsz, block_len, c = ldf.shape
    cs = CHUNK
    dirs = ((ldf, kdf, bf, vf, kkf, rf, yf_ref), (ldb, kdb, bb, vb, kkb, rb, yb_ref))
    t_i = lax.broadcasted_iota(jnp.int32, (cs, LANES), 0)
    lane = lax.broadcasted_iota(jnp.int32, (cs, LANES), 1)
    s_i = jnp.bitwise_and(lane, HEAD - 1)
    m_a = (lane < HEAD).astype(F32).astype(BF16)
    m_b = (lane >= HEAD).astype(F32).astype(BF16)
    r2 = lax.broadcasted_iota(jnp.int32, (LANES, LANES), 0)
    l2 = lax.broadcasted_iota(jnp.int32, (LANES, LANES), 1)
    block = ((r2 >= HEAD) == (l2 >= HEAD)).astype(F32)
    eye = (r2 == l2).astype(F32)
    zeros = jnp.zeros((cs, LANES), BF16)

    def bd(x):
        k = x.shape[1] // LANES
        return jnp.concatenate([x * jnp.concatenate([m_a] * k, axis=1), x * jnp.concatenate([m_b] * k, axis=1)],
                               axis=0)

    def unit(d, b, p, rows):
        ls = slice(p * LANES, (p + 1) * LANES)
        ld, kd, b_, v, kk, r = (ref[b, rows, ls] for ref in dirs[d][:6])
        before = (s_i < t_i) if d == 0 else (s_i > t_i)
        upto = (s_i <= t_i) if d == 0 else (s_i >= t_i)
        tri = upto.astype(F32).astype(BF16)
        lh, ll = _split(ld)
        big_l = jnp.dot(tri, jnp.concatenate([lh, ll], axis=0), preferred_element_type=F32)
        yield
        ltot = big_l[cs - 1:cs] if d == 0 else big_l[0:1]
        kap = kk * jnp.exp(big_l - ld)
        rt = r * jnp.exp(big_l)
        einv = jnp.exp(-big_l)
        kt, bt = kd * einv, b_ * einv
        efin = jnp.exp(ltot - big_l)
        khat, bhat = kd * efin, b_ * efin
        ktb, btb, vb16 = kt.astype(BF16), bt.astype(BF16), v.astype(BF16)
        sc = _dot_nt(jnp.concatenate([kap, rt], axis=0).astype(BF16),
                     jnp.concatenate([ktb * m_a, ktb * m_b, btb * m_a, btb * m_b], axis=0))
        yield
        a_k = jnp.where(before, sc[:cs, :LANES], 0.0)
        n_p = jnp.where(before, -sc[:cs, LANES:], 0.0)
        m_k = jnp.where(upto, sc[cs:, :LANES], 0.0)
        m_nb = jnp.where(upto, -sc[cs:, LANES:], 0.0)
        x = jnp.concatenate([kap, _mm(a_k, bd(vb16))], axis=1)
        yield
        for level in range(6):
            xh, xl = _split(x)
            nb = n_p.astype(BF16)
            x = x + jnp.dot(jnp.concatenate([nb, nb], axis=1), jnp.concatenate([bd(xh), bd(xl)], axis=0),
                            preferred_element_type=F32)
            if level < 5:
                n_p = jnp.dot(nb, bd(nb), preferred_element_type=F32)
            yield
        vz = jnp.concatenate([vb16, zeros], axis=1)
        uw = jnp.concatenate([x[:, LANES:], x[:, :LANES]], axis=1).astype(BF16)
        o1 = _mm(jnp.concatenate([m_k, m_nb], axis=1), jnp.concatenate([bd(vz), bd(uw)], axis=0))
        o2 = _mm(jnp.concatenate([khat, -bhat], axis=0).T, jnp.concatenate([vz, uw], axis=0))
        yield
        h = h_ref[b, d, p]
        y0, q = o1[:, :LANES], rt + o1[:, LANES:]
        psi = o2[:, :LANES] * block
        phi = o2[:, LANES:] * block + eye * jnp.exp(ltot)
        dirs[d][6][b, rows, ls] = y0 + _mm(q, h)
        h_ref[b, d, p] = _mm(phi, h, split_a=True, split_b=True) + psi

    n_sub = block_len // cs
    units = {}
    for s in range(n_sub):
        for d in range(2):
            ci = s if d == 0 else n_sub - 1 - s
            for b in range(bsz):
                for p in range(c // LANES):
                    units[s, d, b, p] = unit(d, b, p, slice(ci * cs, (ci + 1) * cs))
    _run_interleaved(units)


def _rwkv_scan(ld_f, ld_b, kd_f, kd_b, b_f, b_b, v, kk, r, len_ctx):
    bsz, n, c = v.shape
    assert len_ctx % SCAN_BLOCK == 0 and n % SCAN_BLOCK == 0 and SCAN_BLOCK % CHUNK == 0
    nc, nct_c = n // SCAN_BLOCK, len_ctx // SCAN_BLOCK

    def rev(g):
        return jnp.where(g < nct_c, nct_c - 1 - g, nc - 1 - g + nct_c)

    fwd = pl.BlockSpec((bsz, SCAN_BLOCK, c), lambda g: (0, g, 0))
    bwd = pl.BlockSpec((bsz, SCAN_BLOCK, c), lambda g: (0, rev(g), 0))
    tok = jax.ShapeDtypeStruct((bsz, n, c), F32)
    return pl.pallas_call(
        _chunk_scan_kernel,
        grid=(nc,),
        in_specs=[fwd] * 6 + [bwd] * 6,
        out_specs=[fwd, bwd],
        out_shape=[tok, tok],
        scratch_shapes=[pltpu.VMEM((bsz, 2, c // LANES, LANES, LANES), F32)],
        compiler_params=_params("arbitrary"),
        name="rwkv_scan",
    )(ld_f, kd_f, b_f, v, kk, r, ld_b, kd_b, b_b, v, kk, r)


def _readout(yf, yb, r, kdf, kdb, v, gd, gnw, gnb, rk, gup, ones_ref):
    ones = ones_ref[...]
    inv = 1.0 / HEAD
    y = yf[0] + yb[0]
    yc = y - _mm(y, ones, split_a=True) * inv
    var = _mm(yc * yc, ones) * inv
    yn = yc * lax.rsqrt(var + GN_EPS) * gnw[...] + gnb[...]
    bonus = _mm(r[0] * rk[...] * (kdf[0] + kdb[0]), ones) * v[0]
    return (yn + bonus) * _mm(jax.nn.sigmoid(gd[0]), gup[...])


def _pool_kernel(nct, len_ctx, len_lat, main, prev8, next8, pw_ref, scale_ref, o_ref):
    i = pl.program_id(1)
    is_lat = i >= nct
    seq_len = jnp.where(is_lat, len_lat, len_ctx)
    t0 = jnp.where(is_lat, i - nct, i) * TM
    n = TM + 2 * SUBLANES
    pm = main[0]
    ext = jnp.concatenate([prev8[0], pm, next8[0]], axis=0)
    pos = t0 - SUBLANES + lax.broadcasted_iota(jnp.int32, (n, 1), 0)
    e = jnp.where(jnp.logical_and(pos >= 0, pos < seq_len), ext, 0.0)
    a2 = e + pltpu.roll(e, 1, 0)
    a4 = pltpu.roll(a2, 1, 0) + pltpu.roll(a2, n - 1, 0)
    a8 = pltpu.roll(a4, 2, 0) + pltpu.roll(a4, n - 2, 0)
    a16 = pltpu.roll(a8, 4, 0) + pltpu.roll(a8, n - 4, 0)
    t = t0 + lax.broadcasted_iota(jnp.int32, (TM, 1), 0)
    lane = lax.broadcasted_iota(jnp.int32, pm.shape, 1)
    mean = None
    for g, (w, acc) in reversed(list(enumerate(zip(POOL_WINDOWS, (a2, a4, a8, a16))))):
        lo = jnp.maximum(t - w // 2, 0)
        hi = jnp.minimum(t + (w - w // 2) - 1, seq_len - 1)
        m = acc[SUBLANES:SUBLANES + TM] / (hi - lo + 1).astype(F32)
        mean = m if mean is None else jnp.where(lane < (g + 1) * POOL_GROUP, m, mean)
    o_ref[0] = _mm(mean - pm, pw_ref[...], split_a=True, split_b=True) * scale_ref[...]


def _qk_norm_kernel(q_ref, k_ref, v_ref, qg_ref, kg_ref, ones_ref, qo, ko, vo):
    ones = ones_ref[...]
    inv = 1.0 / HEAD
    q, k = q_ref[0], k_ref[0]
    qn = q * lax.rsqrt(_mm(q * q, ones) * inv + NORM_EPS) * qg_ref[...]
    kn = k * lax.rsqrt(_mm(k * k, ones) * inv + NORM_EPS) * kg_ref[...]
    for o_ref, val in ((qo, qn * HEAD ** -0.5), (ko, kn), (vo, v_ref[0])):
        val = val.astype(BF16)
        for h in range(o_ref.shape[1]):
            o_ref[0, h] = val[:, h * HEAD:(h + 1) * HEAD]


def _run_interleaved(units):
    done = {}
    while units:
        for k in list(units):
            try:
                next(units[k])
            except StopIteration as stop:
                done[k] = stop.value
                del units[k]
    return done


def _attend(q, key_sets, bias):
    scores = [_dot_nt(q, k) for k, _ in key_sets]
    yield
    if bias is not None:
        scores[0] = scores[0] + bias
    m = functools.reduce(jnp.maximum, [jnp.max(s, axis=-1, keepdims=True) for s in scores])
    yield
    ps = [jnp.exp(s - m) for s in scores]
    den = functools.reduce(jnp.add, [jnp.sum(p, axis=-1, keepdims=True) for p in ps])
    num = functools.reduce(jnp.add, [jnp.dot(p.astype(BF16), v, preferred_element_type=F32)
                                      for p, (_, v) in zip(ps, key_sets)])
    yield
    return num / den


def _nat_kernel(rows, q_ref, kp, kc, kn, vp, vc, vn, kx_ref, vx_ref, bias_ref, o_ref, ks, vs):
    i = pl.program_id(1) - 1
    heads, tq = q_ref.shape[1], q_ref.shape[2]
    rb = tq // GRID_W
    nloc = WIN_H * GRID_W

    @pl.when(i < 0)
    def _():
        units = {h: _attend(q_ref[0, h], [(kx_ref[0, h], vx_ref[0, h])], None) for h in range(heads)}
        for h, o in _run_interleaved(units).items():
            o_ref[0, :, h * HEAD:(h + 1) * HEAD] = o

    @pl.when(i >= 0)
    def _():
        for j, (kr, vr) in enumerate(((kp, vp), (kc, vc), (kn, vn))):
            ks[:, j * tq:(j + 1) * tq, :] = kr[0]
            vs[:, j * tq:(j + 1) * tq, :] = vr[0]
        for r0 in range(0, rb, NAT_ROWS):
            units = {}
            for rr in range(r0, r0 + NAT_ROWS):
                r = i * rb + rr
                rs = jnp.clip(r - WIN_H // 2, 0, rows - WIN_H)
                off = r - rs
                start = pl.multiple_of((rs - i * rb + rb) * GRID_W, GRID_W)
                qs = slice(rr * GRID_W, (rr + 1) * GRID_W)
                for h in range(heads):
                    units[rr, h] = _attend(q_ref[0, h, qs],
                                           [(ks[h, pl.ds(start, nloc)], vs[h, pl.ds(start, nloc)]),
                                            (kx_ref[0, h], vx_ref[0, h])], bias_ref[h, off])
            for (rr, h), o in _run_interleaved(units).items():
                o_ref[0, rr * GRID_W:(rr + 1) * GRID_W, h * HEAD:(h + 1) * HEAD] = o


def _nat_attention(qn, kn, vn, bias, len_ctx, len_lat):
    bsz, heads, n, hd = qn.shape
    tq = len_ctx
    assert tq == (WIN_H // 2) * GRID_W and len_lat % tq == 0
    rows = len_lat // GRID_W
    nblk = len_lat // tq

    def blk(shift):
        return pl.BlockSpec((1, heads, tq, hd),
                            lambda b, i: (b, 0, jnp.where(i == 0, 0, 1 + jnp.clip(i - 1 + shift, 0, nblk - 1)), 0))

    ctx = pl.BlockSpec((1, heads, tq, hd), lambda b, i: (b, 0, 0, 0))
    return pl.pallas_call(
        functools.partial(_nat_kernel, rows),
        grid=(bsz, 1 + nblk),
        in_specs=[blk(0), blk(-1), blk(0), blk(1), blk(-1), blk(0), blk(1), ctx, ctx, _const_spec(bias.shape)],
        out_specs=pl.BlockSpec((1, tq, heads * hd), lambda b, i: (b, i, 0)),
        out_shape=jax.ShapeDtypeStruct((bsz, n, heads * hd), F32),
        scratch_shapes=[pltpu.VMEM((heads, 3 * tq, hd), BF16), pltpu.VMEM((heads, 3 * tq, hd), BF16)],
        compiler_params=_params("parallel", "parallel"),
        name="nat_attention",
    )(qn, kn, kn, kn, vn, vn, vn, kn, vn, bias)


def _nat_bias_table(rpb):
    qc = np.arange(GRID_W)[:, None]
    kc = np.arange(GRID_W)[None, :]
    cs = np.clip(qc - WIN_W // 2, 0, GRID_W - WIN_W)
    valid = (kc >= cs) & (kc < cs + WIN_W)
    dc = kc - qc + WIN_W - 1
    pick = (np.arange(2 * WIN_W - 1)[:, None, None] == dc[None]) & valid[None]
    cols = jnp.einsum("hdm,mqk->hdqk", rpb, jnp.asarray(pick, F32), precision=lax.Precision.HIGHEST)
    cols = jnp.where(valid[None, None], cols, MASK_BIAS)
    t = jnp.stack([cols[:, WIN_H - 1 - off:2 * WIN_H - 1 - off] for off in range(WIN_H)], axis=1)
    return t.transpose(0, 1, 3, 2, 4).reshape(rpb.shape[0], WIN_H, GRID_W, WIN_H * GRID_W)


def _row_mod(len_ctx, tm, ctx_ref, lat_ref, axis=1):
    row = pl.program_id(axis) * tm + lax.broadcasted_iota(jnp.int32, (tm, 1), 0)
    return jnp.where(row < len_ctx, ctx_ref[0], lat_ref[0])


def _row_mod_specs(d, k):
    return [pl.BlockSpec((1, 1, d), lambda b, i, s=s: ((b * 2 + s) * 6 + k, 0, 0)) for s in range(2)]


def _merge_kernel(len_ctx, *refs):
    a = _readout(*refs[:12])
    bp, cn, ga, gb, gc, x, g1c, g1l, wa, wb, wc, wo, o_ref = refs[12:]
    sig = lambda g: jax.nn.sigmoid(g[0].astype(F32))
    m = sig(ga) * _dot_bf(a, wa[...]) + sig(gb) * _dot_bf(bp[0], wb[...]) + sig(gc) * _dot_bf(cn[0], wc[...])
    o_ref[0] = x[0] + _row_mod(len_ctx, x.shape[1], g1c, g1l) * _dot_bf(m, wo[...])


def _merge(rwkv, bp, cn, gates, x, mod, lw, len_ctx):
    bsz, n, d = x.shape
    assert n % TM_ROW == 0
    tok = lambda w, col=0: pl.BlockSpec((1, TM_ROW, w), lambda b, i: (b, i, col))
    consts = [lw["gn_w"], lw["gn_b"], lw["r_k"], lw["gate_up"], lw["ones_bd"]]
    ws = [lw["w_rwkv_o"], lw["w_pool_o"], lw["w_nat_o"], lw["w_out"]]
    return pl.pallas_call(
        functools.partial(_merge_kernel, len_ctx),
        grid=(bsz, n // TM_ROW),
        in_specs=[tok(a.shape[2]) for a in rwkv] + [_const_spec(a.shape) for a in consts]
        + [tok(bp.shape[2]), tok(cn.shape[2]), tok(d, 0), tok(d, 1), tok(d, 2), tok(d)]
        + _row_mod_specs(d, 2) + [_const_spec(w.shape) for w in ws],
        out_specs=tok(d),
        out_shape=jax.ShapeDtypeStruct((bsz, n, d), F32),
        compiler_params=_params("parallel", "parallel"),
        name="merge",
    )(*rwkv, *consts, bp, cn, gates, gates, gates, x, mod, mod, *ws)


def _ffn_kernel(len_ctx, x_ref, sh_c, sh_l, sc_c, sc_l, g2_c, g2_l, gain, w1, w2, o_ref):
    x = x_ref[0]
    tm = x.shape[0]
    h = _norm_mod(x, gain[...], _row_mod(len_ctx, tm, sh_c, sh_l), _row_mod(len_ctx, tm, sc_c, sc_l))
    u = jnp.dot(h.astype(BF16), w1[...], preferred_element_type=F32)
    hid = w2.shape[0]
    gate, up = u[:, :hid], u[:, hid:]
    act = gate * jax.nn.sigmoid(gate) * up
    o_ref[0] = x + _row_mod(len_ctx, tm, g2_c, g2_l) * jnp.dot(act.astype(BF16), w2[...],
                                                              preferred_element_type=F32)


def _ffn(x, mod, lw, len_ctx, latent_only):
    bsz, n, d = x.shape
    tm = TM if latent_only else TM_ROW
    skip = len_ctx // tm if latent_only else 0
    assert n % tm == 0 and (len_ctx % tm == 0 or not latent_only)
    n_out = n - skip * tm
    return pl.pallas_call(
        functools.partial(_ffn_kernel, 0 if latent_only else len_ctx),
        grid=(bsz, n_out // tm),
        in_specs=[pl.BlockSpec((1, tm, d), lambda b, i: (b, i + skip, 0))]
        + _row_mod_specs(d, 3) + _row_mod_specs(d, 4) + _row_mod_specs(d, 5)
        + [_const_spec(lw["norm2"].shape), _const_spec(lw["w_ffn_in"].shape), _const_spec(lw["w_ffn_out"].shape)],
        out_specs=pl.BlockSpec((1, tm, d), lambda b, i: (b, i, 0)),
        out_shape=jax.ShapeDtypeStruct((bsz, n_out, d), F32),
        compiler_params=_params("parallel", "parallel"),
        name="ffn",
    )(x, *[mod] * 6, lw["norm2"], lw["w_ffn_in"], lw["w_ffn_out"])


def _block_diag(blocks):
    n = len(blocks)
    rows = []
    for i, blk in enumerate(blocks):
        rows.append(jnp.concatenate([blk if j == i else jnp.zeros((blk.shape[0], blocks[j].shape[1]), blk.dtype)
                                     for j in range(n)], axis=1))
    return jnp.concatenate(rows, axis=0)


def _pad_cols(a, width):
    return jnp.pad(a, ((0, 0), (0, width - a.shape[1])))


def _layer_weights(l, prm):
    c = prm["k_k"].shape[1]
    lora = prm["decay_up"].shape[2]
    gl = prm["gate_up"].shape[1]
    pool = prm["pool_scale"].shape[1]
    d = prm["w_out"].shape[1]
    w_in = prm["w_in"][l]
    o_lora, o_gd, o_pool = 3 * c, 3 * c + 4 * lora, 3 * c + 4 * lora + gl
    o_q = o_pool + pool
    o_gate = o_q + 3 * c
    assert 4 * lora == 2 * LANES and gl <= 2 * LANES and pool == 2 * LANES and o_gate + 3 * d == w_in.shape[1]
    w_mix = jnp.concatenate([
        w_in[:, 0:o_lora], w_in[:, o_q:o_gate], w_in[:, o_lora:o_gd],
        _pad_cols(w_in[:, o_gd:o_pool], 2 * LANES), w_in[:, o_pool:o_q]], axis=1).astype(BF16)
    mu = lambda m: _pad_cols(m[l][None, :o_pool], o_pool + 2 * LANES - gl)
    heads = c // HEAD
    row = lambda a: a.reshape(1, -1)
    return {
        "w_mix": w_mix, "w_gate": w_in[:, o_gate:].astype(BF16),
        "norm1": row(prm["norm1"][l]), "norm2": row(prm["norm2"][l]),
        "mu_prev": mu(prm["mu_prev"]), "mu_next": mu(prm["mu_next"]),
        "decay_up": _block_diag([prm["decay_up"][l, 0], prm["decay_up"][l, 1]]),
        "decay_w0": row(prm["decay_w0"][l]),
        "iclr_up": _block_diag([prm["iclr_up"][l, 0], prm["iclr_up"][l, 1]]),
        "iclr_a0": row(prm["iclr_a0"][l]),
        "k_k": row(prm["k_k"][l]), "k_a": row(prm["k_a"][l]), "r_k": row(prm["r_k"][l]),
        "gn_w": row(prm["gn_w"][l]), "gn_b": row(prm["gn_b"][l]),
        "gate_up": jnp.pad(prm["gate_up"][l], ((0, 2 * LANES - gl), (0, 0))),
        "ones_bd": jnp.kron(jnp.eye(heads, dtype=F32), jnp.ones((HEAD, HEAD), F32)).astype(BF16),
        "pool_w": _block_diag([prm["pool_w"][l, g] for g in range(len(POOL_WINDOWS))]),
        "pool_scale": row(prm["pool_scale"][l]),
        "q_gain": row(jnp.tile(prm["q_gain"][l], heads)), "k_gain": row(jnp.tile(prm["k_gain"][l], heads)),
        "nat_bias": _nat_bias_table(prm["rpb"][l]),
        "w_rwkv_o": prm["w_rwkv_o"][l].astype(BF16), "w_pool_o": prm["w_pool_o"][l].astype(BF16),
        "w_nat_o": prm["w_nat_o"][l].astype(BF16), "w_out": prm["w_out"][l].astype(BF16),
        "w_ffn_in": prm["w_ffn_in"][l].astype(BF16), "w_ffn_out": prm["w_ffn_out"][l].astype(BF16),
    }


def kernel(x, c, ctx, c_ctx, w_mod, b_mod, norm1, norm2, w_in, mu_prev, mu_next, decay_w0, decay_up, iclr_a0, iclr_up, gate_up, k_k, k_a, r_k, gn_w, gn_b, pool_w, pool_scale, q_gain, k_gain, rpb, w_rwkv_o, w_pool_o, w_nat_o, w_out, w_ffn_in, w_ffn_out):
    prm = dict(norm1=norm1, norm2=norm2, w_in=w_in, mu_prev=mu_prev, mu_next=mu_next, decay_w0=decay_w0,
               decay_up=decay_up, iclr_a0=iclr_a0, iclr_up=iclr_up, gate_up=gate_up, k_k=k_k, k_a=k_a, r_k=r_k,
               gn_w=gn_w, gn_b=gn_b, pool_w=pool_w, pool_scale=pool_scale, q_gain=q_gain, k_gain=k_gain, rpb=rpb,
               w_rwkv_o=w_rwkv_o, w_pool_o=w_pool_o, w_nat_o=w_nat_o, w_out=w_out, w_ffn_in=w_ffn_in,
               w_ffn_out=w_ffn_out)
    bsz, len_lat, d = x.shape
    len_ctx = ctx.shape[1]
    depth = w_mod.shape[0]
    assert len_ctx % TM == 0 and len_lat % TM == 0 and bsz + 1 <= SUBLANES
    nct = len_ctx // TM

    s_rows = jnp.concatenate([c, c_ctx[None, :], jnp.zeros((SUBLANES - bsz - 1, d), F32)], axis=0)
    mod_all = _modulation(s_rows, w_mod, b_mod)
    xa = jnp.concatenate([ctx, x], axis=1)

    for l in range(depth):
        lw = _layer_weights(l, prm)
        m_lat = mod_all[l, :bsz]
        m_ctx = jnp.broadcast_to(mod_all[l, bsz][None], m_lat.shape)
        mod = jnp.stack([m_ctx, m_lat], axis=1).reshape(bsz * 2 * 6, 1, d)

        p = _in_proj(xa, mod, lw["norm1"], lw["w_mix"], len_ctx, F32)
        gates = _in_proj(xa, mod, lw["norm1"], lw["w_gate"], len_ctx, BF16)
        ld_f, ld_b, kd_f, kd_b, b_f, b_b, v, kk, r, gd, b_br, qn, kn, vn = _mixer_prep(p, lw, nct, len_ctx, len_lat)
        y_f, y_b = _rwkv_scan(ld_f, ld_b, kd_f, kd_b, b_f, b_b, v, kk, r, len_ctx)
        c_br = _nat_attention(qn, kn, vn, lw["nat_bias"], len_ctx, len_lat)
        xa = _merge((y_f, y_b, r, kd_f, kd_b, v, gd), b_br, c_br, gates, xa, mod, lw, len_ctx)
        xa = _ffn(xa, mod, lw, len_ctx, latent_only=(l == depth - 1))
    return xa
```

```python
import functools

import numpy as np
import jax
import jax.numpy as jnp
from jax import lax
from jax.experimental import pallas as pl
from jax.experimental.pallas import tpu as pltpu

F32 = jnp.float32
BF16 = jnp.bfloat16

HEAD = 64
NORM_EPS = 1e-6
GN_EPS = 64e-5
KK_EPS = 1e-24
POOL_WINDOWS = (2, 4, 8, 16)
POOL_GROUP = 64
GRID_W = 64
WIN_H = 8
WIN_W = 16
MASK_BIAS = -1e30

LANES = 128
SUBLANES = 8
VMEM_LIMIT = 56 * 1024 * 1024

TM = 256
NAT_ROWS = 4
CHUNK = 64
SCAN_BLOCK = 256
TM_IN = 1280
TM_ROW = 640
TN_IN = 3072
TN_MOD = 1536


def _dot_hi(a, b):
    return jnp.dot(a, b, precision=lax.Precision.HIGHEST, preferred_element_type=F32)


def _dot_bf(a, b):
    return jnp.dot(a.astype(BF16), b.astype(BF16), preferred_element_type=F32)


def _dot_nt(a, b):
    return lax.dot_general(a, b, (((1,), (1,)), ((), ())), preferred_element_type=F32)


def _params(*sem):
    return pltpu.CompilerParams(dimension_semantics=sem, vmem_limit_bytes=VMEM_LIMIT)


def _const_spec(shape):
    nd = len(shape)
    return pl.BlockSpec(shape, lambda *_: (0,) * nd, pipeline_mode=pl.Buffered(1))


def _mod_kernel(s_ref, w_ref, b_ref, o_ref):
    s = s_ref[...]
    s = s * jax.nn.sigmoid(s)
    o_ref[0] = _dot_hi(s, w_ref[0]) + b_ref[0]


def _modulation(s_rows, w_mod, b_mod):
    depth, d, n = w_mod.shape
    return pl.pallas_call(
        _mod_kernel,
        grid=(depth, n // TN_MOD),
        in_specs=[
            pl.BlockSpec((SUBLANES, d), lambda l, j: (0, 0)),
            pl.BlockSpec((1, d, TN_MOD), lambda l, j: (l, 0, j)),
            pl.BlockSpec((1, 1, TN_MOD), lambda l, j: (l, 0, j)),
        ],
        out_specs=pl.BlockSpec((1, SUBLANES, TN_MOD), lambda l, j: (l, 0, j)),
        out_shape=jax.ShapeDtypeStruct((depth, SUBLANES, n), F32),
        compiler_params=_params("parallel", "parallel"),
        name="modulation",
    )(s_rows, w_mod, b_mod.reshape(depth, 1, n))


def _norm_mod(x, gain, shift, scale):
    ms = jnp.mean(x * x, axis=-1, keepdims=True)
    return x * lax.rsqrt(ms + NORM_EPS) * gain * (1.0 + scale) + shift


def _in_proj_kernel(len_ctx, x_ref, sh_c, sc_c, sh_l, sc_l, g_ref, w_ref, o_ref):
    tm = x_ref.shape[1]
    h = _norm_mod(x_ref[0], g_ref[...], _row_mod(len_ctx, tm, sh_c, sh_l, axis=2), _row_mod(len_ctx, tm, sc_c, sc_l, axis=2))
    o_ref[0] = jnp.dot(h.astype(BF16), w_ref[...], preferred_element_type=F32).astype(o_ref.dtype)


def _in_proj(x, mod, gain, w, len_ctx, out_dtype):
    bsz, n, d = x.shape
    nout = w.shape[1]
    assert n % TM_IN == 0 and nout % TN_IN == 0
    mspec = lambda is_lat, k: pl.BlockSpec((1, 1, d), lambda j, b, i: ((b * 2 + is_lat) * 6 + k, 0, 0))
    return pl.pallas_call(
        functools.partial(_in_proj_kernel, len_ctx),
        grid=(nout // TN_IN, bsz, n // TM_IN),
        in_specs=[
            pl.BlockSpec((1, TM_IN, d), lambda j, b, i: (b, i, 0)),
            mspec(0, 0), mspec(0, 1), mspec(1, 0), mspec(1, 1),
            pl.BlockSpec((1, d), lambda j, b, i: (0, 0)),
            pl.BlockSpec((d, TN_IN), lambda j, b, i: (0, j), pipeline_mode=pl.Buffered(1 if nout == TN_IN else 2)),
        ],
        out_specs=pl.BlockSpec((1, TM_IN, TN_IN), lambda j, b, i: (b, i, j)),
        out_shape=jax.ShapeDtypeStruct((bsz, n, nout), out_dtype),
        compiler_params=_params("parallel", "parallel", "parallel"),
        name="in_proj",
    )(x, mod, mod, mod, mod, gain, w)


def _halo_specs(width, col, n):
    nb = n // SUBLANES
    per = TM // SUBLANES
    return [
        pl.BlockSpec((1, TM, width), lambda b, i: (b, i, col)),
        pl.BlockSpec((1, SUBLANES, width), lambda b, i: (b, jnp.maximum(i * per - 1, 0), col)),
        pl.BlockSpec((1, SUBLANES, width), lambda b, i: (b, jnp.minimum((i + 1) * per, nb - 1), col)),
    ]


def _prep_kernel(nct, nt, rkv_ref, rkv_p, rkv_n, lo_ref, lo_p, lo_n, gd_ref, gd_p, gd_n,
                 mup_ref, mun_ref, du_ref, w0_ref, au_ref, a0_ref, kkw_ref, ka_ref, ones_ref,
                 wf_ref, wb_ref, kdf_ref, kdb_ref, bf_ref, bb_ref, v_ref, kk_ref, r_ref, gdo_ref):
    i = pl.program_id(1)
    first = jnp.logical_or(i == 0, i == nct)
    last = jnp.logical_or(i == nct - 1, i == nt - 1)
    row = lax.broadcasted_iota(jnp.int32, (TM, 1), 0)
    c = rkv_ref.shape[2] // 3

    def mix(main, prev8, next8, lo, hi):
        pm = main[0]
        prow = jnp.where(first, 0.0, prev8[0, SUBLANES - 1:SUBLANES, :])
        nrow = jnp.where(last, 0.0, next8[0, 0:1, :])
        prev = jnp.where(row == 0, prow, pltpu.roll(pm, 1, 0))
        nxt = jnp.where(row == TM - 1, nrow, pltpu.roll(pm, TM - 1, 0))
        return pm + mup_ref[:, lo:hi] * (prev - pm) + mun_ref[:, lo:hi] * (nxt - pm)

    rkv = mix(rkv_ref, rkv_p, rkv_n, 0, 3 * c)
    lora = mix(lo_ref, lo_p, lo_n, 3 * c, 3 * c + 2 * LANES)
    gdo_ref[0] = mix(gd_ref, gd_p, gd_n, 3 * c + 2 * LANES, 3 * c + 4 * LANES)

    r, k, v = rkv[:, 0:c], rkv[:, c:2 * c], rkv[:, 2 * c:3 * c]
    wd, ad = lora[:, 0:LANES], lora[:, LANES:2 * LANES]
    dec = _mm(jnp.tanh(wd), du_ref[...], split_a=True, split_b=True) + w0_ref[...]
    log_decay = -float(np.exp(-0.5)) * jax.nn.sigmoid(dec)
    a = jax.nn.sigmoid(_mm(ad, au_ref[...], split_a=True, split_b=True) + a0_ref[...])
    kk = k * kkw_ref[...]
    kk = kk * lax.rsqrt(jnp.maximum(_mm(kk * kk, ones_ref[...]), KK_EPS))
    ka = ka_ref[...]
    for z, (w_o, kd_o, b_o) in enumerate(((wf_ref, kdf_ref, bf_ref), (wb_ref, kdb_ref, bb_ref))):
        az = a[:, z * c:(z + 1) * c]
        w_o[0] = log_decay[:, z * c:(z + 1) * c]
        kd_o[0] = k * (1.0 + (az - 1.0) * ka)
        b_o[0] = kk * az
    v_ref[0] = v
    kk_ref[0] = kk
    r_ref[0] = r


def _mixer_prep_kernel(nct, nt, len_ctx, len_lat, *refs):
    n_in = (18, 5, 6)
    n_out = (10, 1, 3)
    cuts = np.cumsum(n_in + n_out).tolist()
    prep_in, pool_in, qk_in, prep_out, pool_out, qk_out = [refs[a:b] for a, b in zip([0] + cuts[:-1], cuts)]
    _prep_kernel(nct, nt, *prep_in, *prep_out)
    _pool_kernel(nct, len_ctx, len_lat, *pool_in, *pool_out)
    _qk_norm_kernel(*qk_in, *qk_out)


def _mixer_prep(p, lw, nct, len_ctx, len_lat):
    bsz, n, _ = p.shape
    c = lw["k_k"].shape[1]
    nt = n // TM
    wide = 2 * LANES
    lora_col = (6 * c) // wide
    prep_consts = [lw["mu_prev"], lw["mu_next"], lw["decay_up"], lw["decay_w0"], lw["iclr_up"], lw["iclr_a0"],
                   lw["k_k"], lw["k_a"], lw["ones_bd"]]
    pool_consts = [lw["pool_w"], lw["pool_scale"]]
    qk_consts = [lw["q_gain"], lw["k_gain"], lw["ones_bd"]]
    consts = lambda arrs: [_const_spec(a.shape) for a in arrs]
    tok = lambda w: pl.BlockSpec((1, TM, w), lambda b, i: (b, i, 0))
    heads = c // HEAD
    head_major = pl.BlockSpec((1, heads, TM, HEAD), lambda b, i: (b, 0, i, 0))
    return pl.pallas_call(
        functools.partial(_mixer_prep_kernel, nct, nt, len_ctx, len_lat),
        grid=(bsz, nt),
        in_specs=_halo_specs(3 * c, 0, n) + _halo_specs(wide, lora_col, n) + _halo_specs(wide, lora_col + 1, n)
        + consts(prep_consts) + _halo_specs(wide, lora_col + 2, n) + consts(pool_consts)
        + [pl.BlockSpec((1, TM, c), lambda b, i, j=j: (b, i, 3 + j)) for j in range(3)] + consts(qk_consts),
        out_specs=[tok(c)] * 9 + [tok(wide), tok(wide)] + [head_major] * 3,
        out_shape=[jax.ShapeDtypeStruct((bsz, n, c), F32)] * 9 + [jax.ShapeDtypeStruct((bsz, n, wide), F32)] * 2
        + [jax.ShapeDtypeStruct((bsz, heads, n, HEAD), BF16)] * 3,
        compiler_params=_params("parallel", "parallel"),
        name="mixer_prep",
    )(*[p] * 9, *prep_consts, *[p] * 3, *pool_consts, *[p] * 3, *qk_consts)


def _split(a):
    hi = a.astype(BF16)
    return hi, (a - hi.astype(F32)).astype(BF16)


def _mm(a, b, split_a=False, split_b=False):
    def halves(x, split):
        if x.dtype == BF16 or not split:
            return x.astype(BF16), None
        return _split(x)

    (ah, al), (bh, bl) = halves(a, split_a), halves(b, split_b)
    lhs, rhs = [ah], [bh]
    if al is not None:
        lhs.append(al)
        rhs.append(bh)
    if bl is not None:
        lhs.append(ah)
        rhs.append(bl)
    if len(lhs) == 1:
        return jnp.dot(ah, bh, preferred_element_type=F32)
    return jnp.dot(jnp.concatenate(lhs, axis=1), jnp.concatenate(rhs, axis=0), preferred_element_type=F32)


def _chunk_scan_kernel(ldf, kdf, bf, vf, kkf, rf, ldb, kdb, bb, vb, kkb, rb, yf_ref, yb_ref, h_ref):
    @pl.when(pl.program_id(0) == 0)
    def _():
        h_ref[...] = jnp.zeros_like(h_ref)

    bsz, block_len, c = ldf.shape
    cs = CHUNK
    dirs = ((ldf, kdf, bf, vf, kkf, rf, yf_ref), (ldb, kdb, bb, vb, kkb, rb, yb_ref))
    t_i = lax.broadcasted_iota(jnp.int32, (cs, LANES), 0)
    lane = lax.broadcasted_iota(jnp.int32, (cs, LANES), 1)
    s_i = jnp.bitwise_and(lane, HEAD - 1)
    m_a = (lane < HEAD).astype(F32).astype(BF16)
    m_b = (lane >= HEAD).astype(F32).astype(BF16)
    r2 = lax.broadcasted_iota(jnp.int32, (LANES, LANES), 0)
    l2 = lax.broadcasted_iota(jnp.int32, (LANES, LANES), 1)
    block = ((r2 >= HEAD) == (l2 >= HEAD)).astype(F32)
    eye = (r2 == l2).astype(F32)
    zeros = jnp.zeros((cs, LANES), BF16)

    def bd(x):
        k = x.shape[1] // LANES
        return jnp.concatenate([x * jnp.concatenate([m_a] * k, axis=1), x * jnp.concatenate([m_b] * k, axis=1)],
                               axis=0)

    def unit(d, b, p, rows):
        ls = slice(p * LANES, (p + 1) * LANES)
        ld, kd, b_, v, kk, r = (ref[b, rows, ls] for ref in dirs[d][:6])
        before = (s_i < t_i) if d == 0 else (s_i > t_i)
        upto = (s_i <= t_i) if d == 0 else (s_i >= t_i)
        tri = upto.astype(F32).astype(BF16)
        lh, ll = _split(ld)
        big_l = jnp.dot(tri, jnp.concatenate([lh, ll], axis=0), preferred_element_type=F32)
        yield
        ltot = big_l[cs - 1:cs] if d == 0 else big_l[0:1]
        kap = kk * jnp.exp(big_l - ld)
        rt = r * jnp.exp(big_l)
        einv = jnp.exp(-big_l)
        kt, bt = kd * einv, b_ * einv
        efin = jnp.exp(ltot - big_l)
        khat, bhat = kd * efin, b_ * efin
        ktb, btb, vb16 = kt.astype(BF16), bt.astype(BF16), v.astype(BF16)
        sc = _dot_nt(jnp.concatenate([kap, rt], axis=0).astype(BF16),
                     jnp.concatenate([ktb * m_a, ktb * m_b, btb * m_a, btb * m_b], axis=0))
        yield
        a_k = jnp.where(before, sc[:cs, :LANES], 0.0)
        n_p = jnp.where(before, -sc[:cs, LANES:], 0.0)
        m_k = jnp.where(upto, sc[cs:, :LANES], 0.0)
        m_nb = jnp.where(upto, -sc[cs:, LANES:], 0.0)
        x = jnp.concatenate([kap, _mm(a_k, bd(vb16))], axis=1)
        yield
        for level in range(6):
            xh, xl = _split(x)
            nb = n_p.astype(BF16)
            x = x + jnp.dot(jnp.concatenate([nb, nb], axis=1), jnp.concatenate([bd(xh), bd(xl)], axis=0),
                            preferred_element_type=F32)
            if level < 5:
                n_p = jnp.dot(nb, bd(nb), preferred_element_type=F32)
            yield
        vz = jnp.concatenate([vb16, zeros], axis=1)
        uw = jnp.concatenate([x[:, LANES:], x[:, :LANES]], axis=1).astype(BF16)
        o1 = _mm(jnp.concatenate([m_k, m_nb], axis=1), jnp.concatenate([bd(vz), bd(uw)], axis=0))
        o2 = _mm(jnp.concatenate([khat, -bhat], axis=0).T, jnp.concatenate([vz, uw], axis=0))
        yield
        h = h_ref[b, d, p]
        y0, q = o1[:, :LANES], rt + o1[:, LANES:]
        psi = o2[:, :LANES] * block
        phi = o2[:, LANES:] * block + eye * jnp.exp(ltot)
        dirs[d][6][b, rows, ls] = y0 + _mm(q, h)
        h_ref[b, d, p] = _mm(phi, h, split_a=True, split_b=True) + psi

    n_sub = block_len // cs
    units = {}
    for s in range(n_sub):
        for d in range(2):
            ci = s if d == 0 else n_sub - 1 - s
            for b in range(bsz):
                for p in range(c // LANES):
                    units[s, d, b, p] = unit(d, b, p, slice(ci * cs, (ci + 1) * cs))
    _run_interleaved(units)


def _rwkv_scan(ld_f, ld_b, kd_f, kd_b, b_f, b_b, v, kk, r, len_ctx):
    bsz, n, c = v.shape
    assert len_ctx % SCAN_BLOCK == 0 and n % SCAN_BLOCK == 0 and SCAN_BLOCK % CHUNK == 0
    nc, nct_c = n // SCAN_BLOCK, len_ctx // SCAN_BLOCK

    def rev(g):
        return jnp.where(g < nct_c, nct_c - 1 - g, nc - 1 - g + nct_c)

    fwd = pl.BlockSpec((bsz, SCAN_BLOCK, c), lambda g: (0, g, 0))
    bwd = pl.BlockSpec((bsz, SCAN_BLOCK, c), lambda g: (0, rev(g), 0))
    tok = jax.ShapeDtypeStruct((bsz, n, c), F32)
    return pl.pallas_call(
        _chunk_scan_kernel,
        grid=(nc,),
        in_specs=[fwd] * 6 + [bwd] * 6,
        out_specs=[fwd, bwd],
        out_shape=[tok, tok],
        scratch_shapes=[pltpu.VMEM((bsz, 2, c // LANES, LANES, LANES), F32)],
        compiler_params=_params("arbitrary"),
        name="rwkv_scan",
    )(ld_f, kd_f, b_f, v, kk, r, ld_b, kd_b, b_b, v, kk, r)


def _readout(yf, yb, r, kdf, kdb, v, gd, gnw, gnb, rk, gup, ones_ref):
    ones = ones_ref[...]
    inv = 1.0 / HEAD
    y = yf[0] + yb[0]
    yc = y - _mm(y, ones, split_a=True) * inv
    var = _mm(yc * yc, ones) * inv
    yn = yc * lax.rsqrt(var + GN_EPS) * gnw[...] + gnb[...]
    bonus = _mm(r[0] * rk[...] * (kdf[0] + kdb[0]), ones) * v[0]
    return (yn + bonus) * _mm(jax.nn.sigmoid(gd[0]), gup[...])


def _pool_kernel(nct, len_ctx, len_lat, main, prev8, next8, pw_ref, scale_ref, o_ref):
    i = pl.program_id(1)
    is_lat = i >= nct
    seq_len = jnp.where(is_lat, len_lat, len_ctx)
    t0 = jnp.where(is_lat, i - nct, i) * TM
    n = TM + 2 * SUBLANES
    pm = main[0]
    ext = jnp.concatenate([prev8[0], pm, next8[0]], axis=0)
    pos = t0 - SUBLANES + lax.broadcasted_iota(jnp.int32, (n, 1), 0)
    e = jnp.where(jnp.logical_and(pos >= 0, pos < seq_len), ext, 0.0)
    a2 = e + pltpu.roll(e, 1, 0)
    a4 = pltpu.roll(a2, 1, 0) + pltpu.roll(a2, n - 1, 0)
    a8 = pltpu.roll(a4, 2, 0) + pltpu.roll(a4, n - 2, 0)
    a16 = pltpu.roll(a8, 4, 0) + pltpu.roll(a8, n - 4, 0)
    t = t0 + lax.broadcasted_iota(jnp.int32, (TM, 1), 0)
    lane = lax.broadcasted_iota(jnp.int32, pm.shape, 1)
    mean = None
    for g, (w, acc) in reversed(list(enumerate(zip(POOL_WINDOWS, (a2, a4, a8, a16))))):
        lo = jnp.maximum(t - w // 2, 0)
        hi = jnp.minimum(t + (w - w // 2) - 1, seq_len - 1)
        m = acc[SUBLANES:SUBLANES + TM] / (hi - lo + 1).astype(F32)
        mean = m if mean is None else jnp.where(lane < (g + 1) * POOL_GROUP, m, mean)
    o_ref[0] = _mm(mean - pm, pw_ref[...], split_a=True, split_b=True) * scale_ref[...]


def _qk_norm_kernel(q_ref, k_ref, v_ref, qg_ref, kg_ref, ones_ref, qo, ko, vo):
    ones = ones_ref[...]
    inv = 1.0 / HEAD
    q, k = q_ref[0], k_ref[0]
    qn = q * lax.rsqrt(_mm(q * q, ones) * inv + NORM_EPS) * qg_ref[...]
    kn = k * lax.rsqrt(_mm(k * k, ones) * inv + NORM_EPS) * kg_ref[...]
    for o_ref, val in ((qo, qn * HEAD ** -0.5), (ko, kn), (vo, v_ref[0])):
        val = val.astype(BF16)
        for h in range(o_ref.shape[1]):
            o_ref[0, h] = val[:, h * HEAD:(h + 1) * HEAD]


def _run_interleaved(units):
    done = {}
    while units:
        for k in list(units):
            try:
                next(units[k])
            except StopIteration as stop:
                done[k] = stop.value
                del units[k]
    return done


def _attend(q, key_sets, bias):
    scores = [_dot_nt(q, k) for k, _ in key_sets]
    yield
    if bias is not None:
        scores[0] = scores[0] + bias
    m = functools.reduce(jnp.maximum, [jnp.max(s, axis=-1, keepdims=True) for s in scores])
    yield
    ps = [jnp.exp(s - m) for s in scores]
    den = functools.reduce(jnp.add, [jnp.sum(p, axis=-1, keepdims=True) for p in ps])
    num = functools.reduce(jnp.add, [jnp.dot(p.astype(BF16), v, preferred_element_type=F32)
                                      for p, (_, v) in zip(ps, key_sets)])
    yield
    return num / den


def _nat_kernel(rows, q_ref, kp, kc, kn, vp, vc, vn, kx_ref, vx_ref, bias_ref, o_ref, ks, vs):
    i = pl.program_id(1) - 1
    heads, tq = q_ref.shape[1], q_ref.shape[2]
    rb = tq // GRID_W
    nloc = WIN_H * GRID_W

    @pl.when(i < 0)
    def _():
        units = {h: _attend(q_ref[0, h], [(kx_ref[0, h], vx_ref[0, h])], None) for h in range(heads)}
        for h, o in _run_interleaved(units).items():
            o_ref[0, :, h * HEAD:(h + 1) * HEAD] = o

    @pl.when(i >= 0)
    def _():
        for j, (kr, vr) in enumerate(((kp, vp), (kc, vc), (kn, vn))):
            ks[:, j * tq:(j + 1) * tq, :] = kr[0]
            vs[:, j * tq:(j + 1) * tq, :] = vr[0]
        for r0 in range(0, rb, NAT_ROWS):
            units = {}
            for rr in range(r0, r0 + NAT_ROWS):
                r = i * rb + rr
                rs = jnp.clip(r - WIN_H // 2, 0, rows - WIN_H)
                off = r - rs
                start = pl.multiple_of((rs - i * rb + rb) * GRID_W, GRID_W)
                qs = slice(rr * GRID_W, (rr + 1) * GRID_W)
                for h in range(heads):
                    units[rr, h] = _attend(q_ref[0, h, qs],
                                           [(ks[h, pl.ds(start, nloc)], vs[h, pl.ds(start, nloc)]),
                                            (kx_ref[0, h], vx_ref[0, h])], bias_ref[h, off])
            for (rr, h), o in _run_interleaved(units).items():
                o_ref[0, rr * GRID_W:(rr + 1) * GRID_W, h * HEAD:(h + 1) * HEAD] = o


def _nat_attention(qn, kn, vn, bias, len_ctx, len_lat):
    bsz, heads, n, hd = qn.shape
    tq = len_ctx
    assert tq == (WIN_H // 2) * GRID_W and len_lat % tq == 0
    rows = len_lat // GRID_W
    nblk = len_lat // tq

    def blk(shift):
        return pl.BlockSpec((1, heads, tq, hd),
                            lambda b, i: (b, 0, jnp.where(i == 0, 0, 1 + jnp.clip(i - 1 + shift, 0, nblk - 1)), 0))

    ctx = pl.BlockSpec((1, heads, tq, hd), lambda b, i: (b, 0, 0, 0))
    return pl.pallas_call(
        functools.partial(_nat_kernel, rows),
        grid=(bsz, 1 + nblk),
        in_specs=[blk(0), blk(-1), blk(0), blk(1), blk(-1), blk(0), blk(1), ctx, ctx, _const_spec(bias.shape)],
        out_specs=pl.BlockSpec((1, tq, heads * hd), lambda b, i: (b, i, 0)),
        out_shape=jax.ShapeDtypeStruct((bsz, n, heads * hd), F32),
        scratch_shapes=[pltpu.VMEM((heads, 3 * tq, hd), BF16), pltpu.VMEM((heads, 3 * tq, hd), BF16)],
        compiler_params=_params("parallel", "parallel"),
        name="nat_attention",
    )(qn, kn, kn, kn, vn, vn, vn, kn, vn, bias)


def _nat_bias_table(rpb):
    qc = np.arange(GRID_W)[:, None]
    kc = np.arange(GRID_W)[None, :]
    cs = np.clip(qc - WIN_W // 2, 0, GRID_W - WIN_W)
    valid = (kc >= cs) & (kc < cs + WIN_W)
    dc = kc - qc + WIN_W - 1
    pick = (np.arange(2 * WIN_W - 1)[:, None, None] == dc[None]) & valid[None]
    cols = jnp.einsum("hdm,mqk->hdqk", rpb, jnp.asarray(pick, F32), precision=lax.Precision.HIGHEST)
    cols = jnp.where(valid[None, None], cols, MASK_BIAS)
    t = jnp.stack([cols[:, WIN_H - 1 - off:2 * WIN_H - 1 - off] for off in range(WIN_H)], axis=1)
    return t.transpose(0, 1, 3, 2, 4).reshape(rpb.shape[0], WIN_H, GRID_W, WIN_H * GRID_W)


def _row_mod(len_ctx, tm, ctx_ref, lat_ref, axis=1):
    row = pl.program_id(axis) * tm + lax.broadcasted_iota(jnp.int32, (tm, 1), 0)
    return jnp.where(row < len_ctx, ctx_ref[0], lat_ref[0])


def _row_mod_specs(d, k):
    return [pl.BlockSpec((1, 1, d), lambda b, i, s=s: ((b * 2 + s) * 6 + k, 0, 0)) for s in range(2)]


def _merge_kernel(len_ctx, *refs):
    a = _readout(*refs[:12])
    bp, cn, ga, gb, gc, x, g1c, g1l, wa, wb, wc, wo, o_ref = refs[12:]
    sig = lambda g: jax.nn.sigmoid(g[0].astype(F32))
    m = sig(ga) * _dot_bf(a, wa[...]) + sig(gb) * _dot_bf(bp[0], wb[...]) + sig(gc) * _dot_bf(cn[0], wc[...])
    o_ref[0] = x[0] + _row_mod(len_ctx, x.shape[1], g1c, g1l) * _dot_bf(m, wo[...])


def _merge(rwkv, bp, cn, gates, x, mod, lw, len_ctx):
    bsz, n, d = x.shape
    assert n % TM_ROW == 0
    tok = lambda w, col=0: pl.BlockSpec((1, TM_ROW, w), lambda b, i: (b, i, col))
    consts = [lw["gn_w"], lw["gn_b"], lw["r_k"], lw["gate_up"], lw["ones_bd"]]
    ws = [lw["w_rwkv_o"], lw["w_pool_o"], lw["w_nat_o"], lw["w_out"]]
    return pl.pallas_call(
        functools.partial(_merge_kernel, len_ctx),
        grid=(bsz, n // TM_ROW),
        in_specs=[tok(a.shape[2]) for a in rwkv] + [_const_spec(a.shape) for a in consts]
        + [tok(bp.shape[2]), tok(cn.shape[2]), tok(d, 0), tok(d, 1), tok(d, 2), tok(d)]
        + _row_mod_specs(d, 2) + [_const_spec(w.shape) for w in ws],
        out_specs=tok(d),
        out_shape=jax.ShapeDtypeStruct((bsz, n, d), F32),
        compiler_params=_params("parallel", "parallel"),
        name="merge",
    )(*rwkv, *consts, bp, cn, gates, gates, gates, x, mod, mod, *ws)


def _ffn_kernel(len_ctx, x_ref, sh_c, sh_l, sc_c, sc_l, g2_c, g2_l, gain, w1, w2, o_ref):
    x = x_ref[0]
    tm = x.shape[0]
    h = _norm_mod(x, gain[...], _row_mod(len_ctx, tm, sh_c, sh_l), _row_mod(len_ctx, tm, sc_c, sc_l))
    u = jnp.dot(h.astype(BF16), w1[...], preferred_element_type=F32)
    hid = w2.shape[0]
    gate, up = u[:, :hid], u[:, hid:]
    act = gate * jax.nn.sigmoid(gate) * up
    o_ref[0] = x + _row_mod(len_ctx, tm, g2_c, g2_l) * jnp.dot(act.astype(BF16), w2[...],
                                                              preferred_element_type=F32)


def _ffn(x, mod, lw, len_ctx, latent_only):
    bsz, n, d = x.shape
    tm = TM if latent_only else TM_ROW
    skip = len_ctx // tm if latent_only else 0
    assert n % tm == 0 and (len_ctx % tm == 0 or not latent_only)
    n_out = n - skip * tm
    return pl.pallas_call(
        functools.partial(_ffn_kernel, 0 if latent_only else len_ctx),
        grid=(bsz, n_out // tm),
        in_specs=[pl.BlockSpec((1, tm, d), lambda b, i: (b, i + skip, 0))]
        + _row_mod_specs(d, 3) + _row_mod_specs(d, 4) + _row_mod_specs(d, 5)
        + [_const_spec(lw["norm2"].shape), _const_spec(lw["w_ffn_in"].shape), _const_spec(lw["w_ffn_out"].shape)],
        out_specs=pl.BlockSpec((1, tm, d), lambda b, i: (b, i, 0)),
        out_shape=jax.ShapeDtypeStruct((bsz, n_out, d), F32),
        compiler_params=_params("parallel", "parallel"),
        name="ffn",
    )(x, *[mod] * 6, lw["norm2"], lw["w_ffn_in"], lw["w_ffn_out"])


def _block_diag(blocks):
    n = len(blocks)
    rows = []
    for i, blk in enumerate(blocks):
        rows.append(jnp.concatenate([blk if j == i else jnp.zeros((blk.shape[0], blocks[j].shape[1]), blk.dtype)
                                     for j in range(n)], axis=1))
    return jnp.concatenate(rows, axis=0)


def _pad_cols(a, width):
    return jnp.pad(a, ((0, 0), (0, width - a.shape[1])))


def _layer_weights(l, prm):
    c = prm["k_k"].shape[1]
    lora = prm["decay_up"].shape[2]
    gl = prm["gate_up"].shape[1]
    pool = prm["pool_scale"].shape[1]
    d = prm["w_out"].shape[1]
    w_in = prm["w_in"][l]
    o_lora, o_gd, o_pool = 3 * c, 3 * c + 4 * lora, 3 * c + 4 * lora + gl
    o_q = o_pool + pool
    o_gate = o_q + 3 * c
    assert 4 * lora == 2 * LANES and gl <= 2 * LANES and pool == 2 * LANES and o_gate + 3 * d == w_in.shape[1]
    w_mix = jnp.concatenate([
        w_in[:, 0:o_lora], w_in[:, o_q:o_gate], w_in[:, o_lora:o_gd],
        _pad_cols(w_in[:, o_gd:o_pool], 2 * LANES), w_in[:, o_pool:o_q]], axis=1).astype(BF16)
    mu = lambda m: _pad_cols(m[l][None, :o_pool], o_pool + 2 * LANES - gl)
    heads = c // HEAD
    row = lambda a: a.reshape(1, -1)
    return {
        "w_mix": w_mix, "w_gate": w_in[:, o_gate:].astype(BF16),
        "norm1": row(prm["norm1"][l]), "norm2": row(prm["norm2"][l]),
        "mu_prev": mu(prm["mu_prev"]), "mu_next": mu(prm["mu_next"]),
        "decay_up": _block_diag([prm["decay_up"][l, 0], prm["decay_up"][l, 1]]),
        "decay_w0": row(prm["decay_w0"][l]),
        "iclr_up": _block_diag([prm["iclr_up"][l, 0], prm["iclr_up"][l, 1]]),
        "iclr_a0": row(prm["iclr_a0"][l]),
        "k_k": row(prm["k_k"][l]), "k_a": row(prm["k_a"][l]), "r_k": row(prm["r_k"][l]),
        "gn_w": row(prm["gn_w"][l]), "gn_b": row(prm["gn_b"][l]),
        "gate_up": jnp.pad(prm["gate_up"][l], ((0, 2 * LANES - gl), (0, 0))),
        "ones_bd": jnp.kron(jnp.eye(heads, dtype=F32), jnp.ones((HEAD, HEAD), F32)).astype(BF16),
        "pool_w": _block_diag([prm["pool_w"][l, g] for g in range(len(POOL_WINDOWS))]),
        "pool_scale": row(prm["pool_scale"][l]),
        "q_gain": row(jnp.tile(prm["q_gain"][l], heads)), "k_gain": row(jnp.tile(prm["k_gain"][l], heads)),
        "nat_bias": _nat_bias_table(prm["rpb"][l]),
        "w_rwkv_o": prm["w_rwkv_o"][l].astype(BF16), "w_pool_o": prm["w_pool_o"][l].astype(BF16),
        "w_nat_o": prm["w_nat_o"][l].astype(BF16), "w_out": prm["w_out"][l].astype(BF16),
        "w_ffn_in": prm["w_ffn_in"][l].astype(BF16), "w_ffn_out": prm["w_ffn_out"][l].astype(BF16),
    }


def kernel(x, c, ctx, c_ctx, w_mod, b_mod, norm1, norm2, w_in, mu_prev, mu_next, decay_w0, decay_up, iclr_a0, iclr_up, gate_up, k_k, k_a, r_k, gn_w, gn_b, pool_w, pool_scale, q_gain, k_gain, rpb, w_rwkv_o, w_pool_o, w_nat_o, w_out, w_ffn_in, w_ffn_out):
    prm = dict(norm1=norm1, norm2=norm2, w_in=w_in, mu_prev=mu_prev, mu_next=mu_next, decay_w0=decay_w0,
               decay_up=decay_up, iclr_a0=iclr_a0, iclr_up=iclr_up, gate_up=gate_up, k_k=k_k, k_a=k_a, r_k=r_k,
               gn_w=gn_w, gn_b=gn_b, pool_w=pool_w, pool_scale=pool_scale, q_gain=q_gain, k_gain=k_gain, rpb=rpb,
               w_rwkv_o=w_rwkv_o, w_pool_o=w_pool_o, w_nat_o=w_nat_o, w_out=w_out, w_ffn_in=w_ffn_in,
               w_ffn_out=w_ffn_out)
    bsz, len_lat, d = x.shape
    len_ctx = ctx.shape[1]
    depth = w_mod.shape[0]
    assert len_ctx % TM == 0 and len_lat % TM == 0 and bsz + 1 <= SUBLANES
    nct = len_ctx // TM

    s_rows = jnp.concatenate([c, c_ctx[None, :], jnp.zeros((SUBLANES - bsz - 1, d), F32)], axis=0)
    mod_all = _modulation(s_rows, w_mod, b_mod)
    xa = jnp.concatenate([ctx, x], axis=1)

    for l in range(depth):
        lw = _layer_weights(l, prm)
        m_lat = mod_all[l, :bsz]
        m_ctx = jnp.broadcast_to(mod_all[l, bsz][None], m_lat.shape)
        mod = jnp.stack([m_ctx, m_lat], axis=1).reshape(bsz * 2 * 6, 1, d)

        p = _in_proj(xa, mod, lw["norm1"], lw["w_mix"], len_ctx, F32)
        gates = _in_proj(xa, mod, lw["norm1"], lw["w_gate"], len_ctx, BF16)
        ld_f, ld_b, kd_f, kd_b, b_f, b_b, v, kk, r, gd, b_br, qn, kn, vn = _mixer_prep(p, lw, nct, len_ctx, len_lat)
        y_f, y_b = _rwkv_scan(ld_f, ld_b, kd_f, kd_b, b_f, b_b, v, kk, r, len_ctx)
        c_br = _nat_attention(qn, kn, vn, lw["nat_bias"], len_ctx, len_lat)
        xa = _merge((y_f, y_b, r, kd_f, kd_b, v, gd), b_br, c_br, gates, xa, mod, lw, len_ctx)
        xa = _ffn(xa, mod, lw, len_ctx, latent_only=(l == depth - 1))
    return xa
```

```python
import functools

import numpy as np
import jax
import jax.numpy as jnp
from jax import lax
from jax.experimental import pallas as pl
from jax.experimental.pallas import tpu as pltpu

F32 = jnp.float32
BF16 = jnp.bfloat16

HEAD = 64
NORM_EPS = 1e-6
GN_EPS = 64e-5
KK_EPS = 1e-24
POOL_WINDOWS = (2, 4, 8, 16)
POOL_GROUP = 64
GRID_W = 64
WIN_H = 8
WIN_W = 16
MASK_BIAS = -1e30

LANES = 128
SUBLANES = 8
VMEM_LIMIT = 56 * 1024 * 1024

TM = 256
TM_IN = 1280
TM_ROW = 640
TN_IN = 3072
TN_MOD = 1536
CHUNK = 64
SCAN_BLOCK = 256
NAT_ROWS = 4


def _dot_hi(a, b):
    return jnp.dot(a, b, precision=lax.Precision.HIGHEST, preferred_element_type=F32)


def _dot_bf(a, b):
    return jnp.dot(a.astype(BF16), b.astype(BF16), preferred_element_type=F32)


def _dot_nt(a, b):
    return lax.dot_general(a, b, (((1,), (1,)), ((), ())), preferred_element_type=F32)


def _split(a):
    hi = a.astype(BF16)
    return hi, (a - hi.astype(F32)).astype(BF16)


def _mm(a, b, split_a=False, split_b=False):
    def halves(x, split):
        if x.dtype == BF16 or not split:
            return x.astype(BF16), None
        return _split(x)

    (ah, al), (bh, bl) = halves(a, split_a), halves(b, split_b)
    lhs, rhs = [ah], [bh]
    if al is not None:
        lhs.append(al)
        rhs.append(bh)
    if bl is not None:
        lhs.append(ah)
        rhs.append(bl)
    if len(lhs) == 1:
        return jnp.dot(ah, bh, preferred_element_type=F32)
    return jnp.dot(jnp.concatenate(lhs, axis=1), jnp.concatenate(rhs, axis=0), preferred_element_type=F32)


def _run_interleaved(units):
    done = {}
    while units:
        for k in list(units):
            try:
                next(units[k])
            except StopIteration as stop:
                done[k] = stop.value
                del units[k]
    return done


def _params(*sem):
    return pltpu.CompilerParams(dimension_semantics=sem, vmem_limit_bytes=VMEM_LIMIT)


def _const_spec(shape):
    nd = len(shape)
    return pl.BlockSpec(shape, lambda *_: (0,) * nd, pipeline_mode=pl.Buffered(1))


def _mod_kernel(s_ref, w_ref, b_ref, o_ref):
    s = s_ref[...]
    s = s * jax.nn.sigmoid(s)
    o_ref[0] = _dot_hi(s, w_ref[0]) + b_ref[0]


def _modulation(s_rows, w_mod, b_mod):
    depth, d, n = w_mod.shape
    return pl.pallas_call(
        _mod_kernel,
        grid=(depth, n // TN_MOD),
        in_specs=[
            pl.BlockSpec((SUBLANES, d), lambda l, j: (0, 0)),
            pl.BlockSpec((1, d, TN_MOD), lambda l, j: (l, 0, j)),
            pl.BlockSpec((1, 1, TN_MOD), lambda l, j: (l, 0, j)),
        ],
        out_specs=pl.BlockSpec((1, SUBLANES, TN_MOD), lambda l, j: (l, 0, j)),
        out_shape=jax.ShapeDtypeStruct((depth, SUBLANES, n), F32),
        compiler_params=_params("parallel", "parallel"),
        name="modulation",
    )(s_rows, w_mod, b_mod.reshape(depth, 1, n))


def _norm_mod(x, gain, shift, scale):
    ms = jnp.mean(x * x, axis=-1, keepdims=True)
    return x * lax.rsqrt(ms + NORM_EPS) * gain * (1.0 + scale) + shift


def _row_mod(len_ctx, tm, ctx_ref, lat_ref, axis=1):
    row = pl.program_id(axis) * tm + lax.broadcasted_iota(jnp.int32, (tm, 1), 0)
    return jnp.where(row < len_ctx, ctx_ref[0], lat_ref[0])


def _row_mod_specs(d, k):
    return [pl.BlockSpec((1, 1, d), lambda b, i, s=s: ((b * 2 + s) * 6 + k, 0, 0)) for s in range(2)]


def _in_proj_kernel(len_ctx, x_ref, sh_c, sc_c, sh_l, sc_l, g_ref, w_ref, o_ref):
    tm = x_ref.shape[1]
    shift = _row_mod(len_ctx, tm, sh_c, sh_l, axis=2)
    scale = _row_mod(len_ctx, tm, sc_c, sc_l, axis=2)
    h = _norm_mod(x_ref[0], g_ref[...], shift, scale)
    o_ref[0] = jnp.dot(h.astype(BF16), w_ref[...], preferred_element_type=F32).astype(o_ref.dtype)


def _in_proj(x, mod, gain, w, len_ctx, out_dtype):
    bsz, n, d = x.shape
    nout = w.shape[1]
    assert n % TM_IN == 0 and nout % TN_IN == 0
    mspec = lambda is_lat, k: pl.BlockSpec((1, 1, d), lambda j, b, i: ((b * 2 + is_lat) * 6 + k, 0, 0))
    return pl.pallas_call(
        functools.partial(_in_proj_kernel, len_ctx),
        grid=(nout // TN_IN, bsz, n // TM_IN),
        in_specs=[
            pl.BlockSpec((1, TM_IN, d), lambda j, b, i: (b, i, 0)),
            mspec(0, 0), mspec(0, 1), mspec(1, 0), mspec(1, 1),
            pl.BlockSpec((1, d), lambda j, b, i: (0, 0)),
            pl.BlockSpec((d, TN_IN), lambda j, b, i: (0, j), pipeline_mode=pl.Buffered(1 if nout == TN_IN else 2)),
        ],
        out_specs=pl.BlockSpec((1, TM_IN, TN_IN), lambda j, b, i: (b, i, j)),
        out_shape=jax.ShapeDtypeStruct((bsz, n, nout), out_dtype),
        compiler_params=_params("parallel", "parallel", "parallel"),
        name="in_proj",
    )(x, mod, mod, mod, mod, gain, w)


def _halo_specs(width, col, n):
    nb = n // SUBLANES
    per = TM // SUBLANES
    return [
        pl.BlockSpec((1, TM, width), lambda b, i: (b, i, col)),
        pl.BlockSpec((1, SUBLANES, width), lambda b, i: (b, jnp.maximum(i * per - 1, 0), col)),
        pl.BlockSpec((1, SUBLANES, width), lambda b, i: (b, jnp.minimum((i + 1) * per, nb - 1), col)),
    ]


def _prep_kernel(nct, nt, rkv_ref, rkv_p, rkv_n, lo_ref, lo_p, lo_n, gd_ref, gd_p, gd_n,
                 mup_ref, mun_ref, du_ref, w0_ref, au_ref, a0_ref, kkw_ref, ka_ref, ones_ref,
                 wf_ref, wb_ref, kdf_ref, kdb_ref, bf_ref, bb_ref, v_ref, kk_ref, r_ref, gdo_ref):
    i = pl.program_id(1)
    first = jnp.logical_or(i == 0, i == nct)
    last = jnp.logical_or(i == nct - 1, i == nt - 1)
    row = lax.broadcasted_iota(jnp.int32, (TM, 1), 0)
    c = rkv_ref.shape[2] // 3

    def mix(main, prev8, next8, lo, hi):
        pm = main[0]
        prow = jnp.where(first, 0.0, prev8[0, SUBLANES - 1:SUBLANES, :])
        nrow = jnp.where(last, 0.0, next8[0, 0:1, :])
        prev = jnp.where(row == 0, prow, pltpu.roll(pm, 1, 0))
        nxt = jnp.where(row == TM - 1, nrow, pltpu.roll(pm, TM - 1, 0))
        return pm + mup_ref[:, lo:hi] * (prev - pm) + mun_ref[:, lo:hi] * (nxt - pm)

    rkv = mix(rkv_ref, rkv_p, rkv_n, 0, 3 * c)
    lora = mix(lo_ref, lo_p, lo_n, 3 * c, 3 * c + 2 * LANES)
    gdo_ref[0] = mix(gd_ref, gd_p, gd_n, 3 * c + 2 * LANES, 3 * c + 4 * LANES)

    r, k, v = rkv[:, 0:c], rkv[:, c:2 * c], rkv[:, 2 * c:3 * c]
    wd, ad = lora[:, 0:LANES], lora[:, LANES:2 * LANES]
    dec = _mm(jnp.tanh(wd), du_ref[...], split_a=True, split_b=True) + w0_ref[...]
    log_decay = -float(np.exp(-0.5)) * jax.nn.sigmoid(dec)
    a = jax.nn.sigmoid(_mm(ad, au_ref[...], split_a=True, split_b=True) + a0_ref[...])
    kk = k * kkw_ref[...]
    kk = kk * lax.rsqrt(jnp.maximum(_mm(kk * kk, ones_ref[...]), KK_EPS))
    ka = ka_ref[...]
    for z, (w_o, kd_o, b_o) in enumerate(((wf_ref, kdf_ref, bf_ref), (wb_ref, kdb_ref, bb_ref))):
        az = a[:, z * c:(z + 1) * c]
        w_o[0] = log_decay[:, z * c:(z + 1) * c]
        kd_o[0] = k * (1.0 + (az - 1.0) * ka)
        b_o[0] = kk * az
    v_ref[0] = v
    kk_ref[0] = kk
    r_ref[0] = r


def _mixer_prep_kernel(nct, nt, len_ctx, len_lat, *refs):
    n_in = (18, 5, 6)
    n_out = (10, 1, 3)
    cuts = np.cumsum(n_in + n_out).tolist()
    prep_in, pool_in, qk_in, prep_out, pool_out, qk_out = [refs[a:b] for a, b in zip([0] + cuts[:-1], cuts)]
    _prep_kernel(nct, nt, *prep_in, *prep_out)
    _pool_kernel(nct, len_ctx, len_lat, *pool_in, *pool_out)
    _qk_norm_kernel(*qk_in, *qk_out)


def _mixer_prep(p, lw, nct, len_ctx, len_lat):
    bsz, n, _ = p.shape
    c = lw["k_k"].shape[1]
    nt = n // TM
    wide = 2 * LANES
    lora_col = (6 * c) // wide
    prep_consts = [lw["mu_prev"], lw["mu_next"], lw["decay_up"], lw["decay_w0"], lw["iclr_up"], lw["iclr_a0"],
                   lw["k_k"], lw["k_a"], lw["ones_bd"]]
    pool_consts = [lw["pool_w"], lw["pool_scale"]]
    qk_consts = [lw["q_gain"], lw["k_gain"], lw["ones_bd"]]
    consts = lambda arrs: [_const_spec(a.shape) for a in arrs]
    tok = lambda w: pl.BlockSpec((1, TM, w), lambda b, i: (b, i, 0))
    heads = c // HEAD
    head_major = pl.BlockSpec((1, heads, TM, HEAD), lambda b, i: (b, 0, i, 0))
    return pl.pallas_call(
        functools.partial(_mixer_prep_kernel, nct, nt, len_ctx, len_lat),
        grid=(bsz, nt),
        in_specs=_halo_specs(3 * c, 0, n) + _halo_specs(wide, lora_col, n) + _halo_specs(wide, lora_col + 1, n)
        + consts(prep_consts) + _halo_specs(wide, lora_col + 2, n) + consts(pool_consts)
        + [pl.BlockSpec((1, TM, c), lambda b, i, j=j: (b, i, 3 + j)) for j in range(3)] + consts(qk_consts),
        out_specs=[tok(c)] * 9 + [tok(wide), tok(wide)] + [head_major] * 3,
        out_shape=[jax.ShapeDtypeStruct((bsz, n, c), F32)] * 9 + [jax.ShapeDtypeStruct((bsz, n, wide), F32)] * 2
        + [jax.ShapeDtypeStruct((bsz, heads, n, HEAD), BF16)] * 3,
        compiler_params=_params("parallel", "parallel"),
        name="mixer_prep",
    )(*[p] * 9, *prep_consts, *[p] * 3, *pool_consts, *[p] * 3, *qk_consts)


def _chunk_scan_kernel(ldf, kdf, bf, vf, kkf, rf, ldb, kdb, bb, vb, kkb, rb, yf_ref, yb_ref, h_ref):
    @pl.when(pl.program_id(0) == 0)
    def _():
        h_ref[...] = jnp.zeros_like(h_ref)

    bsz, block_len, c = ldf.shape
    cs = CHUNK
    dirs = ((ldf, kdf, bf, vf, kkf, rf, yf_ref), (ldb, kdb, bb, vb, kkb, rb, yb_ref))
    t_i = lax.broadcasted_iota(jnp.int32, (cs, LANES), 0)
    lane = lax.broadcasted_iota(jnp.int32, (cs, LANES), 1)
    s_i = jnp.bitwise_and(lane, HEAD - 1)
    m_a = (lane < HEAD).astype(F32).astype(BF16)
    m_b = (lane >= HEAD).astype(F32).astype(BF16)
    r2 = lax.broadcasted_iota(jnp.int32, (LANES, LANES), 0)
    l2 = lax.broadcasted_iota(jnp.int32, (LANES, LANES), 1)
    block = ((r2 >= HEAD) == (l2 >= HEAD)).astype(F32)
    eye = (r2 == l2).astype(F32)
    zeros = jnp.zeros((cs, LANES), BF16)

    def bd(x):
        k = x.shape[1] // LANES
        return jnp.concatenate([x * jnp.concatenate([m_a] * k, axis=1), x * jnp.concatenate([m_b] * k, axis=1)],
                               axis=0)

    def unit(d, b, p, rows):
        ls = slice(p * LANES, (p + 1) * LANES)
        ld, kd, b_, v, kk, r = (ref[b, rows, ls] for ref in dirs[d][:6])
        before = (s_i < t_i) if d == 0 else (s_i > t_i)
        upto = (s_i <= t_i) if d == 0 else (s_i >= t_i)
        tri = upto.astype(F32).astype(BF16)
        lh, ll = _split(ld)
        big_l = jnp.dot(tri, jnp.concatenate([lh, ll], axis=0), preferred_element_type=F32)
        yield
        ltot = big_l[cs - 1:cs] if d == 0 else big_l[0:1]
        kap = kk * jnp.exp(big_l - ld)
        rt = r * jnp.exp(big_l)
        einv = jnp.exp(-big_l)
        kt, bt = kd * einv, b_ * einv
        efin = jnp.exp(ltot - big_l)
        khat, bhat = kd * efin, b_ * efin
        ktb, btb, vb16 = kt.astype(BF16), bt.astype(BF16), v.astype(BF16)
        sc = _dot_nt(jnp.concatenate([kap, rt], axis=0).astype(BF16),
                     jnp.concatenate([ktb * m_a, ktb * m_b, btb * m_a, btb * m_b], axis=0))
        yield
        a_k = jnp.where(before, sc[:cs, :LANES], 0.0)
        n_p = jnp.where(before, -sc[:cs, LANES:], 0.0)
        m_k = jnp.where(upto, sc[cs:, :LANES], 0.0)
        m_nb = jnp.where(upto, -sc[cs:, LANES:], 0.0)
        x = jnp.concatenate([kap, _mm(a_k, bd(vb16))], axis=1)
        yield
        for level in range(6):
            xh, xl = _split(x)
            nb = n_p.astype(BF16)
            x = x + jnp.dot(jnp.concatenate([nb, nb], axis=1), jnp.concatenate([bd(xh), bd(xl)], axis=0),
                            preferred_element_type=F32)
            if level < 5:
                n_p = jnp.dot(nb, bd(nb), preferred_element_type=F32)
            yield
        vz = jnp.concatenate([vb16, zeros], axis=1)
        uw = jnp.concatenate([x[:, LANES:], x[:, :LANES]], axis=1).astype(BF16)
        o1 = _mm(jnp.concatenate([m_k, m_nb], axis=1), jnp.concatenate([bd(vz), bd(uw)], axis=0))
        o2 = _mm(jnp.concatenate([khat, -bhat], axis=0).T, jnp.concatenate([vz, uw], axis=0))
        yield
        h = h_ref[b, d, p]
        y0, q = o1[:, :LANES], rt + o1[:, LANES:]
        psi = o2[:, :LANES] * block
        phi = o2[:, LANES:] * block + eye * jnp.exp(ltot)
        dirs[d][6][b, rows, ls] = y0 + _mm(q, h)
        h_ref[b, d, p] = _mm(phi, h, split_a=True, split_b=True) + psi

    n_sub = block_len // cs
    units = {}
    for s in range(n_sub):
        for d in range(2):
            ci = s if d == 0 else n_sub - 1 - s
            for b in range(bsz):
                for p in range(c // LANES):
                    units[s, d, b, p] = unit(d, b, p, slice(ci * cs, (ci + 1) * cs))
    _run_interleaved(units)


def _rwkv_scan(ld_f, ld_b, kd_f, kd_b, b_f, b_b, v, kk, r, len_ctx):
    bsz, n, c = v.shape
    assert len_ctx % SCAN_BLOCK == 0 and n % SCAN_BLOCK == 0 and SCAN_BLOCK % CHUNK == 0
    nc, nct_c = n // SCAN_BLOCK, len_ctx // SCAN_BLOCK

    def rev(g):
        return jnp.where(g < nct_c, nct_c - 1 - g, nc - 1 - g + nct_c)

    fwd = pl.BlockSpec((bsz, SCAN_BLOCK, c), lambda g: (0, g, 0))
    bwd = pl.BlockSpec((bsz, SCAN_BLOCK, c), lambda g: (0, rev(g), 0))
    tok = jax.ShapeDtypeStruct((bsz, n, c), F32)
    return pl.pallas_call(
        _chunk_scan_kernel,
        grid=(nc,),
        in_specs=[fwd] * 6 + [bwd] * 6,
        out_specs=[fwd, bwd],
        out_shape=[tok, tok],
        scratch_shapes=[pltpu.VMEM((bsz, 2, c // LANES, LANES, LANES), F32)],
        compiler_params=_params("arbitrary"),
        name="rwkv_scan",
    )(ld_f, kd_f, b_f, v, kk, r, ld_b, kd_b, b_b, v, kk, r)


def _readout(yf, yb, r, kdf, kdb, v, gd, gnw, gnb, rk, gup, ones_ref):
    ones = ones_ref[...]
    inv = 1.0 / HEAD
    y = yf[0] + yb[0]
    yc = y - _mm(y, ones, split_a=True) * inv
    var = _mm(yc * yc, ones) * inv
    yn = yc * lax.rsqrt(var + GN_EPS) * gnw[...] + gnb[...]
    bonus = _mm(r[0] * rk[...] * (kdf[0] + kdb[0]), ones) * v[0]
    return (yn + bonus) * _mm(jax.nn.sigmoid(gd[0]), gup[...])


def _pool_kernel(nct, len_ctx, len_lat, main, prev8, next8, pw_ref, scale_ref, o_ref):
    i = pl.program_id(1)
    is_lat = i >= nct
    seq_len = jnp.where(is_lat, len_lat, len_ctx)
    t0 = jnp.where(is_lat, i - nct, i) * TM
    n = TM + 2 * SUBLANES
    pm = main[0]
    ext = jnp.concatenate([prev8[0], pm, next8[0]], axis=0)
    pos = t0 - SUBLANES + lax.broadcasted_iota(jnp.int32, (n, 1), 0)
    e = jnp.where(jnp.logical_and(pos >= 0, pos < seq_len), ext, 0.0)
    a2 = e + pltpu.roll(e, 1, 0)
    a4 = pltpu.roll(a2, 1, 0) + pltpu.roll(a2, n - 1, 0)
    a8 = pltpu.roll(a4, 2, 0) + pltpu.roll(a4, n - 2, 0)
    a16 = pltpu.roll(a8, 4, 0) + pltpu.roll(a8, n - 4, 0)
    t = t0 + lax.broadcasted_iota(jnp.int32, (TM, 1), 0)
    lane = lax.broadcasted_iota(jnp.int32, pm.shape, 1)
    mean = None
    for g, (w, acc) in reversed(list(enumerate(zip(POOL_WINDOWS, (a2, a4, a8, a16))))):
        lo = jnp.maximum(t - w // 2, 0)
        hi = jnp.minimum(t + (w - w // 2) - 1, seq_len - 1)
        m = acc[SUBLANES:SUBLANES + TM] / (hi - lo + 1).astype(F32)
        mean = m if mean is None else jnp.where(lane < (g + 1) * POOL_GROUP, m, mean)
    o_ref[0] = _mm(mean - pm, pw_ref[...], split_a=True, split_b=True) * scale_ref[...]


def _qk_norm_kernel(q_ref, k_ref, v_ref, qg_ref, kg_ref, ones_ref, qo, ko, vo):
    ones = ones_ref[...]
    inv = 1.0 / HEAD
    q, k = q_ref[0], k_ref[0]
    qn = q * lax.rsqrt(_mm(q * q, ones) * inv + NORM_EPS) * qg_ref[...]
    kn = k * lax.rsqrt(_mm(k * k, ones) * inv + NORM_EPS) * kg_ref[...]
    for o_ref, val in ((qo, qn * HEAD ** -0.5), (ko, kn), (vo, v_ref[0])):
        val = val.astype(BF16)
        for h in range(o_ref.shape[1]):
            o_ref[0, h] = val[:, h * HEAD:(h + 1) * HEAD]


def _attend(q, key_sets, bias):
    scores = [_dot_nt(q, k) for k, _ in key_sets]
    yield
    if bias is not None:
        scores[0] = scores[0] + bias
    m = functools.reduce(jnp.maximum, [jnp.max(s, axis=-1, keepdims=True) for s in scores])
    yield
    ps = [jnp.exp(s - m) for s in scores]
    den = functools.reduce(jnp.add, [jnp.sum(p, axis=-1, keepdims=True) for p in ps])
    num = functools.reduce(jnp.add, [jnp.dot(p.astype(BF16), v, preferred_element_type=F32)
                                      for p, (_, v) in zip(ps, key_sets)])
    yield
    return num / den


def _nat_kernel(rows, q_ref, kp, kc, kn, vp, vc, vn, kx_ref, vx_ref, bias_ref, o_ref, ks, vs):
    i = pl.program_id(1) - 1
    heads, tq = q_ref.shape[1], q_ref.shape[2]
    rb = tq // GRID_W
    nloc = WIN_H * GRID_W

    @pl.when(i < 0)
    def _():
        units = {h: _attend(q_ref[0, h], [(kx_ref[0, h], vx_ref[0, h])], None) for h in range(heads)}
        for h, o in _run_interleaved(units).items():
            o_ref[0, :, h * HEAD:(h + 1) * HEAD] = o

    @pl.when(i >= 0)
    def _():
        for j, (kr, vr) in enumerate(((kp, vp), (kc, vc), (kn, vn))):
            ks[:, j * tq:(j + 1) * tq, :] = kr[0]
            vs[:, j * tq:(j + 1) * tq, :] = vr[0]
        for r0 in range(0, rb, NAT_ROWS):
            units = {}
            for rr in range(r0, r0 + NAT_ROWS):
                r = i * rb + rr
                rs = jnp.clip(r - WIN_H // 2, 0, rows - WIN_H)
                off = r - rs
                start = pl.multiple_of((rs - i * rb + rb) * GRID_W, GRID_W)
                qs = slice(rr * GRID_W, (rr + 1) * GRID_W)
                for h in range(heads):
                    units[rr, h] = _attend(q_ref[0, h, qs],
                                           [(ks[h, pl.ds(start, nloc)], vs[h, pl.ds(start, nloc)]),
                                            (kx_ref[0, h], vx_ref[0, h])], bias_ref[h, off])
            for (rr, h), o in _run_interleaved(units).items():
                o_ref[0, rr * GRID_W:(rr + 1) * GRID_W, h * HEAD:(h + 1) * HEAD] = o


def _nat_attention(qn, kn, vn, bias, len_ctx, len_lat):
    bsz, heads, n, hd = qn.shape
    tq = len_ctx
    assert tq == (WIN_H // 2) * GRID_W and len_lat % tq == 0
    rows = len_lat // GRID_W
    nblk = len_lat // tq

    def blk(shift):
        return pl.BlockSpec((1, heads, tq, hd),
                            lambda b, i: (b, 0, jnp.where(i == 0, 0, 1 + jnp.clip(i - 1 + shift, 0, nblk - 1)), 0))

    ctx = pl.BlockSpec((1, heads, tq, hd), lambda b, i: (b, 0, 0, 0))
    return pl.pallas_call(
        functools.partial(_nat_kernel, rows),
        grid=(bsz, 1 + nblk),
        in_specs=[blk(0), blk(-1), blk(0), blk(1), blk(-1), blk(0), blk(1), ctx, ctx, _const_spec(bias.shape)],
        out_specs=pl.BlockSpec((1, tq, heads * hd), lambda b, i: (b, i, 0)),
        out_shape=jax.ShapeDtypeStruct((bsz, n, heads * hd), F32),
        scratch_shapes=[pltpu.VMEM((heads, 3 * tq, hd), BF16), pltpu.VMEM((heads, 3 * tq, hd), BF16)],
        compiler_params=_params("parallel", "parallel"),
        name="nat_attention",
    )(qn, kn, kn, kn, vn, vn, vn, kn, vn, bias)


def _nat_bias_table(rpb):
    qc = np.arange(GRID_W)[:, None]
    kc = np.arange(GRID_W)[None, :]
    cs = np.clip(qc - WIN_W // 2, 0, GRID_W - WIN_W)
    valid = (kc >= cs) & (kc < cs + WIN_W)
    dc = kc - qc + WIN_W - 1
    pick = (np.arange(2 * WIN_W - 1)[:, None, None] == dc[None]) & valid[None]
    cols = jnp.einsum("hdm,mqk->hdqk", rpb, jnp.asarray(pick, F32), precision=lax.Precision.HIGHEST)
    cols = jnp.where(valid[None, None], cols, MASK_BIAS)
    t = jnp.stack([cols[:, WIN_H - 1 - off:2 * WIN_H - 1 - off] for off in range(WIN_H)], axis=1)
    return t.transpose(0, 1, 3, 2, 4).reshape(rpb.shape[0], WIN_H, GRID_W, WIN_H * GRID_W)


def _merge_kernel(len_ctx, *refs):
    a = _readout(*refs[:12])
    bp, cn, ga, gb, gc, x, g1c, g1l, wa, wb, wc, wo, o_ref = refs[12:]
    sig = lambda g: jax.nn.sigmoid(g[0].astype(F32))
    m = sig(ga) * _dot_bf(a, wa[...]) + sig(gb) * _dot_bf(bp[0], wb[...]) + sig(gc) * _dot_bf(cn[0], wc[...])
    o_ref[0] = x[0] + _row_mod(len_ctx, x.shape[1], g1c, g1l) * _dot_bf(m, wo[...])


def _merge(rwkv, bp, cn, gates, x, mod, lw, len_ctx):
    bsz, n, d = x.shape
    assert n % TM_ROW == 0
    tok = lambda w, col=0: pl.BlockSpec((1, TM_ROW, w), lambda b, i: (b, i, col))
    consts = [lw["gn_w"], lw["gn_b"], lw["r_k"], lw["gate_up"], lw["ones_bd"]]
    ws = [lw["w_rwkv_o"], lw["w_pool_o"], lw["w_nat_o"], lw["w_out"]]
    return pl.pallas_call(
        functools.partial(_merge_kernel, len_ctx),
        grid=(bsz, n // TM_ROW),
        in_specs=[tok(a.shape[2]) for a in rwkv] + [_const_spec(a.shape) for a in consts]
        + [tok(bp.shape[2]), tok(cn.shape[2]), tok(d, 0), tok(d, 1), tok(d, 2), tok(d)]
        + _row_mod_specs(d, 2) + [_const_spec(w.shape) for w in ws],
        out_specs=tok(d),
        out_shape=jax.ShapeDtypeStruct((bsz, n, d), F32),
        compiler_params=_params("parallel", "parallel"),
        name="merge",
    )(*rwkv, *consts, bp, cn, gates, gates, gates, x, mod, mod, *ws)


def _ffn_kernel(len_ctx, x_ref, sh_c, sh_l, sc_c, sc_l, g2_c, g2_l, gain, w1, w2, o_ref):
    x = x_ref[0]
    tm = x.shape[0]
    h = _norm_mod(x, gain[...], _row_mod(len_ctx, tm, sh_c, sh_l), _row_mod(len_ctx, tm, sc_c, sc_l))
    u = jnp.dot(h.astype(BF16), w1[...], preferred_element_type=F32)
    hid = w2.shape[0]
    gate, up = u[:, :hid], u[:, hid:]
    act = gate * jax.nn.sigmoid(gate) * up
    o_ref[0] = x + _row_mod(len_ctx, tm, g2_c, g2_l) * jnp.dot(act.astype(BF16), w2[...],
                                                              preferred_element_type=F32)


def _ffn(x, mod, lw, len_ctx, latent_only):
    bsz, n, d = x.shape
    tm = TM if latent_only else TM_ROW
    skip = len_ctx // tm if latent_only else 0
    assert n % tm == 0 and (len_ctx % tm == 0 or not latent_only)
    n_out = n - skip * tm
    return pl.pallas_call(
        functools.partial(_ffn_kernel, 0 if latent_only else len_ctx),
        grid=(bsz, n_out // tm),
        in_specs=[pl.BlockSpec((1, tm, d), lambda b, i: (b, i + skip, 0))]
        + _row_mod_specs(d, 3) + _row_mod_specs(d, 4) + _row_mod_specs(d, 5)
        + [_const_spec(lw["norm2"].shape), _const_spec(lw["w_ffn_in"].shape), _const_spec(lw["w_ffn_out"].shape)],
        out_specs=pl.BlockSpec((1, tm, d), lambda b, i: (b, i, 0)),
        out_shape=jax.ShapeDtypeStruct((bsz, n_out, d), F32),
        compiler_params=_params("parallel", "parallel"),
        name="ffn",
    )(x, *[mod] * 6, lw["norm2"], lw["w_ffn_in"], lw["w_ffn_out"])


def _block_diag(blocks):
    n = len(blocks)
    rows = []
    for i, blk in enumerate(blocks):
        rows.append(jnp.concatenate([blk if j == i else jnp.zeros((blk.shape[0], blocks[j].shape[1]), blk.dtype)
                                     for j in range(n)], axis=1))
    return jnp.concatenate(rows, axis=0)


def _pad_cols(a, width):
    return jnp.pad(a, ((0, 0), (0, width - a.shape[1])))


def _layer_weights(l, prm):
    c = prm["k_k"].shape[1]
    lora = prm["decay_up"].shape[2]
    gl = prm["gate_up"].shape[1]
    pool = prm["pool_scale"].shape[1]
    d = prm["w_out"].shape[1]
    w_in = prm["w_in"][l]
    o_lora, o_gd, o_pool = 3 * c, 3 * c + 4 * lora, 3 * c + 4 * lora + gl
    o_q = o_pool + pool
    o_gate = o_q + 3 * c
    assert 4 * lora == 2 * LANES and gl <= 2 * LANES and pool == 2 * LANES and o_gate + 3 * d == w_in.shape[1]
    w_mix = jnp.concatenate([
        w_in[:, 0:o_lora], w_in[:, o_q:o_gate], w_in[:, o_lora:o_gd],
        _pad_cols(w_in[:, o_gd:o_pool], 2 * LANES), w_in[:, o_pool:o_q]], axis=1).astype(BF16)
    mu = lambda m: _pad_cols(m[l][None, :o_pool], o_pool + 2 * LANES - gl)
    heads = c // HEAD
    row = lambda a: a.reshape(1, -1)
    return {
        "w_mix": w_mix, "w_gate": w_in[:, o_gate:].astype(BF16),
        "norm1": row(prm["norm1"][l]), "norm2": row(prm["norm2"][l]),
        "mu_prev": mu(prm["mu_prev"]), "mu_next": mu(prm["mu_next"]),
        "decay_up": _block_diag([prm["decay_up"][l, 0], prm["decay_up"][l, 1]]),
        "decay_w0": row(prm["decay_w0"][l]),
        "iclr_up": _block_diag([prm["iclr_up"][l, 0], prm["iclr_up"][l, 1]]),
        "iclr_a0": row(prm["iclr_a0"][l]),
        "k_k": row(prm["k_k"][l]), "k_a": row(prm["k_a"][l]), "r_k": row(prm["r_k"][l]),
        "gn_w": row(prm["gn_w"][l]), "gn_b": row(prm["gn_b"][l]),
        "gate_up": jnp.pad(prm["gate_up"][l], ((0, 2 * LANES - gl), (0, 0))),
        "ones_bd": jnp.kron(jnp.eye(heads, dtype=F32), jnp.ones((HEAD, HEAD), F32)).astype(BF16),
        "pool_w": _block_diag([prm["pool_w"][l, g] for g in range(len(POOL_WINDOWS))]),
        "pool_scale": row(prm["pool_scale"][l]),
        "q_gain": row(jnp.tile(prm["q_gain"][l], heads)), "k_gain": row(jnp.tile(prm["k_gain"][l], heads)),
        "nat_bias": _nat_bias_table(prm["rpb"][l]),
        "w_rwkv_o": prm["w_rwkv_o"][l].astype(BF16), "w_pool_o": prm["w_pool_o"][l].astype(BF16),
        "w_nat_o": prm["w_nat_o"][l].astype(BF16), "w_out": prm["w_out"][l].astype(BF16),
        "w_ffn_in": prm["w_ffn_in"][l].astype(BF16), "w_ffn_out": prm["w_ffn_out"][l].astype(BF16),
    }


def kernel(x, c, ctx, c_ctx, w_mod, b_mod, norm1, norm2, w_in, mu_prev, mu_next, decay_w0, decay_up, iclr_a0, iclr_up, gate_up, k_k, k_a, r_k, gn_w, gn_b, pool_w, pool_scale, q_gain, k_gain, rpb, w_rwkv_o, w_pool_o, w_nat_o, w_out, w_ffn_in, w_ffn_out):
    prm = dict(norm1=norm1, norm2=norm2, w_in=w_in, mu_prev=mu_prev, mu_next=mu_next, decay_w0=decay_w0,
               decay_up=decay_up, iclr_a0=iclr_a0, iclr_up=iclr_up, gate_up=gate_up, k_k=k_k, k_a=k_a, r_k=r_k,
               gn_w=gn_w, gn_b=gn_b, pool_w=pool_w, pool_scale=pool_scale, q_gain=q_gain, k_gain=k_gain, rpb=rpb,
               w_rwkv_o=w_rwkv_o, w_pool_o=w_pool_o, w_nat_o=w_nat_o, w_out=w_out, w_ffn_in=w_ffn_in,
               w_ffn_out=w_ffn_out)
    bsz, len_lat, d = x.shape
    len_ctx = ctx.shape[1]
    depth = w_mod.shape[0]
    assert len_ctx % TM == 0 and len_lat % TM == 0 and bsz + 1 <= SUBLANES
    if depth == 0:
        return x
    nct = len_ctx // TM

    s_rows = jnp.concatenate([c, c_ctx[None, :], jnp.zeros((SUBLANES - bsz - 1, d), F32)], axis=0)
    mod_all = _modulation(s_rows, w_mod, b_mod)
    xa = jnp.concatenate([ctx, x], axis=1)

    for l in range(depth):
        lw = _layer_weights(l, prm)
        m_lat = mod_all[l, :bsz]
        m_ctx = jnp.broadcast_to(mod_all[l, bsz][None], m_lat.shape)
        mod = jnp.stack([m_ctx, m_lat], axis=1).reshape(bsz * 2 * 6, 1, d)

        p = _in_proj(xa, mod, lw["norm1"], lw["w_mix"], len_ctx, F32)
        gates = _in_proj(xa, mod, lw["norm1"], lw["w_gate"], len_ctx, BF16)
        ld_f, ld_b, kd_f, kd_b, b_f, b_b, v, kk, r, gd, b_br, qn, kn, vn = _mixer_prep(p, lw, nct, len_ctx, len_lat)
        y_f, y_b = _rwkv_scan(ld_f, ld_b, kd_f, kd_b, b_f, b_b, v, kk, r, len_ctx)
        c_br = _nat_attention(qn, kn, vn, lw["nat_bias"], len_ctx, len_lat)
        xa = _merge((y_f, y_b, r, kd_f, kd_b, v, gd), b_br, c_br, gates, xa, mod, lw, len_ctx)
        xa = _ffn(xa, mod, lw, len_ctx, latent_only=(l == depth - 1))
    return xa
```

```python
import functools

import numpy as np
import jax
import jax.numpy as jnp
from jax import lax
from jax.experimental import pallas as pl
from jax.experimental.pallas import tpu as pltpu

F32 = jnp.float32
BF16 = jnp.bfloat16

HEAD = 64
NORM_EPS = 1e-6
GN_EPS = 64e-5
KK_EPS = 1e-24
POOL_WINDOWS = (2, 4, 8, 16)
POOL_GROUP = 64
GRID_W = 64
WIN_H = 8
WIN_W = 16
MASK_BIAS = -1e30

LANES = 128
SUBLANES = 8
VMEM_LIMIT = 56 * 1024 * 1024

TM = 256
TM_IN = 1280
TM_ROW = 640
TN_IN = 3072
TN_MOD = 1536
CHUNK = 64
SCAN_BLOCK = 256
SPLIT_LEVELS = 3
NAT_ROWS = 4


def _dot_hi(a, b):
    return jnp.dot(a, b, precision=lax.Precision.HIGHEST, preferred_element_type=F32)


def _dot_bf(a, b):
    return jnp.dot(a.astype(BF16), b.astype(BF16), preferred_element_type=F32)


def _dot_nt(a, b):
    return lax.dot_general(a, b, (((1,), (1,)), ((), ())), preferred_element_type=F32)


def _split(a):
    hi = a.astype(BF16)
    return hi, (a - hi.astype(F32)).astype(BF16)


def _mm(a, b, split_a=False, split_b=False):
    def halves(x, split):
        if x.dtype == BF16 or not split:
            return x.astype(BF16), None
        return _split(x)

    (ah, al), (bh, bl) = halves(a, split_a), halves(b, split_b)
    lhs, rhs = [ah], [bh]
    if al is not None:
        lhs.append(al)
        rhs.append(bh)
    if bl is not None:
        lhs.append(ah)
        rhs.append(bl)
    if len(lhs) == 1:
        return jnp.dot(ah, bh, preferred_element_type=F32)
    return jnp.dot(jnp.concatenate(lhs, axis=1), jnp.concatenate(rhs, axis=0), preferred_element_type=F32)


def _run_interleaved(units):
    done = {}
    while units:
        for k in list(units):
            try:
                next(units[k])
            except StopIteration as stop:
                done[k] = stop.value
                del units[k]
    return done


def _params(*sem):
    return pltpu.CompilerParams(dimension_semantics=sem, vmem_limit_bytes=VMEM_LIMIT)


def _const_spec(shape):
    nd = len(shape)
    return pl.BlockSpec(shape, lambda *_: (0,) * nd, pipeline_mode=pl.Buffered(1))


def _mod_kernel(s_ref, w_ref, b_ref, o_ref):
    s = s_ref[...]
    s = s * jax.nn.sigmoid(s)
    o_ref[0] = _dot_hi(s, w_ref[0]) + b_ref[0]


def _modulation(s_rows, w_mod, b_mod):
    depth, d, n = w_mod.shape
    return pl.pallas_call(
        _mod_kernel,
        grid=(depth, n // TN_MOD),
        in_specs=[
            pl.BlockSpec((SUBLANES, d), lambda l, j: (0, 0)),
            pl.BlockSpec((1, d, TN_MOD), lambda l, j: (l, 0, j)),
            pl.BlockSpec((1, 1, TN_MOD), lambda l, j: (l, 0, j)),
        ],
        out_specs=pl.BlockSpec((1, SUBLANES, TN_MOD), lambda l, j: (l, 0, j)),
        out_shape=jax.ShapeDtypeStruct((depth, SUBLANES, n), F32),
        compiler_params=_params("parallel", "parallel"),
        name="modulation",
    )(s_rows, w_mod, b_mod.reshape(depth, 1, n))


def _norm_mod(x, gain, shift, scale):
    ms = jnp.mean(x * x, axis=-1, keepdims=True)
    return x * lax.rsqrt(ms + NORM_EPS) * gain * (1.0 + scale) + shift


def _row_mod(len_ctx, tm, ctx_ref, lat_ref, axis=1):
    row = pl.program_id(axis) * tm + lax.broadcasted_iota(jnp.int32, (tm, 1), 0)
    return jnp.where(row < len_ctx, ctx_ref[0], lat_ref[0])


def _row_mod_specs(d, k):
    return [pl.BlockSpec((1, 1, d), lambda b, i, s=s: ((b * 2 + s) * 6 + k, 0, 0)) for s in range(2)]


def _in_proj_kernel(len_ctx, x_ref, sh_c, sc_c, sh_l, sc_l, g_ref, w_ref, o_ref):
    tm = x_ref.shape[1]
    shift = _row_mod(len_ctx, tm, sh_c, sh_l, axis=2)
    scale = _row_mod(len_ctx, tm, sc_c, sc_l, axis=2)
    h = _norm_mod(x_ref[0], g_ref[...], shift, scale)
    o_ref[0] = jnp.dot(h.astype(BF16), w_ref[...], preferred_element_type=F32).astype(o_ref.dtype)


def _in_proj(x, mod, gain, w, len_ctx, out_dtype):
    bsz, n, d = x.shape
    nout = w.shape[1]
    assert n % TM_IN == 0 and nout % TN_IN == 0
    mspec = lambda is_lat, k: pl.BlockSpec((1, 1, d), lambda j, b, i: ((b * 2 + is_lat) * 6 + k, 0, 0))
    return pl.pallas_call(
        functools.partial(_in_proj_kernel, len_ctx),
        grid=(nout // TN_IN, bsz, n // TM_IN),
        in_specs=[
            pl.BlockSpec((1, TM_IN, d), lambda j, b, i: (b, i, 0)),
            mspec(0, 0), mspec(0, 1), mspec(1, 0), mspec(1, 1),
            pl.BlockSpec((1, d), lambda j, b, i: (0, 0)),
            pl.BlockSpec((d, TN_IN), lambda j, b, i: (0, j), pipeline_mode=pl.Buffered(1 if nout == TN_IN else 2)),
        ],
        out_specs=pl.BlockSpec((1, TM_IN, TN_IN), lambda j, b, i: (b, i, j)),
        out_shape=jax.ShapeDtypeStruct((bsz, n, nout), out_dtype),
        compiler_params=_params("parallel", "parallel", "parallel"),
        name="in_proj",
    )(x, mod, mod, mod, mod, gain, w)


def _halo_specs(width, col, n):
    nb = n // SUBLANES
    per = TM // SUBLANES
    return [
        pl.BlockSpec((1, TM, width), lambda b, i: (b, i, col)),
        pl.BlockSpec((1, SUBLANES, width), lambda b, i: (b, jnp.maximum(i * per - 1, 0), col)),
        pl.BlockSpec((1, SUBLANES, width), lambda b, i: (b, jnp.minimum((i + 1) * per, nb - 1), col)),
    ]


def _prep_kernel(nct, nt, rkv_ref, rkv_p, rkv_n, lo_ref, lo_p, lo_n, gd_ref, gd_p, gd_n,
                 mup_ref, mun_ref, du_ref, w0_ref, au_ref, a0_ref, kkw_ref, ka_ref, ones_ref,
                 wf_ref, wb_ref, kdf_ref, kdb_ref, bf_ref, bb_ref, v_ref, kk_ref, r_ref, gdo_ref):
    i = pl.program_id(1)
    first = jnp.logical_or(i == 0, i == nct)
    last = jnp.logical_or(i == nct - 1, i == nt - 1)
    row = lax.broadcasted_iota(jnp.int32, (TM, 1), 0)
    c = rkv_ref.shape[2] // 3

    def mix(main, prev8, next8, lo, hi):
        pm = main[0]
        prow = jnp.where(first, 0.0, prev8[0, SUBLANES - 1:SUBLANES, :])
        nrow = jnp.where(last, 0.0, next8[0, 0:1, :])
        prev = jnp.where(row == 0, prow, pltpu.roll(pm, 1, 0))
        nxt = jnp.where(row == TM - 1, nrow, pltpu.roll(pm, TM - 1, 0))
        return pm + mup_ref[:, lo:hi] * (prev - pm) + mun_ref[:, lo:hi] * (nxt - pm)

    rkv = mix(rkv_ref, rkv_p, rkv_n, 0, 3 * c)
    lora = mix(lo_ref, lo_p, lo_n, 3 * c, 3 * c + 2 * LANES)
    gdo_ref[0] = mix(gd_ref, gd_p, gd_n, 3 * c + 2 * LANES, 3 * c + 4 * LANES)

    r, k, v = rkv[:, 0:c], rkv[:, c:2 * c], rkv[:, 2 * c:3 * c]
    wd, ad = lora[:, 0:LANES], lora[:, LANES:2 * LANES]
    dec = _mm(jnp.tanh(wd), du_ref[...], split_a=True, split_b=True) + w0_ref[...]
    log_decay = -float(np.exp(-0.5)) * jax.nn.sigmoid(dec)
    a = jax.nn.sigmoid(_mm(ad, au_ref[...], split_a=True, split_b=True) + a0_ref[...])
    kk = k * kkw_ref[...]
    kk = kk * lax.rsqrt(jnp.maximum(_mm(kk * kk, ones_ref[...]), KK_EPS))
    ka = ka_ref[...]
    for z, (w_o, kd_o, b_o) in enumerate(((wf_ref, kdf_ref, bf_ref), (wb_ref, kdb_ref, bb_ref))):
        az = a[:, z * c:(z + 1) * c]
        w_o[0] = log_decay[:, z * c:(z + 1) * c]
        kd_o[0] = k * (1.0 + (az - 1.0) * ka)
        b_o[0] = kk * az
    v_ref[0] = v
    kk_ref[0] = kk
    r_ref[0] = r


def _mixer_prep_kernel(nct, nt, len_ctx, len_lat, *refs):
    n_in = (18, 5, 6)
    n_out = (10, 1, 3)
    cuts = np.cumsum(n_in + n_out).tolist()
    prep_in, pool_in, qk_in, prep_out, pool_out, qk_out = [refs[a:b] for a, b in zip([0] + cuts[:-1], cuts)]
    _prep_kernel(nct, nt, *prep_in, *prep_out)
    _pool_kernel(nct, len_ctx, len_lat, *pool_in, *pool_out)
    _qk_norm_kernel(*qk_in, *qk_out)


def _mixer_prep(p, lw, nct, len_ctx, len_lat):
    bsz, n, _ = p.shape
    c = lw["k_k"].shape[1]
    nt = n // TM
    wide = 2 * LANES
    lora_col = (6 * c) // wide
    prep_consts = [lw["mu_prev"], lw["mu_next"], lw["decay_up"], lw["decay_w0"], lw["iclr_up"], lw["iclr_a0"],
                   lw["k_k"], lw["k_a"], lw["ones_bd"]]
    pool_consts = [lw["pool_w"], lw["pool_scale"]]
    qk_consts = [lw["q_gain"], lw["k_gain"], lw["ones_bd"]]
    consts = lambda arrs: [_const_spec(a.shape) for a in arrs]
    tok = lambda w: pl.BlockSpec((1, TM, w), lambda b, i: (b, i, 0))
    heads = c // HEAD
    head_major = pl.BlockSpec((1, heads, TM, HEAD), lambda b, i: (b, 0, i, 0))
    return pl.pallas_call(
        functools.partial(_mixer_prep_kernel, nct, nt, len_ctx, len_lat),
        grid=(bsz, nt),
        in_specs=_halo_specs(3 * c, 0, n) + _halo_specs(wide, lora_col, n) + _halo_specs(wide, lora_col + 1, n)
        + consts(prep_consts) + _halo_specs(wide, lora_col + 2, n) + consts(pool_consts)
        + [pl.BlockSpec((1, TM, c), lambda b, i, j=j: (b, i, 3 + j)) for j in range(3)] + consts(qk_consts),
        out_specs=[tok(c)] * 9 + [tok(wide), tok(wide)] + [head_major] * 3,
        out_shape=[jax.ShapeDtypeStruct((bsz, n, c), F32)] * 9 + [jax.ShapeDtypeStruct((bsz, n, wide), F32)] * 2
        + [jax.ShapeDtypeStruct((bsz, heads, n, HEAD), BF16)] * 3,
        compiler_params=_params("parallel", "parallel"),
        name="mixer_prep",
    )(*[p] * 9, *prep_consts, *[p] * 3, *pool_consts, *[p] * 3, *qk_consts)


def _chunk_scan_kernel(ldf, kdf, bf, vf, kkf, rf, ldb, kdb, bb, vb, kkb, rb, yf_ref, yb_ref, h_ref):
    @pl.when(pl.program_id(0) == 0)
    def _():
        h_ref[...] = jnp.zeros_like(h_ref)

    bsz, block_len, c = ldf.shape
    cs = CHUNK
    dirs = ((ldf, kdf, bf, vf, kkf, rf, yf_ref), (ldb, kdb, bb, vb, kkb, rb, yb_ref))
    t_i = lax.broadcasted_iota(jnp.int32, (cs, LANES), 0)
    lane = lax.broadcasted_iota(jnp.int32, (cs, LANES), 1)
    s_i = jnp.bitwise_and(lane, HEAD - 1)
    m_a = (lane < HEAD).astype(F32).astype(BF16)
    m_b = (lane >= HEAD).astype(F32).astype(BF16)
    r2 = lax.broadcasted_iota(jnp.int32, (LANES, LANES), 0)
    l2 = lax.broadcasted_iota(jnp.int32, (LANES, LANES), 1)
    block = ((r2 >= HEAD) == (l2 >= HEAD)).astype(F32)
    eye = (r2 == l2).astype(F32)
    zeros = jnp.zeros((cs, LANES), BF16)

    def bd(x):
        k = x.shape[1] // LANES
        return jnp.concatenate([x * jnp.concatenate([m_a] * k, axis=1), x * jnp.concatenate([m_b] * k, axis=1)],
                               axis=0)

    def unit(d, b, p, rows):
        ls = slice(p * LANES, (p + 1) * LANES)
        ld, kd, b_, v, kk, r = (ref[b, rows, ls] for ref in dirs[d][:6])
        before = (s_i < t_i) if d == 0 else (s_i > t_i)
        upto = (s_i <= t_i) if d == 0 else (s_i >= t_i)
        tri = upto.astype(F32).astype(BF16)
        lh, ll = _split(ld)
        big_l = jnp.dot(tri, jnp.concatenate([lh, ll], axis=0), preferred_element_type=F32)
        yield
        ltot = big_l[cs - 1:cs] if d == 0 else big_l[0:1]
        kap = kk * jnp.exp(big_l - ld)
        rt = r * jnp.exp(big_l)
        einv = jnp.exp(-big_l)
        kt, bt = kd * einv, b_ * einv
        efin = jnp.exp(ltot - big_l)
        khat, bhat = kd * efin, b_ * efin
        ktb, btb, vb16 = kt.astype(BF16), bt.astype(BF16), v.astype(BF16)
        sc = _dot_nt(jnp.concatenate([kap, rt], axis=0).astype(BF16),
                     jnp.concatenate([ktb * m_a, ktb * m_b, btb * m_a, btb * m_b], axis=0))
        yield
        a_k = jnp.where(before, sc[:cs, :LANES], 0.0)
        n_p = jnp.where(before, -sc[:cs, LANES:], 0.0)
        m_k = jnp.where(upto, sc[cs:, :LANES], 0.0)
        m_nb = jnp.where(upto, -sc[cs:, LANES:], 0.0)
        x = jnp.concatenate([kap, _mm(a_k, bd(vb16))], axis=1)
        yield
        for level in range(6):
            nb = n_p.astype(BF16)
            if level < SPLIT_LEVELS:
                xh, xl = _split(x)
                x = x + jnp.dot(jnp.concatenate([nb, nb], axis=1), jnp.concatenate([bd(xh), bd(xl)], axis=0),
                                preferred_element_type=F32)
            else:
                x = x + jnp.dot(nb, bd(x.astype(BF16)), preferred_element_type=F32)
            if level < 5:
                n_p = jnp.dot(nb, bd(nb), preferred_element_type=F32)
            yield
        vz = jnp.concatenate([vb16, zeros], axis=1)
        uw = jnp.concatenate([x[:, LANES:], x[:, :LANES]], axis=1).astype(BF16)
        o1 = _mm(jnp.concatenate([m_k, m_nb], axis=1), jnp.concatenate([bd(vz), bd(uw)], axis=0))
        o2 = _mm(jnp.concatenate([khat, -bhat], axis=0).T, jnp.concatenate([vz, uw], axis=0))
        yield
        h = h_ref[b, d, p]
        y0, q = o1[:, :LANES], rt + o1[:, LANES:]
        psi = o2[:, :LANES] * block
        phi = o2[:, LANES:] * block + eye * jnp.exp(ltot)
        dirs[d][6][b, rows, ls] = y0 + _mm(q, h)
        h_ref[b, d, p] = _mm(phi, h, split_a=True, split_b=True) + psi

    n_sub = block_len // cs
    units = {}
    for s in range(n_sub):
        for d in range(2):
            ci = s if d == 0 else n_sub - 1 - s
            for b in range(bsz):
                for p in range(c // LANES):
                    units[s, d, b, p] = unit(d, b, p, slice(ci * cs, (ci + 1) * cs))
    _run_interleaved(units)


def _rwkv_scan(ld_f, ld_b, kd_f, kd_b, b_f, b_b, v, kk, r, len_ctx):
    bsz, n, c = v.shape
    assert len_ctx % SCAN_BLOCK == 0 and n % SCAN_BLOCK == 0 and SCAN_BLOCK % CHUNK == 0
    nc, nct_c = n // SCAN_BLOCK, len_ctx // SCAN_BLOCK

    def rev(g):
        return jnp.where(g < nct_c, nct_c - 1 - g, nc - 1 - g + nct_c)

    fwd = pl.BlockSpec((bsz, SCAN_BLOCK, c), lambda g: (0, g, 0))
    bwd = pl.BlockSpec((bsz, SCAN_BLOCK, c), lambda g: (0, rev(g), 0))
    tok = jax.ShapeDtypeStruct((bsz, n, c), F32)
    return pl.pallas_call(
        _chunk_scan_kernel,
        grid=(nc,),
        in_specs=[fwd] * 6 + [bwd] * 6,
        out_specs=[fwd, bwd],
        out_shape=[tok, tok],
        scratch_shapes=[pltpu.VMEM((bsz, 2, c // LANES, LANES, LANES), F32)],
        compiler_params=_params("arbitrary"),
        name="rwkv_scan",
    )(ld_f, kd_f, b_f, v, kk, r, ld_b, kd_b, b_b, v, kk, r)


def _readout(yf, yb, r, kdf, kdb, v, gd, gnw, gnb, rk, gup, ones_ref):
    ones = ones_ref[...]
    inv = 1.0 / HEAD
    y = yf[0] + yb[0]
    yc = y - _mm(y, ones, split_a=True) * inv
    var = _mm(yc * yc, ones) * inv
    yn = yc * lax.rsqrt(var + GN_EPS) * gnw[...] + gnb[...]
    bonus = _mm(r[0] * rk[...] * (kdf[0] + kdb[0]), ones) * v[0]
    return (yn + bonus) * _mm(jax.nn.sigmoid(gd[0]), gup[...])


def _pool_kernel(nct, len_ctx, len_lat, main, prev8, next8, pw_ref, scale_ref, o_ref):
    i = pl.program_id(1)
    is_lat = i >= nct
    seq_len = jnp.where(is_lat, len_lat, len_ctx)
    t0 = jnp.where(is_lat, i - nct, i) * TM
    n = TM + 2 * SUBLANES
    pm = main[0]
    ext = jnp.concatenate([prev8[0], pm, next8[0]], axis=0)
    pos = t0 - SUBLANES + lax.broadcasted_iota(jnp.int32, (n, 1), 0)
    e = jnp.where(jnp.logical_and(pos >= 0, pos < seq_len), ext, 0.0)
    a2 = e + pltpu.roll(e, 1, 0)
    a4 = pltpu.roll(a2, 1, 0) + pltpu.roll(a2, n - 1, 0)
    a8 = pltpu.roll(a4, 2, 0) + pltpu.roll(a4, n - 2, 0)
    a16 = pltpu.roll(a8, 4, 0) + pltpu.roll(a8, n - 4, 0)
    t = t0 + lax.broadcasted_iota(jnp.int32, (TM, 1), 0)
    lane = lax.broadcasted_iota(jnp.int32, pm.shape, 1)
    mean = None
    for g, (w, acc) in reversed(list(enumerate(zip(POOL_WINDOWS, (a2, a4, a8, a16))))):
        lo = jnp.maximum(t - w // 2, 0)
        hi = jnp.minimum(t + (w - w // 2) - 1, seq_len - 1)
        m = acc[SUBLANES:SUBLANES + TM] / (hi - lo + 1).astype(F32)
        mean = m if mean is None else jnp.where(lane < (g + 1) * POOL_GROUP, m, mean)
    o_ref[0] = _mm(mean - pm, pw_ref[...], split_a=True, split_b=True) * scale_ref[...]


def _qk_norm_kernel(q_ref, k_ref, v_ref, qg_ref, kg_ref, ones_ref, qo, ko, vo):
    ones = ones_ref[...]
    inv = 1.0 / HEAD
    q, k = q_ref[0], k_ref[0]
    qn = q * lax.rsqrt(_mm(q * q, ones) * inv + NORM_EPS) * qg_ref[...]
    kn = k * lax.rsqrt(_mm(k * k, ones) * inv + NORM_EPS) * kg_ref[...]
    for o_ref, val in ((qo, qn * HEAD ** -0.5), (ko, kn), (vo, v_ref[0])):
        val = val.astype(BF16)
        for h in range(o_ref.shape[1]):
            o_ref[0, h] = val[:, h * HEAD:(h + 1) * HEAD]


def _attend(q, key_sets, bias):
    scores = [_dot_nt(q, k) for k, _ in key_sets]
    yield
    if bias is not None:
        scores[0] = scores[0] + bias
    m = functools.reduce(jnp.maximum, [jnp.max(s, axis=-1, keepdims=True) for s in scores])
    yield
    ps = [jnp.exp(s - m) for s in scores]
    den = functools.reduce(jnp.add, [jnp.sum(p, axis=-1, keepdims=True) for p in ps])
    num = functools.reduce(jnp.add, [jnp.dot(p.astype(BF16), v, preferred_element_type=F32)
                                      for p, (_, v) in zip(ps, key_sets)])
    yield
    return num / den


def _nat_kernel(rows, q_ref, kp, kc, kn, vp, vc, vn, kx_ref, vx_ref, bias_ref, o_ref, ks, vs):
    i = pl.program_id(1) - 1
    heads, tq = q_ref.shape[1], q_ref.shape[2]
    rb = tq // GRID_W
    nloc = WIN_H * GRID_W

    @pl.when(i < 0)
    def _():
        units = {h: _attend(q_ref[0, h], [(kx_ref[0, h], vx_ref[0, h])], None) for h in range(heads)}
        for h, o in _run_interleaved(units).items():
            o_ref[0, :, h * HEAD:(h + 1) * HEAD] = o

    @pl.when(i >= 0)
    def _():
        for j, (kr, vr) in enumerate(((kp, vp), (kc, vc), (kn, vn))):
            ks[:, j * tq:(j + 1) * tq, :] = kr[0]
            vs[:, j * tq:(j + 1) * tq, :] = vr[0]
        for r0 in range(0, rb, NAT_ROWS):
            units = {}
            for rr in range(r0, r0 + NAT_ROWS):
                r = i * rb + rr
                rs = jnp.clip(r - WIN_H // 2, 0, rows - WIN_H)
                off = r - rs
                start = pl.multiple_of((rs - i * rb + rb) * GRID_W, GRID_W)
                qs = slice(rr * GRID_W, (rr + 1) * GRID_W)
                for h in range(heads):
                    units[rr, h] = _attend(q_ref[0, h, qs],
                                           [(ks[h, pl.ds(start, nloc)], vs[h, pl.ds(start, nloc)]),
                                            (kx_ref[0, h], vx_ref[0, h])], bias_ref[h, off])
            for (rr, h), o in _run_interleaved(units).items():
                o_ref[0, rr * GRID_W:(rr + 1) * GRID_W, h * HEAD:(h + 1) * HEAD] = o


def _nat_attention(qn, kn, vn, bias, len_ctx, len_lat):
    bsz, heads, n, hd = qn.shape
    tq = len_ctx
    assert tq == (WIN_H // 2) * GRID_W and len_lat % tq == 0
    rows = len_lat // GRID_W
    nblk = len_lat // tq

    def blk(shift):
        return pl.BlockSpec((1, heads, tq, hd),
                            lambda b, i: (b, 0, jnp.where(i == 0, 0, 1 + jnp.clip(i - 1 + shift, 0, nblk - 1)), 0))

    ctx = pl.BlockSpec((1, heads, tq, hd), lambda b, i: (b, 0, 0, 0))
    return pl.pallas_call(
        functools.partial(_nat_kernel, rows),
        grid=(bsz, 1 + nblk),
        in_specs=[blk(0), blk(-1), blk(0), blk(1), blk(-1), blk(0), blk(1), ctx, ctx, _const_spec(bias.shape)],
        out_specs=pl.BlockSpec((1, tq, heads * hd), lambda b, i: (b, i, 0)),
        out_shape=jax.ShapeDtypeStruct((bsz, n, heads * hd), F32),
        scratch_shapes=[pltpu.VMEM((heads, 3 * tq, hd), BF16), pltpu.VMEM((heads, 3 * tq, hd), BF16)],
        compiler_params=_params("parallel", "parallel"),
        name="nat_attention",
    )(qn, kn, kn, kn, vn, vn, vn, kn, vn, bias)


def _nat_bias_table(rpb):
    qc = np.arange(GRID_W)[:, None]
    kc = np.arange(GRID_W)[None, :]
    cs = np.clip(qc - WIN_W // 2, 0, GRID_W - WIN_W)
    valid = (kc >= cs) & (kc < cs + WIN_W)
    dc = kc - qc + WIN_W - 1
    pick = (np.arange(2 * WIN_W - 1)[:, None, None] == dc[None]) & valid[None]
    cols = jnp.einsum("hdm,mqk->hdqk", rpb, jnp.asarray(pick, F32), precision=lax.Precision.HIGHEST)
    cols = jnp.where(valid[None, None], cols, MASK_BIAS)
    t = jnp.stack([cols[:, WIN_H - 1 - off:2 * WIN_H - 1 - off] for off in range(WIN_H)], axis=1)
    return t.transpose(0, 1, 3, 2, 4).reshape(rpb.shape[0], WIN_H, GRID_W, WIN_H * GRID_W)


def _merge_kernel(len_ctx, *refs):
    a = _readout(*refs[:12])
    bp, cn, ga, gb, gc, x, g1c, g1l, wa, wb, wc, wo, o_ref = refs[12:]
    sig = lambda g: jax.nn.sigmoid(g[0].astype(F32))
    m = sig(ga) * _dot_bf(a, wa[...]) + sig(gb) * _dot_bf(bp[0], wb[...]) + sig(gc) * _dot_bf(cn[0], wc[...])
    o_ref[0] = x[0] + _row_mod(len_ctx, x.shape[1], g1c, g1l) * _dot_bf(m, wo[...])


def _merge(rwkv, bp, cn, gates, x, mod, lw, len_ctx):
    bsz, n, d = x.shape
    assert n % TM_ROW == 0
    tok = lambda w, col=0: pl.BlockSpec((1, TM_ROW, w), lambda b, i: (b, i, col))
    consts = [lw["gn_w"], lw["gn_b"], lw["r_k"], lw["gate_up"], lw["ones_bd"]]
    ws = [lw["w_rwkv_o"], lw["w_pool_o"], lw["w_nat_o"], lw["w_out"]]
    return pl.pallas_call(
        functools.partial(_merge_kernel, len_ctx),
        grid=(bsz, n // TM_ROW),
        in_specs=[tok(a.shape[2]) for a in rwkv] + [_const_spec(a.shape) for a in consts]
        + [tok(bp.shape[2]), tok(cn.shape[2]), tok(d, 0), tok(d, 1), tok(d, 2), tok(d)]
        + _row_mod_specs(d, 2) + [_const_spec(w.shape) for w in ws],
        out_specs=tok(d),
        out_shape=jax.ShapeDtypeStruct((bsz, n, d), F32),
        compiler_params=_params("parallel", "parallel"),
        name="merge",
    )(*rwkv, *consts, bp, cn, gates, gates, gates, x, mod, mod, *ws)


def _ffn_kernel(len_ctx, x_ref, sh_c, sh_l, sc_c, sc_l, g2_c, g2_l, gain, w1, w2, o_ref):
    x = x_ref[0]
    tm = x.shape[0]
    h = _norm_mod(x, gain[...], _row_mod(len_ctx, tm, sh_c, sh_l), _row_mod(len_ctx, tm, sc_c, sc_l))
    u = jnp.dot(h.astype(BF16), w1[...], preferred_element_type=F32)
    hid = w2.shape[0]
    gate, up = u[:, :hid], u[:, hid:]
    act = gate * jax.nn.sigmoid(gate) * up
    o_ref[0] = x + _row_mod(len_ctx, tm, g2_c, g2_l) * jnp.dot(act.astype(BF16), w2[...],
                                                              preferred_element_type=F32)


def _ffn(x, mod, lw, len_ctx, latent_only):
    bsz, n, d = x.shape
    tm = TM if latent_only else TM_ROW
    skip = len_ctx // tm if latent_only else 0
    assert n % tm == 0 and (len_ctx % tm == 0 or not latent_only)
    n_out = n - skip * tm
    return pl.pallas_call(
        functools.partial(_ffn_kernel, 0 if latent_only else len_ctx),
        grid=(bsz, n_out // tm),
        in_specs=[pl.BlockSpec((1, tm, d), lambda b, i: (b, i + skip, 0))]
        + _row_mod_specs(d, 3) + _row_mod_specs(d, 4) + _row_mod_specs(d, 5)
        + [_const_spec(lw["norm2"].shape), _const_spec(lw["w_ffn_in"].shape), _const_spec(lw["w_ffn_out"].shape)],
        out_specs=pl.BlockSpec((1, tm, d), lambda b, i: (b, i, 0)),
        out_shape=jax.ShapeDtypeStruct((bsz, n_out, d), F32),
        compiler_params=_params("parallel", "parallel"),
        name="ffn",
    )(x, *[mod] * 6, lw["norm2"], lw["w_ffn_in"], lw["w_ffn_out"])


def _block_diag(blocks):
    n = len(blocks)
    rows = []
    for i, blk in enumerate(blocks):
        rows.append(jnp.concatenate([blk if j == i else jnp.zeros((blk.shape[0], blocks[j].shape[1]), blk.dtype)
                                     for j in range(n)], axis=1))
    return jnp.concatenate(rows, axis=0)


def _pad_cols(a, width):
    return jnp.pad(a, ((0, 0), (0, width - a.shape[1])))


def _layer_weights(l, prm):
    c = prm["k_k"].shape[1]
    lora = prm["decay_up"].shape[2]
    gl = prm["gate_up"].shape[1]
    pool = prm["pool_scale"].shape[1]
    d = prm["w_out"].shape[1]
    w_in = prm["w_in"][l]
    o_lora, o_gd, o_pool = 3 * c, 3 * c + 4 * lora, 3 * c + 4 * lora + gl
    o_q = o_pool + pool
    o_gate = o_q + 3 * c
    assert 4 * lora == 2 * LANES and gl <= 2 * LANES and pool == 2 * LANES and o_gate + 3 * d == w_in.shape[1]
    w_mix = jnp.concatenate([
        w_in[:, 0:o_lora], w_in[:, o_q:o_gate], w_in[:, o_lora:o_gd],
        _pad_cols(w_in[:, o_gd:o_pool], 2 * LANES), w_in[:, o_pool:o_q]], axis=1).astype(BF16)
    mu = lambda m: _pad_cols(m[l][None, :o_pool], o_pool + 2 * LANES - gl)
    heads = c // HEAD
    row = lambda a: a.reshape(1, -1)
    return {
        "w_mix": w_mix, "w_gate": w_in[:, o_gate:].astype(BF16),
        "norm1": row(prm["norm1"][l]), "norm2": row(prm["norm2"][l]),
        "mu_prev": mu(prm["mu_prev"]), "mu_next": mu(prm["mu_next"]),
        "decay_up": _block_diag([prm["decay_up"][l, 0], prm["decay_up"][l, 1]]),
        "decay_w0": row(prm["decay_w0"][l]),
        "iclr_up": _block_diag([prm["iclr_up"][l, 0], prm["iclr_up"][l, 1]]),
        "iclr_a0": row(prm["iclr_a0"][l]),
        "k_k": row(prm["k_k"][l]), "k_a": row(prm["k_a"][l]), "r_k": row(prm["r_k"][l]),
        "gn_w": row(prm["gn_w"][l]), "gn_b": row(prm["gn_b"][l]),
        "gate_up": jnp.pad(prm["gate_up"][l], ((0, 2 * LANES - gl), (0, 0))),
        "ones_bd": jnp.kron(jnp.eye(heads, dtype=F32), jnp.ones((HEAD, HEAD), F32)).astype(BF16),
        "pool_w": _block_diag([prm["pool_w"][l, g] for g in range(len(POOL_WINDOWS))]),
        "pool_scale": row(prm["pool_scale"][l]),
        "q_gain": row(jnp.tile(prm["q_gain"][l], heads)), "k_gain": row(jnp.tile(prm["k_gain"][l], heads)),
        "nat_bias": _nat_bias_table(prm["rpb"][l]),
        "w_rwkv_o": prm["w_rwkv_o"][l].astype(BF16), "w_pool_o": prm["w_pool_o"][l].astype(BF16),
        "w_nat_o": prm["w_nat_o"][l].astype(BF16), "w_out": prm["w_out"][l].astype(BF16),
        "w_ffn_in": prm["w_ffn_in"][l].astype(BF16), "w_ffn_out": prm["w_ffn_out"][l].astype(BF16),
    }


def kernel(x, c, ctx, c_ctx, w_mod, b_mod, norm1, norm2, w_in, mu_prev, mu_next, decay_w0, decay_up, iclr_a0, iclr_up, gate_up, k_k, k_a, r_k, gn_w, gn_b, pool_w, pool_scale, q_gain, k_gain, rpb, w_rwkv_o, w_pool_o, w_nat_o, w_out, w_ffn_in, w_ffn_out):
    prm = dict(norm1=norm1, norm2=norm2, w_in=w_in, mu_prev=mu_prev, mu_next=mu_next, decay_w0=decay_w0,
               decay_up=decay_up, iclr_a0=iclr_a0, iclr_up=iclr_up, gate_up=gate_up, k_k=k_k, k_a=k_a, r_k=r_k,
               gn_w=gn_w, gn_b=gn_b, pool_w=pool_w, pool_scale=pool_scale, q_gain=q_gain, k_gain=k_gain, rpb=rpb,
               w_rwkv_o=w_rwkv_o, w_pool_o=w_pool_o, w_nat_o=w_nat_o, w_out=w_out, w_ffn_in=w_ffn_in,
               w_ffn_out=w_ffn_out)
    bsz, len_lat, d = x.shape
    len_ctx = ctx.shape[1]
    depth = w_mod.shape[0]
    assert len_ctx % TM == 0 and len_lat % TM == 0 and bsz + 1 <= SUBLANES
    if depth == 0:
        return x
    nct = len_ctx // TM

    s_rows = jnp.concatenate([c, c_ctx[None, :], jnp.zeros((SUBLANES - bsz - 1, d), F32)], axis=0)
    mod_all = _modulation(s_rows, w_mod, b_mod)
    xa = jnp.concatenate([ctx, x], axis=1)

    for l in range(depth):
        lw = _layer_weights(l, prm)
        m_lat = mod_all[l, :bsz]
        m_ctx = jnp.broadcast_to(mod_all[l, bsz][None], m_lat.shape)
        mod = jnp.stack([m_ctx, m_lat], axis=1).reshape(bsz * 2 * 6, 1, d)

        p = _in_proj(xa, mod, lw["norm1"], lw["w_mix"], len_ctx, F32)
        gates = _in_proj(xa, mod, lw["norm1"], lw["w_gate"], len_ctx, BF16)
        ld_f, ld_b, kd_f, kd_b, b_f, b_b, v, kk, r, gd, b_br, qn, kn, vn = _mixer_prep(p, lw, nct, len_ctx, len_lat)
        y_f, y_b = _rwkv_scan(ld_f, ld_b, kd_f, kd_b, b_f, b_b, v, kk, r, len_ctx)
        c_br = _nat_attention(qn, kn, vn, lw["nat_bias"], len_ctx, len_lat)
        xa = _merge((y_f, y_b, r, kd_f, kd_b, v, gd), b_br, c_br, gates, xa, mod, lw, len_ctx)
        xa = _ffn(xa, mod, lw, len_ctx, latent_only=(l == depth - 1))
    return xa
```

```python
import functools

import numpy as np
import jax
import jax.numpy as jnp
from jax import lax
from jax.experimental import pallas as pl
from jax.experimental.pallas import tpu as pltpu

F32 = jnp.float32
BF16 = jnp.bfloat16

HEAD = 64
NORM_EPS = 1e-6
GN_EPS = 64e-5
KK_EPS = 1e-24
POOL_WINDOWS = (2, 4, 8, 16)
POOL_GROUP = 64
GRID_W = 64
WIN_H = 8
WIN_W = 16
MASK_BIAS = -1e30

LANES = 128
SUBLANES = 8
VMEM_LIMIT = 56 * 1024 * 1024

TM = 256
TM_IN = 1280
TM_ROW = 640
TN_IN = 3072
TN_MOD = 1536
CHUNK = 64
SCAN_BLOCK = 256
SPLIT_LEVELS = 3
NAT_ROWS = 4


def _dot_hi(a, b):
    return jnp.dot(a, b, precision=lax.Precision.HIGHEST, preferred_element_type=F32)


def _dot_bf(a, b):
    return jnp.dot(a.astype(BF16), b.astype(BF16), preferred_element_type=F32)


def _dot_nt(a, b):
    return lax.dot_general(a, b, (((1,), (1,)), ((), ())), preferred_element_type=F32)


def _split(a):
    hi = a.astype(BF16)
    return hi, (a - hi.astype(F32)).astype(BF16)


def _mm(a, b, split_a=False, split_b=False):
    def halves(x, split):
        if isinstance(x, tuple):
            return x
        if x.dtype == BF16 or not split:
            return x.astype(BF16), None
        return _split(x)

    (ah, al), (bh, bl) = halves(a, split_a), halves(b, split_b)
    lhs, rhs = [ah], [bh]
    if al is not None:
        lhs.append(al)
        rhs.append(bh)
    if bl is not None:
        lhs.append(ah)
        rhs.append(bl)
    if len(lhs) == 1:
        return jnp.dot(ah, bh, preferred_element_type=F32)
    return jnp.dot(jnp.concatenate(lhs, axis=1), jnp.concatenate(rhs, axis=0), preferred_element_type=F32)


def _run_interleaved(units):
    done = {}
    while units:
        for k in list(units):
            try:
                next(units[k])
            except StopIteration as stop:
                done[k] = stop.value
                del units[k]
    return done


def _params(*sem):
    return pltpu.CompilerParams(dimension_semantics=sem, vmem_limit_bytes=VMEM_LIMIT)


def _const_spec(shape):
    nd = len(shape)
    return pl.BlockSpec(shape, lambda *_: (0,) * nd, pipeline_mode=pl.Buffered(1))


def _mod_kernel(s_ref, w_ref, b_ref, o_ref):
    s = s_ref[...]
    s = s * jax.nn.sigmoid(s)
    o_ref[0] = _dot_hi(s, w_ref[0]) + b_ref[0]


def _modulation(s_rows, w_mod, b_mod):
    depth, d, n = w_mod.shape
    return pl.pallas_call(
        _mod_kernel,
        grid=(depth, n // TN_MOD),
        in_specs=[
            pl.BlockSpec((SUBLANES, d), lambda l, j: (0, 0)),
            pl.BlockSpec((1, d, TN_MOD), lambda l, j: (l, 0, j)),
            pl.BlockSpec((1, 1, TN_MOD), lambda l, j: (l, 0, j)),
        ],
        out_specs=pl.BlockSpec((1, SUBLANES, TN_MOD), lambda l, j: (l, 0, j)),
        out_shape=jax.ShapeDtypeStruct((depth, SUBLANES, n), F32),
        compiler_params=_params("parallel", "parallel"),
        name="modulation",
    )(s_rows, w_mod, b_mod.reshape(depth, 1, n))


def _norm_mod(x, gain, shift, scale):
    ms = jnp.mean(x * x, axis=-1, keepdims=True)
    return x * lax.rsqrt(ms + NORM_EPS) * gain * (1.0 + scale) + shift


def _row_mod(len_ctx, tm, ctx_ref, lat_ref, axis=1):
    row = pl.program_id(axis) * tm + lax.broadcasted_iota(jnp.int32, (tm, 1), 0)
    return jnp.where(row < len_ctx, ctx_ref[0], lat_ref[0])


def _row_mod_specs(d, k):
    return [pl.BlockSpec((1, 1, d), lambda b, i, s=s: ((b * 2 + s) * 6 + k, 0, 0)) for s in range(2)]


def _in_proj_kernel(len_ctx, x_ref, sh_c, sc_c, sh_l, sc_l, g_ref, w_ref, o_ref):
    tm = x_ref.shape[1]
    shift = _row_mod(len_ctx, tm, sh_c, sh_l, axis=2)
    scale = _row_mod(len_ctx, tm, sc_c, sc_l, axis=2)
    h = _norm_mod(x_ref[0], g_ref[...], shift, scale)
    o_ref[0] = jnp.dot(h.astype(BF16), w_ref[...], preferred_element_type=F32).astype(o_ref.dtype)


def _in_proj(x, mod, gain, w, len_ctx, out_dtype):
    bsz, n, d = x.shape
    nout = w.shape[1]
    assert n % TM_IN == 0 and nout % TN_IN == 0
    mspec = lambda is_lat, k: pl.BlockSpec((1, 1, d), lambda j, b, i: ((b * 2 + is_lat) * 6 + k, 0, 0))
    return pl.pallas_call(
        functools.partial(_in_proj_kernel, len_ctx),
        grid=(nout // TN_IN, bsz, n // TM_IN),
        in_specs=[
            pl.BlockSpec((1, TM_IN, d), lambda j, b, i: (b, i, 0)),
            mspec(0, 0), mspec(0, 1), mspec(1, 0), mspec(1, 1),
            pl.BlockSpec((1, d), lambda j, b, i: (0, 0)),
            pl.BlockSpec((d, TN_IN), lambda j, b, i: (0, j), pipeline_mode=pl.Buffered(1 if nout == TN_IN else 2)),
        ],
        out_specs=pl.BlockSpec((1, TM_IN, TN_IN), lambda j, b, i: (b, i, j)),
        out_shape=jax.ShapeDtypeStruct((bsz, n, nout), out_dtype),
        compiler_params=_params("parallel", "parallel", "parallel"),
        name="in_proj",
    )(x, mod, mod, mod, mod, gain, w)


def _halo_specs(width, col, n):
    nb = n // SUBLANES
    per = TM // SUBLANES
    return [
        pl.BlockSpec((1, TM, width), lambda b, i: (b, i, col)),
        pl.BlockSpec((1, SUBLANES, width), lambda b, i: (b, jnp.maximum(i * per - 1, 0), col)),
        pl.BlockSpec((1, SUBLANES, width), lambda b, i: (b, jnp.minimum((i + 1) * per, nb - 1), col)),
    ]


def _prep_kernel(nct, nt, rkv_ref, rkv_p, rkv_n, lo_ref, lo_p, lo_n, gd_ref, gd_p, gd_n,
                 mup_ref, mun_ref, du_ref, w0_ref, au_ref, a0_ref, kkw_ref, ka_ref, ones_ref,
                 wf_ref, wb_ref, kdf_ref, kdb_ref, bf_ref, bb_ref, v_ref, kk_ref, r_ref, gdo_ref):
    i = pl.program_id(1)
    first = jnp.logical_or(i == 0, i == nct)
    last = jnp.logical_or(i == nct - 1, i == nt - 1)
    row = lax.broadcasted_iota(jnp.int32, (TM, 1), 0)
    c = rkv_ref.shape[2] // 3

    def mix(main, prev8, next8, lo, hi):
        pm = main[0]
        prow = jnp.where(first, 0.0, prev8[0, SUBLANES - 1:SUBLANES, :])
        nrow = jnp.where(last, 0.0, next8[0, 0:1, :])
        prev = jnp.where(row == 0, prow, pltpu.roll(pm, 1, 0))
        nxt = jnp.where(row == TM - 1, nrow, pltpu.roll(pm, TM - 1, 0))
        return pm + mup_ref[:, lo:hi] * (prev - pm) + mun_ref[:, lo:hi] * (nxt - pm)

    rkv = mix(rkv_ref, rkv_p, rkv_n, 0, 3 * c)
    lora = mix(lo_ref, lo_p, lo_n, 3 * c, 3 * c + 2 * LANES)
    gdo_ref[0] = mix(gd_ref, gd_p, gd_n, 3 * c + 2 * LANES, 3 * c + 4 * LANES)

    r, k, v = rkv[:, 0:c], rkv[:, c:2 * c], rkv[:, 2 * c:3 * c]
    wd, ad = lora[:, 0:LANES], lora[:, LANES:2 * LANES]
    dec = _mm(jnp.tanh(wd), (du_ref[0], du_ref[1]), split_a=True) + w0_ref[...]
    log_decay = -float(np.exp(-0.5)) * jax.nn.sigmoid(dec)
    a = jax.nn.sigmoid(_mm(ad, (au_ref[0], au_ref[1]), split_a=True) + a0_ref[...])
    kk = k * kkw_ref[...]
    kk = kk * lax.rsqrt(jnp.maximum(_mm(kk * kk, ones_ref[...]), KK_EPS))
    ka = ka_ref[...]
    for z, (w_o, kd_o, b_o) in enumerate(((wf_ref, kdf_ref, bf_ref), (wb_ref, kdb_ref, bb_ref))):
        az = a[:, z * c:(z + 1) * c]
        w_o[0] = log_decay[:, z * c:(z + 1) * c]
        kd_o[0] = k * (1.0 + (az - 1.0) * ka)
        b_o[0] = kk * az
    v_ref[0] = v
    kk_ref[0] = kk
    r_ref[0] = r


def _mixer_prep_kernel(nct, nt, len_ctx, len_lat, *refs):
    n_in = (18, 5, 6)
    n_out = (10, 1, 3)
    cuts = np.cumsum(n_in + n_out).tolist()
    prep_in, pool_in, qk_in, prep_out, pool_out, qk_out = [refs[a:b] for a, b in zip([0] + cuts[:-1], cuts)]
    _prep_kernel(nct, nt, *prep_in, *prep_out)
    _pool_kernel(nct, len_ctx, len_lat, *pool_in, *pool_out)
    _qk_norm_kernel(*qk_in, *qk_out)


def _mixer_prep(p, lw, nct, len_ctx, len_lat):
    bsz, n, _ = p.shape
    c = lw["k_k"].shape[1]
    nt = n // TM
    wide = 2 * LANES
    lora_col = (6 * c) // wide
    prep_consts = [lw["mu_prev"], lw["mu_next"], lw["decay_up"], lw["decay_w0"], lw["iclr_up"], lw["iclr_a0"],
                   lw["k_k"], lw["k_a"], lw["ones_bd"]]
    pool_consts = [lw["pool_w"], lw["pool_scale"]]
    qk_consts = [lw["q_gain"], lw["k_gain"], lw["ones_bd"]]
    consts = lambda arrs: [_const_spec(a.shape) for a in arrs]
    tok = lambda w: pl.BlockSpec((1, TM, w), lambda b, i: (b, i, 0))
    heads = c // HEAD
    head_major = pl.BlockSpec((1, heads, TM, HEAD), lambda b, i: (b, 0, i, 0))
    return pl.pallas_call(
        functools.partial(_mixer_prep_kernel, nct, nt, len_ctx, len_lat),
        grid=(bsz, nt),
        in_specs=_halo_specs(3 * c, 0, n) + _halo_specs(wide, lora_col, n) + _halo_specs(wide, lora_col + 1, n)
        + consts(prep_consts) + _halo_specs(wide, lora_col + 2, n) + consts(pool_consts)
        + [pl.BlockSpec((1, TM, c), lambda b, i, j=j: (b, i, 3 + j)) for j in range(3)] + consts(qk_consts),
        out_specs=[tok(c)] * 9 + [tok(wide), tok(wide)] + [head_major] * 3,
        out_shape=[jax.ShapeDtypeStruct((bsz, n, c), F32)] * 9 + [jax.ShapeDtypeStruct((bsz, n, wide), F32)] * 2
        + [jax.ShapeDtypeStruct((bsz, heads, n, HEAD), BF16)] * 3,
        compiler_params=_params("parallel", "parallel"),
        name="mixer_prep",
    )(*[p] * 9, *prep_consts, *[p] * 3, *pool_consts, *[p] * 3, *qk_consts)


def _chunk_scan_kernel(ldf, kdf, bf, vf, kkf, rf, ldb, kdb, bb, vb, kkb, rb, yf_ref, yb_ref, h_ref):
    @pl.when(pl.program_id(0) == 0)
    def _():
        h_ref[...] = jnp.zeros_like(h_ref)

    bsz, block_len, c = ldf.shape
    cs = CHUNK
    dirs = ((ldf, kdf, bf, vf, kkf, rf, yf_ref), (ldb, kdb, bb, vb, kkb, rb, yb_ref))
    t_i = lax.broadcasted_iota(jnp.int32, (cs, LANES), 0)
    lane = lax.broadcasted_iota(jnp.int32, (cs, LANES), 1)
    s_i = jnp.bitwise_and(lane, HEAD - 1)
    m_a = (lane < HEAD).astype(F32).astype(BF16)
    m_b = (lane >= HEAD).astype(F32).astype(BF16)
    r2 = lax.broadcasted_iota(jnp.int32, (LANES, LANES), 0)
    l2 = lax.broadcasted_iota(jnp.int32, (LANES, LANES), 1)
    block = ((r2 >= HEAD) == (l2 >= HEAD)).astype(F32)
    eye = (r2 == l2).astype(F32)
    zeros = jnp.zeros((cs, LANES), BF16)

    def bd(x):
        k = x.shape[1] // LANES
        return jnp.concatenate([x * jnp.concatenate([m_a] * k, axis=1), x * jnp.concatenate([m_b] * k, axis=1)],
                               axis=0)

    def unit(d, b, p, rows):
        ls = slice(p * LANES, (p + 1) * LANES)
        ld, kd, b_, v, kk, r = (ref[b, rows, ls] for ref in dirs[d][:6])
        before = (s_i < t_i) if d == 0 else (s_i > t_i)
        upto = (s_i <= t_i) if d == 0 else (s_i >= t_i)
        tri = upto.astype(F32).astype(BF16)
        lh, ll = _split(ld)
        big_l = jnp.dot(tri, jnp.concatenate([lh, ll], axis=0), preferred_element_type=F32)
        yield
        ltot = big_l[cs - 1:cs] if d == 0 else big_l[0:1]
        kap = kk * jnp.exp(big_l - ld)
        rt = r * jnp.exp(big_l)
        einv = jnp.exp(-big_l)
        kt, bt = kd * einv, b_ * einv
        efin = jnp.exp(ltot - big_l)
        khat, bhat = kd * efin, b_ * efin
        ktb, btb, vb16 = kt.astype(BF16), bt.astype(BF16), v.astype(BF16)
        sc = _dot_nt(jnp.concatenate([kap, rt], axis=0).astype(BF16),
                     jnp.concatenate([ktb * m_a, ktb * m_b, btb * m_a, btb * m_b], axis=0))
        yield
        a_k = jnp.where(before, sc[:cs, :LANES], 0.0)
        n_p = jnp.where(before, -sc[:cs, LANES:], 0.0)
        m_k = jnp.where(upto, sc[cs:, :LANES], 0.0)
        m_nb = jnp.where(upto, -sc[cs:, LANES:], 0.0)
        x = jnp.concatenate([kap, _mm(a_k, bd(vb16))], axis=1)
        yield
        for level in range(6):
            nb = n_p.astype(BF16)
            if level < SPLIT_LEVELS:
                xh, xl = _split(x)
                x = x + jnp.dot(jnp.concatenate([nb, nb], axis=1), jnp.concatenate([bd(xh), bd(xl)], axis=0),
                                preferred_element_type=F32)
            else:
                x = x + jnp.dot(nb, bd(x.astype(BF16)), preferred_element_type=F32)
            if level < 5:
                n_p = jnp.dot(nb, bd(nb), preferred_element_type=F32)
            yield
        vz = jnp.concatenate([vb16, zeros], axis=1)
        uw = jnp.concatenate([x[:, LANES:], x[:, :LANES]], axis=1).astype(BF16)
        o1 = _mm(jnp.concatenate([m_k, m_nb], axis=1), jnp.concatenate([bd(vz), bd(uw)], axis=0))
        o2 = _mm(jnp.concatenate([khat, -bhat], axis=0).T, jnp.concatenate([vz, uw], axis=0))
        yield
        h = h_ref[b, d, p]
        y0, q = o1[:, :LANES], rt + o1[:, LANES:]
        psi = o2[:, :LANES] * block
        phi = o2[:, LANES:] * block + eye * jnp.exp(ltot)
        dirs[d][6][b, rows, ls] = y0 + _mm(q, h)
        h_ref[b, d, p] = _mm(phi, h, split_a=True, split_b=True) + psi

    n_sub = block_len // cs
    units = {}
    for s in range(n_sub):
        for d in range(2):
            ci = s if d == 0 else n_sub - 1 - s
            for b in range(bsz):
                for p in range(c // LANES):
                    units[s, d, b, p] = unit(d, b, p, slice(ci * cs, (ci + 1) * cs))
    _run_interleaved(units)


def _rwkv_scan(ld_f, ld_b, kd_f, kd_b, b_f, b_b, v, kk, r, len_ctx):
    bsz, n, c = v.shape
    assert len_ctx % SCAN_BLOCK == 0 and n % SCAN_BLOCK == 0 and SCAN_BLOCK % CHUNK == 0
    nc, nct_c = n // SCAN_BLOCK, len_ctx // SCAN_BLOCK

    def rev(g):
        return jnp.where(g < nct_c, nct_c - 1 - g, nc - 1 - g + nct_c)

    fwd = pl.BlockSpec((bsz, SCAN_BLOCK, c), lambda g: (0, g, 0))
    bwd = pl.BlockSpec((bsz, SCAN_BLOCK, c), lambda g: (0, rev(g), 0))
    tok = jax.ShapeDtypeStruct((bsz, n, c), F32)
    return pl.pallas_call(
        _chunk_scan_kernel,
        grid=(nc,),
        in_specs=[fwd] * 6 + [bwd] * 6,
        out_specs=[fwd, bwd],
        out_shape=[tok, tok],
        scratch_shapes=[pltpu.VMEM((bsz, 2, c // LANES, LANES, LANES), F32)],
        compiler_params=_params("arbitrary"),
        name="rwkv_scan",
    )(ld_f, kd_f, b_f, v, kk, r, ld_b, kd_b, b_b, v, kk, r)


def _readout(yf, yb, r, kdf, kdb, v, gd, gnw, gnb, rk, gup, ones_ref):
    ones = ones_ref[...]
    inv = 1.0 / HEAD
    y = yf[0] + yb[0]
    yc = y - _mm(y, ones, split_a=True) * inv
    var = _mm(yc * yc, ones) * inv
    yn = yc * lax.rsqrt(var + GN_EPS) * gnw[...] + gnb[...]
    bonus = _mm(r[0] * rk[...] * (kdf[0] + kdb[0]), ones) * v[0]
    return (yn + bonus) * _mm(jax.nn.sigmoid(gd[0]), gup[...])


def _pool_kernel(nct, len_ctx, len_lat, main, prev8, next8, pw_ref, scale_ref, o_ref):
    i = pl.program_id(1)
    is_lat = i >= nct
    seq_len = jnp.where(is_lat, len_lat, len_ctx)
    t0 = jnp.where(is_lat, i - nct, i) * TM
    n = TM + 2 * SUBLANES
    pm = main[0]
    ext = jnp.concatenate([prev8[0], pm, next8[0]], axis=0)
    pos = t0 - SUBLANES + lax.broadcasted_iota(jnp.int32, (n, 1), 0)
    e = jnp.where(jnp.logical_and(pos >= 0, pos < seq_len), ext, 0.0)
    a2 = e + pltpu.roll(e, 1, 0)
    a4 = pltpu.roll(a2, 1, 0) + pltpu.roll(a2, n - 1, 0)
    a8 = pltpu.roll(a4, 2, 0) + pltpu.roll(a4, n - 2, 0)
    a16 = pltpu.roll(a8, 4, 0) + pltpu.roll(a8, n - 4, 0)
    t = t0 + lax.broadcasted_iota(jnp.int32, (TM, 1), 0)
    lane = lax.broadcasted_iota(jnp.int32, pm.shape, 1)
    mean = None
    for g, (w, acc) in reversed(list(enumerate(zip(POOL_WINDOWS, (a2, a4, a8, a16))))):
        lo = jnp.maximum(t - w // 2, 0)
        hi = jnp.minimum(t + (w - w // 2) - 1, seq_len - 1)
        m = acc[SUBLANES:SUBLANES + TM] / (hi - lo + 1).astype(F32)
        mean = m if mean is None else jnp.where(lane < (g + 1) * POOL_GROUP, m, mean)
    o_ref[0] = _mm(mean - pm, (pw_ref[0], pw_ref[1]), split_a=True) * scale_ref[...]


def _qk_norm_kernel(q_ref, k_ref, v_ref, qg_ref, kg_ref, ones_ref, qo, ko, vo):
    ones = ones_ref[...]
    inv = 1.0 / HEAD
    q, k = q_ref[0], k_ref[0]
    qn = q * lax.rsqrt(_mm(q * q, ones) * inv + NORM_EPS) * qg_ref[...]
    kn = k * lax.rsqrt(_mm(k * k, ones) * inv + NORM_EPS) * kg_ref[...]
    for o_ref, val in ((qo, qn * HEAD ** -0.5), (ko, kn), (vo, v_ref[0])):
        val = val.astype(BF16)
        for h in range(o_ref.shape[1]):
            o_ref[0, h] = val[:, h * HEAD:(h + 1) * HEAD]


def _attend(q, key_sets, bias):
    scores = [_dot_nt(q, k) for k, _ in key_sets]
    yield
    if bias is not None:
        scores[0] = scores[0] + bias
    m = functools.reduce(jnp.maximum, [jnp.max(s, axis=-1, keepdims=True) for s in scores])
    yield
    ps = [jnp.exp(s - m) for s in scores]
    den = functools.reduce(jnp.add, [jnp.sum(p, axis=-1, keepdims=True) for p in ps])
    num = functools.reduce(jnp.add, [jnp.dot(p.astype(BF16), v, preferred_element_type=F32)
                                      for p, (_, v) in zip(ps, key_sets)])
    yield
    return num / den


def _nat_kernel(rows, q_ref, kp, kc, kn, vp, vc, vn, kx_ref, vx_ref, bias_ref, o_ref, ks, vs):
    i = pl.program_id(1) - 1
    heads, tq = q_ref.shape[1], q_ref.shape[2]
    rb = tq // GRID_W
    nloc = WIN_H * GRID_W

    @pl.when(i < 0)
    def _():
        units = {h: _attend(q_ref[0, h], [(kx_ref[0, h], vx_ref[0, h])], None) for h in range(heads)}
        for h, o in _run_interleaved(units).items():
            o_ref[0, :, h * HEAD:(h + 1) * HEAD] = o

    @pl.when(i >= 0)
    def _():
        for j, (kr, vr) in enumerate(((kp, vp), (kc, vc), (kn, vn))):
            ks[:, j * tq:(j + 1) * tq, :] = kr[0]
            vs[:, j * tq:(j + 1) * tq, :] = vr[0]
        for r0 in range(0, rb, NAT_ROWS):
            units = {}
            for rr in range(r0, r0 + NAT_ROWS):
                r = i * rb + rr
                rs = jnp.clip(r - WIN_H // 2, 0, rows - WIN_H)
                off = r - rs
                start = pl.multiple_of((rs - i * rb + rb) * GRID_W, GRID_W)
                qs = slice(rr * GRID_W, (rr + 1) * GRID_W)
                for h in range(heads):
                    units[rr, h] = _attend(q_ref[0, h, qs],
                                           [(ks[h, pl.ds(start, nloc)], vs[h, pl.ds(start, nloc)]),
                                            (kx_ref[0, h], vx_ref[0, h])], bias_ref[h, off])
            for (rr, h), o in _run_interleaved(units).items():
                o_ref[0, rr * GRID_W:(rr + 1) * GRID_W, h * HEAD:(h + 1) * HEAD] = o


def _nat_attention(qn, kn, vn, bias, len_ctx, len_lat):
    bsz, heads, n, hd = qn.shape
    tq = len_ctx
    assert tq == (WIN_H // 2) * GRID_W and len_lat % tq == 0
    rows = len_lat // GRID_W
    nblk = len_lat // tq

    def blk(shift):
        return pl.BlockSpec((1, heads, tq, hd),
                            lambda b, i: (b, 0, jnp.where(i == 0, 0, 1 + jnp.clip(i - 1 + shift, 0, nblk - 1)), 0))

    ctx = pl.BlockSpec((1, heads, tq, hd), lambda b, i: (b, 0, 0, 0))
    return pl.pallas_call(
        functools.partial(_nat_kernel, rows),
        grid=(bsz, 1 + nblk),
        in_specs=[blk(0), blk(-1), blk(0), blk(1), blk(-1), blk(0), blk(1), ctx, ctx, _const_spec(bias.shape)],
        out_specs=pl.BlockSpec((1, tq, heads * hd), lambda b, i: (b, i, 0)),
        out_shape=jax.ShapeDtypeStruct((bsz, n, heads * hd), F32),
        scratch_shapes=[pltpu.VMEM((heads, 3 * tq, hd), BF16), pltpu.VMEM((heads, 3 * tq, hd), BF16)],
        compiler_params=_params("parallel", "parallel"),
        name="nat_attention",
    )(qn, kn, kn, kn, vn, vn, vn, kn, vn, bias)


def _nat_bias_table(rpb):
    qc = np.arange(GRID_W)[:, None]
    kc = np.arange(GRID_W)[None, :]
    cs = np.clip(qc - WIN_W // 2, 0, GRID_W - WIN_W)
    valid = (kc >= cs) & (kc < cs + WIN_W)
    dc = kc - qc + WIN_W - 1
    pick = (np.arange(2 * WIN_W - 1)[:, None, None] == dc[None]) & valid[None]
    cols = jnp.einsum("hdm,mqk->hdqk", rpb, jnp.asarray(pick, F32), precision=lax.Precision.HIGHEST)
    cols = jnp.where(valid[None, None], cols, MASK_BIAS)
    t = jnp.stack([cols[:, WIN_H - 1 - off:2 * WIN_H - 1 - off] for off in range(WIN_H)], axis=1)
    return t.transpose(0, 1, 3, 2, 4).reshape(rpb.shape[0], WIN_H, GRID_W, WIN_H * GRID_W)


def _merge_kernel(len_ctx, *refs):
    a = _readout(*refs[:12])
    bp, cn, ga, gb, gc, x, g1c, g1l, wa, wb, wc, wo, o_ref = refs[12:]
    sig = lambda g: jax.nn.sigmoid(g[0].astype(F32))
    m = sig(ga) * _dot_bf(a, wa[...]) + sig(gb) * _dot_bf(bp[0], wb[...]) + sig(gc) * _dot_bf(cn[0], wc[...])
    o_ref[0] = x[0] + _row_mod(len_ctx, x.shape[1], g1c, g1l) * _dot_bf(m, wo[...])


def _merge(rwkv, bp, cn, gates, x, mod, lw, len_ctx):
    bsz, n, d = x.shape
    assert n % TM_ROW == 0
    tok = lambda w, col=0: pl.BlockSpec((1, TM_ROW, w), lambda b, i: (b, i, col))
    consts = [lw["gn_w"], lw["gn_b"], lw["r_k"], lw["gate_up"], lw["ones_bd"]]
    ws = [lw["w_rwkv_o"], lw["w_pool_o"], lw["w_nat_o"], lw["w_out"]]
    return pl.pallas_call(
        functools.partial(_merge_kernel, len_ctx),
        grid=(bsz, n // TM_ROW),
        in_specs=[tok(a.shape[2]) for a in rwkv] + [_const_spec(a.shape) for a in consts]
        + [tok(bp.shape[2]), tok(cn.shape[2]), tok(d, 0), tok(d, 1), tok(d, 2), tok(d)]
        + _row_mod_specs(d, 2) + [_const_spec(w.shape) for w in ws],
        out_specs=tok(d),
        out_shape=jax.ShapeDtypeStruct((bsz, n, d), F32),
        compiler_params=_params("parallel", "parallel"),
        name="merge",
    )(*rwkv, *consts, bp, cn, gates, gates, gates, x, mod, mod, *ws)


def _ffn_kernel(len_ctx, x_ref, sh_c, sh_l, sc_c, sc_l, g2_c, g2_l, gain, w1, w2, o_ref):
    x = x_ref[0]
    tm = x.shape[0]
    h = _norm_mod(x, gain[...], _row_mod(len_ctx, tm, sh_c, sh_l), _row_mod(len_ctx, tm, sc_c, sc_l))
    u = jnp.dot(h.astype(BF16), w1[...], preferred_element_type=F32)
    hid = w2.shape[0]
    gate, up = u[:, :hid], u[:, hid:]
    act = gate * jax.nn.sigmoid(gate) * up
    o_ref[0] = x + _row_mod(len_ctx, tm, g2_c, g2_l) * jnp.dot(act.astype(BF16), w2[...],
                                                              preferred_element_type=F32)


def _ffn(x, mod, lw, len_ctx, latent_only):
    bsz, n, d = x.shape
    tm = TM if latent_only else TM_ROW
    skip = len_ctx // tm if latent_only else 0
    assert n % tm == 0 and (len_ctx % tm == 0 or not latent_only)
    n_out = n - skip * tm
    return pl.pallas_call(
        functools.partial(_ffn_kernel, 0 if latent_only else len_ctx),
        grid=(bsz, n_out // tm),
        in_specs=[pl.BlockSpec((1, tm, d), lambda b, i: (b, i + skip, 0))]
        + _row_mod_specs(d, 3) + _row_mod_specs(d, 4) + _row_mod_specs(d, 5)
        + [_const_spec(lw["norm2"].shape), _const_spec(lw["w_ffn_in"].shape), _const_spec(lw["w_ffn_out"].shape)],
        out_specs=pl.BlockSpec((1, tm, d), lambda b, i: (b, i, 0)),
        out_shape=jax.ShapeDtypeStruct((bsz, n_out, d), F32),
        compiler_params=_params("parallel", "parallel"),
        name="ffn",
    )(x, *[mod] * 6, lw["norm2"], lw["w_ffn_in"], lw["w_ffn_out"])


def _block_diag(blocks):
    n = len(blocks)
    rows = []
    for i, blk in enumerate(blocks):
        rows.append(jnp.concatenate([blk if j == i else jnp.zeros((blk.shape[0], blocks[j].shape[1]), blk.dtype)
                                     for j in range(n)], axis=1))
    return jnp.concatenate(rows, axis=0)


def _pad_cols(a, width):
    return jnp.pad(a, ((0, 0), (0, width - a.shape[1])))


def _layer_weights(l, prm):
    c = prm["k_k"].shape[1]
    lora = prm["decay_up"].shape[2]
    gl = prm["gate_up"].shape[1]
    pool = prm["pool_scale"].shape[1]
    d = prm["w_out"].shape[1]
    w_in = prm["w_in"][l]
    o_lora, o_gd, o_pool = 3 * c, 3 * c + 4 * lora, 3 * c + 4 * lora + gl
    o_q = o_pool + pool
    o_gate = o_q + 3 * c
    assert 4 * lora == 2 * LANES and gl <= 2 * LANES and pool == 2 * LANES and o_gate + 3 * d == w_in.shape[1]
    w_mix = jnp.concatenate([
        w_in[:, 0:o_lora], w_in[:, o_q:o_gate], w_in[:, o_lora:o_gd],
        _pad_cols(w_in[:, o_gd:o_pool], 2 * LANES), w_in[:, o_pool:o_q]], axis=1).astype(BF16)
    mu = lambda m: _pad_cols(m[l][None, :o_pool], o_pool + 2 * LANES - gl)
    heads = c // HEAD
    row = lambda a: a.reshape(1, -1)
    hi_lo = lambda a: jnp.stack(_split(a))
    return {
        "w_mix": w_mix, "w_gate": w_in[:, o_gate:].astype(BF16),
        "norm1": row(prm["norm1"][l]), "norm2": row(prm["norm2"][l]),
        "mu_prev": mu(prm["mu_prev"]), "mu_next": mu(prm["mu_next"]),
        "decay_up": hi_lo(_block_diag([prm["decay_up"][l, 0], prm["decay_up"][l, 1]])),
        "decay_w0": row(prm["decay_w0"][l]),
        "iclr_up": hi_lo(_block_diag([prm["iclr_up"][l, 0], prm["iclr_up"][l, 1]])),
        "iclr_a0": row(prm["iclr_a0"][l]),
        "k_k": row(prm["k_k"][l]), "k_a": row(prm["k_a"][l]), "r_k": row(prm["r_k"][l]),
        "gn_w": row(prm["gn_w"][l]), "gn_b": row(prm["gn_b"][l]),
        "gate_up": jnp.pad(prm["gate_up"][l], ((0, 2 * LANES - gl), (0, 0))),
        "ones_bd": jnp.kron(jnp.eye(heads, dtype=F32), jnp.ones((HEAD, HEAD), F32)).astype(BF16),
        "pool_w": hi_lo(_block_diag([prm["pool_w"][l, g] for g in range(len(POOL_WINDOWS))])),
        "pool_scale": row(prm["pool_scale"][l]),
        "q_gain": row(jnp.tile(prm["q_gain"][l], heads)), "k_gain": row(jnp.tile(prm["k_gain"][l], heads)),
        "nat_bias": _nat_bias_table(prm["rpb"][l]),
        "w_rwkv_o": prm["w_rwkv_o"][l].astype(BF16), "w_pool_o": prm["w_pool_o"][l].astype(BF16),
        "w_nat_o": prm["w_nat_o"][l].astype(BF16), "w_out": prm["w_out"][l].astype(BF16),
        "w_ffn_in": prm["w_ffn_in"][l].astype(BF16), "w_ffn_out": prm["w_ffn_out"][l].astype(BF16),
    }


def kernel(x, c, ctx, c_ctx, w_mod, b_mod, norm1, norm2, w_in, mu_prev, mu_next, decay_w0, decay_up, iclr_a0, iclr_up, gate_up, k_k, k_a, r_k, gn_w, gn_b, pool_w, pool_scale, q_gain, k_gain, rpb, w_rwkv_o, w_pool_o, w_nat_o, w_out, w_ffn_in, w_ffn_out):
    prm = dict(norm1=norm1, norm2=norm2, w_in=w_in, mu_prev=mu_prev, mu_next=mu_next, decay_w0=decay_w0,
               decay_up=decay_up, iclr_a0=iclr_a0, iclr_up=iclr_up, gate_up=gate_up, k_k=k_k, k_a=k_a, r_k=r_k,
               gn_w=gn_w, gn_b=gn_b, pool_w=pool_w, pool_scale=pool_scale, q_gain=q_gain, k_gain=k_gain, rpb=rpb,
               w_rwkv_o=w_rwkv_o, w_pool_o=w_pool_o, w_nat_o=w_nat_o, w_out=w_out, w_ffn_in=w_ffn_in,
               w_ffn_out=w_ffn_out)
    bsz, len_lat, d = x.shape
    len_ctx = ctx.shape[1]
    depth = w_mod.shape[0]
    assert len_ctx % TM == 0 and len_lat % TM == 0 and bsz + 1 <= SUBLANES
    if depth == 0:
        return x
    nct = len_ctx // TM

    s_rows = jnp.concatenate([c, c_ctx[None, :], jnp.zeros((SUBLANES - bsz - 1, d), F32)], axis=0)
    mod_all = _modulation(s_rows, w_mod, b_mod)
    xa = jnp.concatenate([ctx, x], axis=1)

    for l in range(depth):
        lw = _layer_weights(l, prm)
        m_lat = mod_all[l, :bsz]
        m_ctx = jnp.broadcast_to(mod_all[l, bsz][None], m_lat.shape)
        mod = jnp.stack([m_ctx, m_lat], axis=1).reshape(bsz * 2 * 6, 1, d)

        p = _in_proj(xa, mod, lw["norm1"], lw["w_mix"], len_ctx, F32)
        gates = _in_proj(xa, mod, lw["norm1"], lw["w_gate"], len_ctx, BF16)
        ld_f, ld_b, kd_f, kd_b, b_f, b_b, v, kk, r, gd, b_br, qn, kn, vn = _mixer_prep(p, lw, nct, len_ctx, len_lat)
        y_f, y_b = _rwkv_scan(ld_f, ld_b, kd_f, kd_b, b_f, b_b, v, kk, r, len_ctx)
        c_br = _nat_attention(qn, kn, vn, lw["nat_bias"], len_ctx, len_lat)
        xa = _merge((y_f, y_b, r, kd_f, kd_b, v, gd), b_br, c_br, gates, xa, mod, lw, len_ctx)
        xa = _ffn(xa, mod, lw, len_ctx, latent_only=(l == depth - 1))
    return xa
```
